```python
import jax, jax.numpy as jnp
from jax import lax
import numpy as np

D_MODEL = 1024
BATCH = 8
SEQ = 2048
DEPTH = 2
DEC_BATCH = 128
DEC_SEQ = 8
PAST_LEN = 16384
PAGE_SIZE = 128

N_AB_LAYERS = (DEPTH + 1) // 2
N_C_LAYERS = DEPTH // 2
CHUNK = 64
EPS = 1e-6

DN_HEADS = 4
DN_DK = 128
DN_DV = 128
DN_QK = DN_HEADS * DN_DK
DN_V = DN_HEADS * DN_DV
CONV_W = 4
CONV_CH = 2 * DN_QK + DN_V
GLA_HEADS = 4
GLA_DK = 64
GLA_DV = 128
GLA_K = GLA_HEADS * GLA_DK
GLA_V = GLA_HEADS * GLA_DV
GLA_RANK = 16
GLA_NORMALIZER = 16.0
AB_SPLITS = (CONV_CH, DN_HEADS, DN_HEADS, DN_V, GLA_K, GLA_K, GLA_V, GLA_RANK, GLA_V)
AB_IN = sum(AB_SPLITS)
HG_EXPAND = 128
HG_HEADS = D_MODEL // HG_EXPAND
HG_DK = HG_EXPAND
HG_DV = D_MODEL // HG_HEADS
HG_F = HG_HEADS * HG_DK
HG_SPLITS = (HG_F, HG_F, D_MODEL, D_MODEL)
HG_IN = sum(HG_SPLITS)
N_GROUPS = 4
EXPERTS_PER_GROUP = 8
N_EXPERTS = N_GROUPS * EXPERTS_PER_GROUP
TOP_K = 2
D_EXPERT = 512

kernel_name = "hybrid_deltanet_gla_hgrn2_hmoe_step"


def _split(p, sizes):
    return jnp.split(p, np.cumsum(sizes)[:-1].tolist(), axis=-1)


def _rmsnorm(x, gain):
    xf = x.astype(jnp.float32)
    y = xf * lax.rsqrt(jnp.mean(xf * xf, axis=-1, keepdims=True) + EPS) * gain.astype(jnp.float32)
    return y.astype(x.dtype)


def _gated_rmsnorm(o, gate, gain):
    o = o * lax.rsqrt(jnp.mean(o * o, axis=-1, keepdims=True) + EPS) * gain.astype(jnp.float32)
    return o * jax.nn.silu(gate.astype(jnp.float32))


def _l2norm(x):
    return x * lax.rsqrt(jnp.sum(x * x, axis=-1, keepdims=True) + 1e-6)


def _causal_conv(u, buf, w):
    T = u.shape[1]
    full = jnp.concatenate([buf.astype(u.dtype), u], axis=1)
    out = full[:, 0:T] * w[0]
    for j in range(1, CONV_W):
        out = out + full[:, j:j + T] * w[j]
    return out, full[:, T:]


def _pad_time(x, tp):
    pad = [(0, 0)] * x.ndim
    pad[1] = (0, tp - x.shape[1])
    return jnp.pad(x, pad)


def _to_chunks(x, c):
    b, t, h = x.shape[:3]
    x = jnp.moveaxis(x, 2, 1)
    return x.reshape((b, h, t // c, c) + x.shape[3:])


def _from_chunks(x, t):
    b, h, n, c = x.shape[:4]
    x = x.reshape((b, h, n * c) + x.shape[4:])[:, :, :t]
    return jnp.moveaxis(x, 1, 2)


def _gated_delta_chunked(q, k, v, g, beta, s0):
    f32 = jnp.float32
    b_, t, h, dk = q.shape
    dv = v.shape[-1]
    c = min(CHUNK, t)
    n = -(-t // c)
    tp = n * c
    q, k, v = (_to_chunks(_pad_time(a.astype(f32), tp), c) for a in (q, k, v))
    g, beta = (_to_chunks(_pad_time(a.astype(f32), tp), c) for a in (g, beta))
    q = q * dk ** -0.5
    G = jnp.cumsum(g, axis=-1)
    incl = jnp.tril(jnp.ones((c, c), bool))
    strict = jnp.tril(jnp.ones((c, c), bool), -1)
    decay = jnp.exp(jnp.where(incl, G[..., :, None] - G[..., None, :], -jnp.inf))
    kk = jnp.einsum('bhntd,bhnjd->bhntj', k, k)
    tri = jnp.where(strict, beta[..., None] * kk * decay, 0.0) + jnp.eye(c, dtype=f32)
    rhs = jnp.concatenate([beta[..., None] * v, (beta * jnp.exp(G))[..., None] * k], axis=-1)
    sol = lax.linalg.triangular_solve(tri, rhs, left_side=True, lower=True, unit_diagonal=True)
    u0, w = sol[..., :dv], sol[..., dv:]
    qk = jnp.einsum('bhntd,bhnjd->bhntj', q, k) * decay
    qg = q * jnp.exp(G)[..., None]
    kg = k * jnp.exp(G[..., -1:] - G)[..., None]
    glast = jnp.exp(G[..., -1])
    xs = tuple(jnp.moveaxis(a, 2, 0) for a in (u0, w, qk, qg, kg, glast))

    def step(s, xc):
        u0c, wc, qkc, qgc, kgc, glc = xc
        u = u0c - jnp.einsum('bhtd,bhdv->bhtv', wc, s)
        o = jnp.einsum('bhtd,bhdv->bhtv', qgc, s) + jnp.einsum('bhtj,bhjv->bhtv', qkc, u)
        s = s * glc[..., None, None] + jnp.einsum('bhjd,bhjv->bhdv', kgc, u)
        return s, o

    s, o = lax.scan(step, s0.astype(f32), xs)
    return _from_chunks(jnp.moveaxis(o, 0, 2), t), s


def _gla_chunked(q, k, v, logf, s0):
    f32 = jnp.float32
    b_, t, h, dk = q.shape
    c = min(CHUNK, t)
    n = -(-t // c)
    tp = n * c
    q, k, v, logf = (_to_chunks(_pad_time(a.astype(f32), tp), c) for a in (q, k, v, logf))
    q = q * dk ** -0.5
    G = jnp.cumsum(logf, axis=3)
    incl = jnp.tril(jnp.ones((c, c), bool))[:, :, None]
    xs = tuple(jnp.moveaxis(a, 2, 0) for a in (q, k, v, G))

    def step(s, xc):
        qc, kc, vc, gc = xc
        decay = jnp.exp(jnp.where(incl, gc[:, :, :, None, :] - gc[:, :, None, :, :], -jnp.inf))
        att = jnp.einsum('bhtd,bhjd,bhtjd->bhtj', qc, kc, decay)
        o = jnp.einsum('bhtd,bhdv->bhtv', qc * jnp.exp(gc), s) + jnp.einsum('bhtj,bhjv->bhtv', att, vc)
        gl = gc[:, :, -1]
        s = s * jnp.exp(gl)[..., None] + jnp.einsum('bhjd,bhjv->bhdv', kc * jnp.exp(gl[:, :, None] - gc), vc)
        return s, o

    s, o = lax.scan(step, s0.astype(f32), xs)
    return _from_chunks(jnp.moveaxis(o, 0, 2), t), s


def _ab_mixer(h, conv_buf, s_dn, s_gla, w_in, conv_w, a_log, dt_bias, dn_norm, gk_w2, gk_b, gla_norm, w_out):
    f32 = jnp.float32
    b, t, _ = h.shape
    p = h @ w_in
    u, a, bb, z, gq, gk, gv, glr, gg = _split(p, AB_SPLITS)
    cv_out, new_buf = _causal_conv(u.astype(f32), conv_buf, conv_w.astype(f32))
    cq, ck, cvv = _split(jax.nn.silu(cv_out), (DN_QK, DN_QK, DN_V))
    q = _l2norm(cq.reshape(b, t, DN_HEADS, DN_DK))
    k = _l2norm(ck.reshape(b, t, DN_HEADS, DN_DK))
    v = cvv.reshape(b, t, DN_HEADS, DN_DV)
    beta = jax.nn.sigmoid(bb.astype(f32))
    g = -jnp.exp(a_log.astype(f32)) * jax.nn.softplus(a.astype(f32) + dt_bias.astype(f32))
    o_dn, s_dn_new = _gated_delta_chunked(q, k, v, g, beta, s_dn)
    o_dn = _gated_rmsnorm(o_dn, z.reshape(b, t, DN_HEADS, DN_DV), dn_norm)
    logf = jax.nn.log_sigmoid(glr.astype(f32) @ gk_w2.astype(f32) + gk_b.astype(f32)) / GLA_NORMALIZER
    o_gla, s_gla_new = _gla_chunked(gq.reshape(b, t, GLA_HEADS, GLA_DK), gk.reshape(b, t, GLA_HEADS, GLA_DK),
                                    gv.reshape(b, t, GLA_HEADS, GLA_DV), logf.reshape(b, t, GLA_HEADS, GLA_DK), s_gla)
    o_gla = _gated_rmsnorm(o_gla, gg.reshape(b, t, GLA_HEADS, GLA_DV), gla_norm)
    o = jnp.concatenate([o_dn.reshape(b, t, DN_V), o_gla.reshape(b, t, GLA_V)], axis=-1).astype(h.dtype)
    return (o @ w_out, new_buf.astype(conv_buf.dtype), s_dn_new.astype(s_dn.dtype), s_gla_new.astype(s_gla.dtype))


def _hgrn2_mixer(h, s_hg, w_in, lb, norm, w_out):
    f32 = jnp.float32
    b, t, _ = h.shape
    q, fr, i, gt = _split(h @ w_in, HG_SPLITS)
    lb = lb.reshape(HG_HEADS, HG_DK)
    fr = fr.astype(f32).reshape(b, t, HG_HEADS, HG_DK)
    logf = jnp.log(lb + (1.0 - lb) * jax.nn.sigmoid(fr))
    k = (1.0 - lb) * jax.nn.sigmoid(-fr)
    o, s_new = _gla_chunked(q.reshape(b, t, HG_HEADS, HG_DK), k, i.reshape(b, t, HG_HEADS, HG_DV), logf, s_hg)
    o = _gated_rmsnorm(o, gt.reshape(b, t, HG_HEADS, HG_DV), norm)
    return o.reshape(b, t, D_MODEL).astype(h.dtype) @ w_out, s_new.astype(s_hg.dtype)


def _hier_moe(x, gain, w_g, b_g, w_e, b_e, w1, w3, w2):
    f32 = jnp.float32
    b, t, d = x.shape
    h = _rmsnorm(x, gain).reshape(b * t, d)
    g_prob = jax.nn.softmax((h @ w_g + b_g).astype(f32), axis=-1)
    g_top = jnp.argmax(g_prob, axis=-1)
    p_g = jnp.take_along_axis(g_prob, g_top[:, None], axis=1)[:, 0]
    e_logits = (h @ w_e + b_e).astype(f32).reshape(-1, N_GROUPS, EXPERTS_PER_GROUP)
    e_sel = jnp.take_along_axis(e_logits, g_top[:, None, None], axis=1)[:, 0]
    top_w, top_i = lax.top_k(jax.nn.softmax(e_sel, axis=-1), TOP_K)
    top_w = top_w / jnp.sum(top_w, axis=-1, keepdims=True)
    weights = p_g[:, None] * top_w
    expert_idx = g_top[:, None] * EXPERTS_PER_GROUP + top_i
    gate = jnp.sum(jax.nn.one_hot(expert_idx, N_EXPERTS, dtype=f32) * weights[..., None], axis=1)

    def expert_step(acc, prm):
        w1e, w3e, w2e, ge = prm
        y = (jax.nn.silu(h @ w1e) * (h @ w3e)) @ w2e
        return acc + ge[:, None] * y, None

    out, _ = lax.scan(expert_step, jnp.zeros_like(h), (w1, w3, w2, gate.T.astype(h.dtype)))
    return out.reshape(b, t, d)


def _trunk(x, conv, delta, gla, hgrn, w):
    (mix_norm, ab_w_in, ab_conv_w, ab_a_log, ab_dt_bias, ab_dn_norm, ab_gk_w2, ab_gk_b, ab_gla_norm, ab_w_out,
     c_w_in, c_lower_bounds, c_norm, c_w_out, ffn_norm, moe_w_group, moe_b_group, moe_w_expert, moe_b_expert,
     moe_w1, moe_w3, moe_w2, final_norm) = w
    sm = jax.nn.softmax(c_lower_bounds.astype(jnp.float32), axis=0)
    lb_all = jnp.cumsum(sm, axis=0) - sm[0:1]
    new_conv, new_delta, new_gla, new_hgrn = [], [], [], []
    for layer in range(DEPTH):
        hn = _rmsnorm(x, mix_norm[layer])
        if layer % 2 == 0:
            j = layer // 2
            o, cb, sd, sg = _ab_mixer(hn, conv[j], delta[j], gla[j], ab_w_in[j], ab_conv_w[j], ab_a_log[j],
                                      ab_dt_bias[j], ab_dn_norm[j], ab_gk_w2[j], ab_gk_b[j], ab_gla_norm[j], ab_w_out[j])
            new_conv.append(cb)
            new_delta.append(sd)
            new_gla.append(sg)
        else:
            j = layer // 2
            o, sh = _hgrn2_mixer(hn, hgrn[j], c_w_in[j], lb_all[layer], c_norm[j], c_w_out[j])
            new_hgrn.append(sh)
        x = x + o
        x = x + _hier_moe(x, ffn_norm[layer], moe_w_group[layer], moe_b_group[layer], moe_w_expert[layer],
                          moe_b_expert[layer], moe_w1[layer], moe_w3[layer], moe_w2[layer])
    y = _rmsnorm(x, final_norm)
    return y, jnp.stack(new_conv), jnp.stack(new_delta), jnp.stack(new_gla), jnp.stack(new_hgrn)


def setup_inputs(seed: int = 0) -> dict:
    key = jax.random.key(seed)
    ks = iter(jax.random.split(key, 40))

    def nrm(shape, scale):
        return scale * jax.random.normal(next(ks), shape, jnp.float32)

    dt = jnp.exp(jax.random.uniform(next(ks), (N_AB_LAYERS, DN_HEADS), jnp.float32,
                                    minval=float(np.log(1e-3)), maxval=float(np.log(1e-1))))
    return {
        "x_prompt": nrm((BATCH, SEQ, D_MODEL), 1.0),
        "x_sample": nrm((DEC_BATCH, DEC_SEQ, D_MODEL), 1.0),
        "state_conv_ab": nrm((N_AB_LAYERS, DEC_BATCH, CONV_W - 1, CONV_CH), 1.0),
        "state_delta_ab": nrm((N_AB_LAYERS, DEC_BATCH, DN_HEADS, DN_DK, DN_DV), 0.1),
        "state_gla_ab": nrm((N_AB_LAYERS, DEC_BATCH, GLA_HEADS, GLA_DK, GLA_DV), 1.0),
        "state_hgrn_c": nrm((N_C_LAYERS, DEC_BATCH, HG_HEADS, HG_DK, HG_DV), 0.5),
        "mix_norm": 1.0 + nrm((DEPTH, D_MODEL), 0.01),
        "ab_w_in": nrm((N_AB_LAYERS, D_MODEL, AB_IN), D_MODEL ** -0.5),
        "ab_conv_w": nrm((N_AB_LAYERS, CONV_W, CONV_CH), CONV_W ** -0.5),
        "ab_a_log": jnp.log(jax.random.uniform(next(ks), (N_AB_LAYERS, DN_HEADS), jnp.float32, minval=1.0, maxval=16.0)),
        "ab_dt_bias": dt + jnp.log(-jnp.expm1(-dt)),
        "ab_dn_norm": 1.0 + nrm((N_AB_LAYERS, DN_DV), 0.01),
        "ab_gk_w2": nrm((N_AB_LAYERS, GLA_RANK, GLA_K), GLA_RANK ** -0.5),
        "ab_gk_b": nrm((N_AB_LAYERS, GLA_K), 0.1),
        "ab_gla_norm": 1.0 + nrm((N_AB_LAYERS, GLA_DV), 0.01),
        "ab_w_out": nrm((N_AB_LAYERS, D_MODEL, D_MODEL), D_MODEL ** -0.5),
        "c_w_in": nrm((N_C_LAYERS, D_MODEL, HG_IN), D_MODEL ** -0.5),
        "c_lower_bounds": nrm((DEPTH, HG_F), 0.1),
        "c_norm": 1.0 + nrm((N_C_LAYERS, HG_DV), 0.01),
        "c_w_out": nrm((N_C_LAYERS, D_MODEL, D_MODEL), D_MODEL ** -0.5),
        "ffn_norm": 1.0 + nrm((DEPTH, D_MODEL), 0.01),
        "moe_w_group": nrm((DEPTH, D_MODEL, N_GROUPS), D_MODEL ** -0.5),
        "moe_b_group": nrm((DEPTH, N_GROUPS), 0.01),
        "moe_w_expert": nrm((DEPTH, D_MODEL, N_EXPERTS), D_MODEL ** -0.5),
        "moe_b_expert": nrm((DEPTH, N_EXPERTS), 0.01),
        "moe_w1": nrm((DEPTH, N_EXPERTS, D_MODEL, D_EXPERT), D_MODEL ** -0.5),
        "moe_w3": nrm((DEPTH, N_EXPERTS, D_MODEL, D_EXPERT), D_MODEL ** -0.5),
        "moe_w2": nrm((DEPTH, N_EXPERTS, D_EXPERT, D_MODEL), D_EXPERT ** -0.5),
        "final_norm": 1.0 + nrm((D_MODEL,), 0.01),
    }


def reference(x_prompt, x_sample, state_conv_ab, state_delta_ab, state_gla_ab, state_hgrn_c,
              mix_norm, ab_w_in, ab_conv_w, ab_a_log, ab_dt_bias, ab_dn_norm, ab_gk_w2, ab_gk_b, ab_gla_norm,
              ab_w_out, c_w_in, c_lower_bounds, c_norm, c_w_out, ffn_norm, moe_w_group, moe_b_group,
              moe_w_expert, moe_b_expert, moe_w1, moe_w3, moe_w2, final_norm):
    w = (mix_norm, ab_w_in, ab_conv_w, ab_a_log, ab_dt_bias, ab_dn_norm, ab_gk_w2, ab_gk_b, ab_gla_norm, ab_w_out,
         c_w_in, c_lower_bounds, c_norm, c_w_out, ffn_norm, moe_w_group, moe_b_group, moe_w_expert, moe_b_expert,
         moe_w1, moe_w3, moe_w2, final_norm)
    b = x_prompt.shape[0]
    dt_ = x_prompt.dtype
    z_conv = jnp.zeros((N_AB_LAYERS, b, CONV_W - 1, CONV_CH), dt_)
    z_delta = jnp.zeros((N_AB_LAYERS, b, DN_HEADS, DN_DK, DN_DV), dt_)
    z_gla = jnp.zeros((N_AB_LAYERS, b, GLA_HEADS, GLA_DK, GLA_DV), dt_)
    z_hgrn = jnp.zeros((N_C_LAYERS, b, HG_HEADS, HG_DK, HG_DV), dt_)
    y_prompt, p_conv, p_delta, p_gla, p_hgrn = _trunk(x_prompt, z_conv, z_delta, z_gla, z_hgrn, w)
    y_sample, s_conv, s_delta, s_gla, s_hgrn = _trunk(x_sample, state_conv_ab, state_delta_ab, state_gla_ab,
                                                      state_hgrn_c, w)
    return (y_prompt, y_sample, p_conv, p_delta, p_gla, p_hgrn, s_conv, s_delta, s_gla, s_hgrn)
```

```python
import functools

import jax
import jax.numpy as jnp
from jax import lax
from jax.experimental import pallas as pl
from jax.experimental.pallas import tpu as pltpu

F32 = jnp.float32
BF16 = jnp.bfloat16
I32 = jnp.int32

D_MODEL = 1024
EPS = 1e-6
CHUNK = 64
CONV_W = 4
DN_HEADS, DN_DK, DN_DV = 4, 128, 128
DN_QK = DN_HEADS * DN_DK
DN_V = DN_HEADS * DN_DV
CONV_CH = 2 * DN_QK + DN_V
GLA_HEADS, GLA_DK, GLA_DV = 4, 64, 128
GLA_K = GLA_HEADS * GLA_DK
GLA_V = GLA_HEADS * GLA_DV
GLA_RANK = 16
GLA_NORMALIZER = 16.0
AB_SPLITS = (CONV_CH, DN_HEADS, DN_HEADS, DN_V, GLA_K, GLA_K, GLA_V, GLA_RANK, GLA_V)
HG_HEADS, HG_DK, HG_DV = 8, 128, 128
HG_F = HG_HEADS * HG_DK
N_GROUPS, EXPERTS_PER_GROUP = 4, 8
N_EXPERTS = N_GROUPS * EXPERTS_PER_GROUP
TOP_K = 2
D_EXPERT = 512

LANES = 128
SUBLANES = 8
VMEM_LIMIT_BYTES = 56 * 1024 * 1024

AB_U, AB_Z, AB_GQ, AB_GK, AB_GV, AB_GG, AB_MISC = 0, 1536, 2048, 2304, 2560, 3072, 3584
AB_COLS = AB_MISC + LANES
MISC_A, MISC_B, MISC_LR = 0, DN_HEADS, 2 * DN_HEADS

TOKEN_TILE = 256
MOE_TILE = 256
NEG = -1e30


def _params(sem):
    return pltpu.CompilerParams(dimension_semantics=sem, vmem_limit_bytes=VMEM_LIMIT_BYTES)


def _mm(a, b):
    return jnp.dot(a.astype(BF16), b.astype(BF16), preferred_element_type=F32)


def _mm_nt(a, b):
    return lax.dot_general(a.astype(BF16), b.astype(BF16), (((1,), (1,)), ((), ())), preferred_element_type=F32)


def _mm_tn(a, b):
    return lax.dot_general(a.astype(BF16), b.astype(BF16), (((0,), (0,)), ((), ())), preferred_element_type=F32)


def _mm_hi(a, b):
    return jnp.dot(a, b, preferred_element_type=F32, precision=lax.Precision.HIGHEST)


def _sigmoid(x):
    return 1.0 / (1.0 + jnp.exp(-x))


def _softplus(x):
    return jnp.maximum(x, 0.0) + jnp.log(1.0 + jnp.exp(-jnp.abs(x)))


def _rms(x, gain):
    return x * lax.rsqrt(jnp.mean(x * x, axis=-1, keepdims=True) + EPS) * gain


def _gated_rms(o, gate, gain):
    return _rms(o, gain) * (gate * _sigmoid(gate))


def _row_to_col(row):
    n = row.shape[1]
    r = lax.broadcasted_iota(I32, (n, n), 0)
    c = lax.broadcasted_iota(I32, (n, n), 1)
    return jnp.sum(jnp.where(r == c, jnp.broadcast_to(row, (n, n)), 0.0), axis=1, keepdims=True)


def _unit_lower_inverse(nm):
    c = nm.shape[0]
    r = lax.broadcasted_iota(I32, (c, c), 0)
    cc = lax.broadcasted_iota(I32, (c, c), 1)
    t = jnp.where(r == cc, 1.0, 0.0) - nm
    p = nm
    k = 2
    while k < c:
        p = _mm_hi(p, p)
        t = t + _mm_hi(t, p)
        k *= 2
    return t


def _delta_chunk(q, k, v, gc, gr, beta, s):
    c = q.shape[0]
    ti = lax.broadcasted_iota(I32, (c, c), 0)
    tj = lax.broadcasted_iota(I32, (c, c), 1)
    decay = jnp.exp(jnp.where(ti >= tj, gc - gr, NEG))
    qs = q * (DN_DK ** -0.5)
    kk = _mm_nt(k, k)
    t_inv = _unit_lower_inverse(jnp.where(ti > tj, beta * kk * decay, 0.0))
    eg = jnp.exp(gc)
    rhs = jnp.concatenate([beta * v, (beta * eg) * k], axis=1)
    sol = _mm_hi(t_inv, rhs)
    u0, w = sol[:, :DN_DV], sol[:, DN_DV:]
    qk = _mm_nt(qs, k) * decay
    gl = gc[c - 1:c]
    u = u0 - _mm(w, s)
    o = _mm(qs * eg, s) + _mm(qk, u)
    s_new = s * jnp.exp(gl) + _mm_tn(k * jnp.exp(gl - gc), u)
    return o, s_new


def _gla_chunk(q, k, v, g, s, m):
    c, dk = q.shape
    o = _mm(q * jnp.exp(g), s)
    jrow = lax.broadcasted_iota(I32, (c, 1), 0)
    lane = lax.broadcasted_iota(I32, (m, c), 1)
    sub = lax.broadcasted_iota(I32, (m, 1), 0)
    blocks = []
    for blk in range(c // m):
        r0 = blk * m
        qb, gb, kb = q[r0:r0 + m], g[r0:r0 + m], k[r0:r0 + m]
        if blk > 0:
            base = g[r0 - 1:r0]
            kt = k * jnp.exp(jnp.where(jrow < r0, base - g, NEG))
            att = _mm_nt(qb * jnp.exp(gb - base), kt)
        else:
            att = jnp.zeros((m, c), F32)
        for j in range(m):
            e = jnp.exp(jnp.where(sub >= j, gb - gb[j:j + 1], NEG))
            col = jnp.sum(qb * kb[j:j + 1] * e, axis=1, keepdims=True)
            att = jnp.where(lane == r0 + j, col, att)
        blocks.append(att)
    att = blocks[0] if len(blocks) == 1 else jnp.concatenate(blocks, axis=0)
    o = o + _mm(att, v)
    gl = g[c - 1:c]
    s_new = s * _row_to_col(jnp.exp(gl)) + _mm_tn(k * jnp.exp(gl - g), v)
    return o, s_new


def _tri_incl(c):
    r = lax.broadcasted_iota(I32, (c, c), 0)
    cc = lax.broadcasted_iota(I32, (c, c), 1)
    return jnp.where(r >= cc, 1.0, 0.0).astype(F32)


def _norm_proj_kernel(*refs, add_slots):
    if add_slots:
        x_ref, s0_ref, s1_ref, gain_ref, w_ref, xs_ref, p_ref = refs
        x = x_ref[...] + (s0_ref[...] + s1_ref[...])
        xs_ref[...] = x
    else:
        x_ref, gain_ref, w_ref, p_ref = refs
        x = x_ref[...]
    h = _rms(x, gain_ref[...])
    p_ref[...] = jnp.dot(h.astype(BF16), w_ref[...], preferred_element_type=F32)


def _norm_proj(x, slots, gain, w):
    n, d = x.shape
    m = w.shape[1]
    tm = TOKEN_TILE
    row = pl.BlockSpec((tm, d), lambda i: (i, 0))
    const = lambda shape: pl.BlockSpec(shape, lambda i: (0,) * len(shape))
    in_specs = [row]
    args = [x]
    if slots is not None:
        in_specs += [row, pl.BlockSpec((tm, d), lambda i: (i + n // tm, 0))]
        args += [slots, slots]
    in_specs += [const((1, d)), const((d, m))]
    args += [gain, w]
    out_shape = [jax.ShapeDtypeStruct((n, m), F32)]
    out_specs = [pl.BlockSpec((tm, m), lambda i: (i, 0))]
    if slots is not None:
        out_shape = [jax.ShapeDtypeStruct((n, d), F32)] + out_shape
        out_specs = [row] + out_specs
    return pl.pallas_call(
        functools.partial(_norm_proj_kernel, add_slots=slots is not None),
        grid=(n // tm,),
        in_specs=in_specs,
        out_specs=out_specs,
        out_shape=out_shape,
        compiler_params=_params(("parallel",)),
        name="norm_proj",
    )(*args)


def _ab_mixer_kernel(*refs, c, nseq, zero_init):
    if zero_init:
        p_ref, cw_ref, prm_ref, gkw_ref, gkb_ref, o_ref, conv_ref, sdn_ref, sgla_ref = refs
    else:
        (p_ref, cw_ref, prm_ref, gkw_ref, gkb_ref, conv_in, sdn_in, sgla_in,
         o_ref, conv_ref, sdn_ref, sgla_ref) = refs

    @pl.when(pl.program_id(1) == 0)
    def _():
        if zero_init:
            conv_ref[...] = jnp.zeros_like(conv_ref)
            sdn_ref[...] = jnp.zeros_like(sdn_ref)
            sgla_ref[...] = jnp.zeros_like(sgla_ref)
        else:
            conv_ref[...] = conv_in[...]
            sdn_ref[...] = sdn_in[...]
            sgla_ref[...] = sgla_in[...]

    m = min(16, c)
    tri = _tri_incl(c)
    cw = cw_ref[...]
    a_log, dt_bias = prm_ref[0:1], prm_ref[1:2]
    dn_gain, gla_gain = prm_ref[2:3], prm_ref[3:4]
    for b in range(nseq):
        r0 = b * c
        u = p_ref[r0:r0 + c, AB_U:AB_U + CONV_CH]
        ucat = jnp.concatenate([conv_ref[b], u], axis=0)
        acc = u * cw[CONV_W - 1:CONV_W]
        for j in range(1, CONV_W):
            acc = acc + pltpu.roll(ucat, j, 0)[SUBLANES:SUBLANES + c] * cw[CONV_W - 1 - j:CONV_W - j]
        conv_ref[b] = ucat[c:c + SUBLANES]
        qkv = acc * _sigmoid(acc)

        misc = p_ref[r0:r0 + c, AB_MISC:AB_MISC + LANES]
        g_all = -jnp.exp(a_log) * _softplus(misc + dt_bias)
        beta_all = _sigmoid(misc)
        gcum = _mm_hi(tri, g_all)
        gcum_t = jnp.concatenate([gcum, jnp.zeros((LANES - c, LANES), F32)], axis=0).T

        for h in range(DN_HEADS):
            lo = h * DN_DK
            q = qkv[:, lo:lo + DN_DK]
            k = qkv[:, DN_QK + lo:DN_QK + lo + DN_DK]
            v = qkv[:, 2 * DN_QK + h * DN_DV:2 * DN_QK + (h + 1) * DN_DV]
            q = q * lax.rsqrt(jnp.sum(q * q, axis=-1, keepdims=True) + 1e-6)
            k = k * lax.rsqrt(jnp.sum(k * k, axis=-1, keepdims=True) + 1e-6)
            o, s_new = _delta_chunk(q, k, v, gcum[:, MISC_A + h:MISC_A + h + 1], gcum_t[MISC_A + h:MISC_A + h + 1, :c],
                                    beta_all[:, MISC_B + h:MISC_B + h + 1], sdn_ref[b, h])
            sdn_ref[b, h] = s_new
            z = p_ref[r0:r0 + c, AB_Z + h * DN_DV:AB_Z + (h + 1) * DN_DV]
            o_ref[r0:r0 + c, h * DN_DV:(h + 1) * DN_DV] = _gated_rms(o, z, dn_gain).astype(o_ref.dtype)

        logits = _mm_hi(misc, gkw_ref[...]) + gkb_ref[...]
        logf = -_softplus(-logits) * (1.0 / GLA_NORMALIZER)
        gall = _mm_hi(tri, logf)
        for h in range(GLA_HEADS):
            lo = h * GLA_DK
            q = p_ref[r0:r0 + c, AB_GQ + lo:AB_GQ + lo + GLA_DK] * (GLA_DK ** -0.5)
            k = p_ref[r0:r0 + c, AB_GK + lo:AB_GK + lo + GLA_DK]
            v = p_ref[r0:r0 + c, AB_GV + h * GLA_DV:AB_GV + (h + 1) * GLA_DV]
            o, s_new = _gla_chunk(q, k, v, gall[:, lo:lo + GLA_DK], sgla_ref[b, h], m)
            sgla_ref[b, h] = s_new
            gate = p_ref[r0:r0 + c, AB_GG + h * GLA_DV:AB_GG + (h + 1) * GLA_DV]
            o_ref[r0:r0 + c, DN_V + h * GLA_DV:DN_V + (h + 1) * GLA_DV] = _gated_rms(o, gate, gla_gain).astype(o_ref.dtype)


def _ab_mixer(p, o_alias_rows, row0, nb, t, nseq, cw, prm, gkw, gkb, states):
    c = min(CHUNK, t)
    nchunk = t // c
    rows = nseq * c
    blk0 = row0 // rows
    n_groups = nb // nseq
    zero_init = states is None
    row_map = lambda g, i: (blk0 + g * nchunk + i, 0)
    st_map3 = lambda g, i: (g, 0, 0)
    st_map4 = lambda g, i: (g, 0, 0, 0)
    const = lambda shape: pl.BlockSpec(shape, lambda g, i: (0,) * len(shape))
    in_specs = [pl.BlockSpec((rows, AB_COLS), row_map), const(cw.shape), const(prm.shape), const(gkw.shape),
                const(gkb.shape)]
    args = [p, cw, prm, gkw, gkb]
    st_specs = [pl.BlockSpec((nseq, SUBLANES, CONV_CH), st_map3),
                pl.BlockSpec((nseq, DN_HEADS, DN_DK, DN_DV), st_map4),
                pl.BlockSpec((nseq, GLA_HEADS, GLA_DK, GLA_DV), st_map4)]
    if not zero_init:
        in_specs += st_specs
        args += list(states)
    out_shape = [jax.ShapeDtypeStruct((o_alias_rows, D_MODEL), BF16),
                 jax.ShapeDtypeStruct((nb, SUBLANES, CONV_CH), F32),
                 jax.ShapeDtypeStruct((nb, DN_HEADS, DN_DK, DN_DV), F32),
                 jax.ShapeDtypeStruct((nb, GLA_HEADS, GLA_DK, GLA_DV), F32)]
    out_specs = [pl.BlockSpec((rows, D_MODEL), lambda g, i: (g * nchunk + i, 0))] + st_specs
    return pl.pallas_call(
        functools.partial(_ab_mixer_kernel, c=c, nseq=nseq, zero_init=zero_init),
        grid=(n_groups, nchunk),
        in_specs=in_specs,
        out_specs=out_specs,
        out_shape=out_shape,
        compiler_params=_params(("parallel", "arbitrary")),
        name="ab_mixer",
    )(*args)


def _hgrn_mixer_kernel(*refs, c, nseq, zero_init, layer):
    if zero_init:
        p_ref, lb_ref, gain_ref, o_ref, s_ref = refs
    else:
        p_ref, lb_ref, gain_ref, s_in, o_ref, s_ref = refs

    @pl.when(pl.program_id(1) == 0)
    def _():
        if zero_init:
            s_ref[...] = jnp.zeros_like(s_ref)
        else:
            s_ref[...] = s_in[...]

    raw = lb_ref[...]
    e = jnp.exp(raw - jnp.max(raw, axis=0, keepdims=True))
    sm = e / jnp.sum(e, axis=0, keepdims=True)
    cum = sm[0:1]
    for l in range(1, layer + 1):
        cum = cum + sm[l:l + 1]
    lb = cum - sm[0:1]

    m = min(16, c)
    tri = _tri_incl(c)
    gain = gain_ref[...]
    for b in range(nseq):
        r0 = b * c
        fr = p_ref[r0:r0 + c, HG_F:2 * HG_F]
        logf = jnp.log(lb + (1.0 - lb) * _sigmoid(fr))
        kall = (1.0 - lb) * _sigmoid(-fr)
        gall = _mm_hi(tri, logf)
        for h in range(HG_HEADS):
            lo = h * HG_DK
            q = p_ref[r0:r0 + c, lo:lo + HG_DK] * (HG_DK ** -0.5)
            v = p_ref[r0:r0 + c, 2 * HG_F + h * HG_DV:2 * HG_F + (h + 1) * HG_DV]
            o, s_new = _gla_chunk(q, kall[:, lo:lo + HG_DK], v, gall[:, lo:lo + HG_DK], s_ref[b, h], m)
            s_ref[b, h] = s_new
            gate = p_ref[r0:r0 + c, 2 * HG_F + D_MODEL + h * HG_DV:2 * HG_F + D_MODEL + (h + 1) * HG_DV]
            o_ref[r0:r0 + c, h * HG_DV:(h + 1) * HG_DV] = _gated_rms(o, gate, gain).astype(o_ref.dtype)


def _hgrn_mixer(p, row0, nb, t, nseq, lb, gain, state, layer):
    c = min(CHUNK, t)
    nchunk = t // c
    rows = nseq * c
    blk0 = row0 // rows
    zero_init = state is None
    cols = p.shape[1]
    const = lambda shape: pl.BlockSpec(shape, lambda g, i: (0,) * len(shape))
    st_spec = pl.BlockSpec((nseq, HG_HEADS, HG_DK, HG_DV), lambda g, i: (g, 0, 0, 0))
    in_specs = [pl.BlockSpec((rows, cols), lambda g, i: (blk0 + g * nchunk + i, 0)), const(lb.shape), const(gain.shape)]
    args = [p, lb, gain]
    if not zero_init:
        in_specs.append(st_spec)
        args.append(state)
    return pl.pallas_call(
        functools.partial(_hgrn_mixer_kernel, c=c, nseq=nseq, zero_init=zero_init, layer=layer),
        grid=(nb // nseq, nchunk),
        in_specs=in_specs,
        out_specs=[pl.BlockSpec((rows, D_MODEL), lambda g, i: (g * nchunk + i, 0)), st_spec],
        out_shape=[jax.ShapeDtypeStruct((nb * t, D_MODEL), BF16),
                   jax.ShapeDtypeStruct((nb, HG_HEADS, HG_DK, HG_DV), F32)],
        compiler_params=_params(("parallel", "arbitrary")),
        name="hgrn_mixer",
    )(*args)


def _outproj_router_kernel(o_ref, x_ref, wout_ref, gain_ref, wr_ref, br_ref, x1_ref, h_ref, info_ref):
    x1 = x_ref[...] + jnp.dot(o_ref[...], wout_ref[...], preferred_element_type=F32)
    x1_ref[...] = x1
    h = _rms(x1, gain_ref[...])
    h_ref[...] = h
    logit = _mm_hi(h, wr_ref[...]) + br_ref[...]
    lane = lax.broadcasted_iota(I32, logit.shape, 1)
    far = jnp.int32(LANES)
    lg = jnp.where(lane < N_GROUPS, logit, NEG)
    mg = jnp.max(lg, axis=-1, keepdims=True)
    p_g = 1.0 / jnp.sum(jnp.exp(lg - mg), axis=-1, keepdims=True)
    g_top = jnp.min(jnp.where(lg == mg, lane, far), axis=-1, keepdims=True)
    lo = N_GROUPS + g_top * EXPERTS_PER_GROUP
    sel = jnp.logical_and(lane >= lo, lane < lo + EXPERTS_PER_GROUP)
    le = jnp.where(sel, logit, NEG)
    ee = jnp.exp(le - jnp.max(le, axis=-1, keepdims=True))
    pe = jnp.where(sel, ee / jnp.sum(ee, axis=-1, keepdims=True), -1.0)
    p1 = jnp.max(pe, axis=-1, keepdims=True)
    i1 = jnp.min(jnp.where(pe == p1, lane, far), axis=-1, keepdims=True)
    pe2 = jnp.where(lane == i1, -1.0, pe)
    p2 = jnp.max(pe2, axis=-1, keepdims=True)
    i2 = jnp.min(jnp.where(pe2 == p2, lane, far), axis=-1, keepdims=True)
    denom = p1 + p2
    w1 = p_g * (p1 / denom)
    w2 = p_g * (p2 / denom)
    info = jnp.where(lane == 0, (i1 - N_GROUPS).astype(F32),
                     jnp.where(lane == 1, (i2 - N_GROUPS).astype(F32),
                               jnp.where(lane == 2, w1, jnp.where(lane == 3, w2, 0.0))))
    info_ref[...] = info


def _outproj_router(o, x, wout, gain, wr, br):
    n, d = x.shape
    tm = TOKEN_TILE
    row = lambda w: pl.BlockSpec((tm, w), lambda i: (i, 0))
    const = lambda shape: pl.BlockSpec(shape, lambda i: (0,) * len(shape))
    return pl.pallas_call(
        _outproj_router_kernel,
        grid=(n // tm,),
        in_specs=[row(d), row(d), const((d, d)), const((1, d)), const((d, LANES)), const((1, LANES))],
        out_specs=[row(d), row(d), row(LANES)],
        out_shape=[jax.ShapeDtypeStruct((n, d), F32), jax.ShapeDtypeStruct((n, d), F32),
                   jax.ShapeDtypeStruct((n, LANES), F32)],
        compiler_params=_params(("parallel",)),
        name="outproj_router",
    )(o, x, wout, gain, wr, br)


def _moe_kernel(te_ref, nv_ref, src_ref, dst_ref,
                wrow_ref, w1_ref, w3_ref, w2_ref, h_hbm,
                out_hbm,
                xbuf, ybuf, w1b, w3b, w2b, gsem, ssem):
    i = pl.program_id(0)
    n = pl.num_programs(0)
    tm = xbuf.shape[1]
    slot = i % 2
    prev = jnp.maximum(i - 1, 0)
    rows_of = lambda tile: pl.multiple_of(nv_ref[tile], SUBLANES)
    cnt = rows_of(i)

    def gather_start(tile, s):
        def body(r, carry):
            tok = src_ref[tile * tm + r]
            pltpu.make_async_copy(h_hbm.at[pl.ds(tok, 1)], xbuf.at[s, pl.ds(r, 1)], gsem.at[s]).start()
            return carry
        lax.fori_loop(0, rows_of(tile), body, 0)

    def gather_wait(rows, s):
        pltpu.make_async_copy(h_hbm.at[pl.ds(0, rows)], xbuf.at[s, pl.ds(0, rows)], gsem.at[s]).wait()

    def scatter_wait(rows):
        pltpu.make_async_copy(ybuf.at[pl.ds(0, rows)], out_hbm.at[pl.ds(0, rows)], ssem.at[0]).wait()

    @pl.when(i == 0)
    def _():
        xbuf[...] = jnp.zeros_like(xbuf)
        sink = pltpu.make_async_copy(xbuf.at[0, pl.ds(0, SUBLANES)],
                                     out_hbm.at[pl.ds(out_hbm.shape[0] - SUBLANES, SUBLANES)], ssem.at[0])
        sink.start()
        sink.wait()
        gather_start(0, 0)

    @pl.when(i + 1 < n)
    def _():
        gather_start(i + 1, 1 - slot)

    @pl.when(cnt > 0)
    def _():
        gather_wait(cnt, slot)

        @pl.when(jnp.logical_or(i == 0, te_ref[i] != te_ref[prev]))
        def _():
            w1b[...] = w1_ref[0].astype(BF16)
            w3b[...] = w3_ref[0].astype(BF16)
            w2b[...] = w2_ref[0].astype(BF16)

        xb = xbuf[slot].astype(BF16)
        h1 = jnp.dot(xb, w1b[...], preferred_element_type=F32)
        h3 = jnp.dot(xb, w3b[...], preferred_element_type=F32)
        act = (h1 * _sigmoid(h1)) * h3
        y = jnp.dot(act.astype(BF16), w2b[...], preferred_element_type=F32) * wrow_ref[...]

        @pl.when(i > 0)
        def _():
            scatter_wait(rows_of(prev))

        ybuf[...] = y

        def body(r, carry):
            row = dst_ref[i * tm + r]
            pltpu.make_async_copy(ybuf.at[pl.ds(r, 1)], out_hbm.at[pl.ds(row, 1)], ssem.at[0]).start()
            return carry
        lax.fori_loop(0, cnt, body, 0)

        @pl.when(i == n - 1)
        def _():
            scatter_wait(cnt)

    @pl.when(jnp.logical_and(cnt == 0, i > 0))
    def _():
        @pl.when(nv_ref[prev] > 0)
        def _():
            scatter_wait(rows_of(prev))


def _moe(h, te, nv, src, dst, wrow, w1, w3, w2):
    n, d = h.shape
    f = w1.shape[2]
    tm = MOE_TILE
    n_tiles = te.shape[0]
    grid_spec = pltpu.PrefetchScalarGridSpec(
        num_scalar_prefetch=4,
        grid=(n_tiles,),
        in_specs=[
            pl.BlockSpec((tm, 1), lambda i, te, nv, src, dst: (i, 0)),
            pl.BlockSpec((1, d, f), lambda i, te, nv, src, dst: (te[i], 0, 0)),
            pl.BlockSpec((1, d, f), lambda i, te, nv, src, dst: (te[i], 0, 0)),
            pl.BlockSpec((1, f, d), lambda i, te, nv, src, dst: (te[i], 0, 0)),
            pl.BlockSpec(memory_space=pl.ANY),
        ],
        out_specs=pl.BlockSpec(memory_space=pl.ANY),
        scratch_shapes=[
            pltpu.VMEM((2, tm, d), F32),
            pltpu.VMEM((tm, d), F32),
            pltpu.VMEM((d, f), BF16),
            pltpu.VMEM((d, f), BF16),
            pltpu.VMEM((f, d), BF16),
            pltpu.SemaphoreType.DMA((2,)),
            pltpu.SemaphoreType.DMA((1,)),
        ],
    )
    return pl.pallas_call(
        _moe_kernel,
        grid_spec=grid_spec,
        out_shape=jax.ShapeDtypeStruct((TOP_K * n + SUBLANES, d), F32),
        compiler_params=_params(("arbitrary",)),
        name="moe_experts",
    )(te, nv, src, dst, wrow, w1, w3, w2, h)


def _route_tables(info, n, tm):
    npair = TOP_K * n
    n_tiles = (npair + N_EXPERTS * (tm - 1)) // tm
    e_flat = info[:, 0:TOP_K].astype(I32).T.reshape(npair)
    w_flat = info[:, TOP_K:2 * TOP_K].T.reshape(npair)
    counts = jnp.sum((e_flat[:, None] == jnp.arange(N_EXPERTS, dtype=I32)[None, :]).astype(I32), axis=0)
    order = jnp.argsort(e_flat, stable=True).astype(I32)
    padded = ((counts + tm - 1) // tm) * tm
    cum_padded = jnp.cumsum(padded)
    off_padded = cum_padded - padded
    off_sorted = jnp.cumsum(counts) - counts
    tile_start = jnp.arange(n_tiles, dtype=I32) * tm
    active = tile_start < cum_padded[-1]
    te = jnp.searchsorted(cum_padded, tile_start, side="right").astype(I32)
    last_active = jnp.maximum(cum_padded[-1] // tm - 1, 0)
    te = jnp.where(active, jnp.minimum(te, N_EXPERTS - 1), te[last_active])
    nv = jnp.where(active, jnp.clip(counts[te] - (tile_start - off_padded[te]), 0, tm), 0).astype(I32)
    r = jnp.arange(n_tiles * tm, dtype=I32)
    tile = r // tm
    j = r - tile * tm
    valid = j < nv[tile]
    pair = order[jnp.clip(off_sorted[te[tile]] + (tile_start[tile] - off_padded[te[tile]]) + j, 0, npair - 1)]
    src = jnp.where(valid, pair % n, 0).astype(I32)
    dst = jnp.where(valid, pair, npair + j % SUBLANES).astype(I32)
    wrow = jnp.where(valid, w_flat[pair], 0.0).reshape(n_tiles * tm, 1)
    nv = ((nv + SUBLANES - 1) // SUBLANES) * SUBLANES
    return te, nv, src, dst, wrow


def _final_norm_kernel(x_ref, s0_ref, s1_ref, gain_ref, y_ref):
    y_ref[...] = _rms(x_ref[...] + (s0_ref[...] + s1_ref[...]), gain_ref[...])


def _final_norm(x, slots, gain):
    n, d = x.shape
    tm = TOKEN_TILE
    row = pl.BlockSpec((tm, d), lambda i: (i, 0))
    return pl.pallas_call(
        _final_norm_kernel,
        grid=(n // tm,),
        in_specs=[row, row, pl.BlockSpec((tm, d), lambda i: (i + n // tm, 0)), pl.BlockSpec((1, d), lambda i: (0, 0))],
        out_specs=row,
        out_shape=jax.ShapeDtypeStruct((n, d), F32),
        compiler_params=_params(("parallel",)),
        name="final_norm",
    )(x, slots, slots, gain)


def _pad_lanes(a, width=LANES):
    return jnp.pad(a, [(0, 0)] * (a.ndim - 1) + [(0, width - a.shape[-1])])


def _sample_group(nb):
    g = 8
    while nb % g:
        g //= 2
    return g


def kernel(x_prompt, x_sample, state_conv_ab, state_delta_ab, state_gla_ab, state_hgrn_c, mix_norm, ab_w_in, ab_conv_w, ab_a_log, ab_dt_bias, ab_dn_norm, ab_gk_w2, ab_gk_b, ab_gla_norm, ab_w_out, c_w_in, c_lower_bounds, c_norm, c_w_out, ffn_norm, moe_w_group, moe_b_group, moe_w_expert, moe_b_expert, moe_w1, moe_w3, moe_w2, final_norm):
    bp, tp, d = x_prompt.shape
    bs, ts, _ = x_sample.shape
    n_p, n_s = bp * tp, bs * ts
    n = n_p + n_s
    depth = mix_norm.shape[0]
    x = jnp.concatenate([x_prompt.reshape(n_p, d), x_sample.reshape(n_s, d)], axis=0)
    sg = _sample_group(bs)

    slots = None
    conv_p, delta_p, gla_p, hgrn_p = [], [], [], []
    conv_s, delta_s, gla_s, hgrn_s = [], [], [], []
    for layer in range(depth):
        j = layer // 2
        gain = mix_norm[layer].reshape(1, d)
        if layer % 2 == 0:
            u_w, a_w, b_w, z_w, gq_w, gk_w, gv_w, lr_w, gg_w = jnp.split(ab_w_in[j], _split_points(AB_SPLITS), axis=1)
            w_in = jnp.concatenate([u_w, z_w, gq_w, gk_w, gv_w, gg_w, _pad_lanes(jnp.concatenate([a_w, b_w, lr_w], axis=1))],
                                   axis=1).astype(BF16)
        else:
            w_in = c_w_in[j].astype(BF16)
        res = _norm_proj(x, slots, gain, w_in)
        if slots is None:
            (p,) = res
        else:
            x, p = res

        if layer % 2 == 0:
            prm = jnp.concatenate([_pad_lanes(ab_a_log[j].reshape(1, -1)), _pad_lanes(ab_dt_bias[j].reshape(1, -1)),
                                   ab_dn_norm[j].reshape(1, -1), ab_gla_norm[j].reshape(1, -1),
                                   jnp.zeros((4, LANES), F32)], axis=0)
            gkw = jnp.zeros((LANES, GLA_K), F32).at[MISC_LR:MISC_LR + GLA_RANK].set(ab_gk_w2[j])
            gkb = ab_gk_b[j].reshape(1, GLA_K)
            cw = ab_conv_w[j]
            o_p, c8, sd, sgl = _ab_mixer(p, n_p, 0, bp, tp, 1, cw, prm, gkw, gkb, None)
            conv_p.append(c8[:, SUBLANES - (CONV_W - 1):])
            delta_p.append(sd)
            gla_p.append(sgl)
            conv8 = jnp.pad(state_conv_ab[j], ((0, 0), (SUBLANES - (CONV_W - 1), 0), (0, 0)))
            o_s, c8, sd, sgl = _ab_mixer(p, n_s, n_p, bs, ts, sg, cw, prm, gkw, gkb,
                                         (conv8, state_delta_ab[j], state_gla_ab[j]))
            conv_s.append(c8[:, SUBLANES - (CONV_W - 1):])
            delta_s.append(sd)
            gla_s.append(sgl)
            w_out = ab_w_out[j]
        else:
            hg_gain = c_norm[j].reshape(1, -1)
            o_p, sh = _hgrn_mixer(p, 0, bp, tp, 1, c_lower_bounds, hg_gain, None, layer)
            hgrn_p.append(sh)
            o_s, sh = _hgrn_mixer(p, n_p, bs, ts, sg, c_lower_bounds, hg_gain, state_hgrn_c[j], layer)
            hgrn_s.append(sh)
            w_out = c_w_out[j]
        o = jnp.concatenate([o_p, o_s], axis=0)

        wr = _pad_lanes(jnp.concatenate([moe_w_group[layer], moe_w_expert[layer]], axis=1))
        br = _pad_lanes(jnp.concatenate([moe_b_group[layer], moe_b_expert[layer]]).reshape(1, -1))
        x, hn, info = _outproj_router(o, x, w_out.astype(BF16), ffn_norm[layer].reshape(1, d), wr, br)
        te, nv, src, dst, wrow = _route_tables(info, n, MOE_TILE)
        slots = _moe(hn, te, nv, src, dst, wrow, moe_w1[layer], moe_w3[layer], moe_w2[layer])

    y = _final_norm(x, slots, final_norm.reshape(1, d))
    y_prompt = y[:n_p].reshape(bp, tp, d)
    y_sample = y[n_p:].reshape(bs, ts, d)
    return (y_prompt, y_sample, jnp.stack(conv_p), jnp.stack(delta_p), jnp.stack(gla_p), jnp.stack(hgrn_p),
            jnp.stack(conv_s), jnp.stack(delta_s), jnp.stack(gla_s), jnp.stack(hgrn_s))


def _split_points(sizes):
    pts, acc = [], 0
    for s in sizes[:-1]:
        acc += s
        pts.append(acc)
    return pts
```

```python
import functools

import jax
import jax.numpy as jnp
from jax import lax
from jax.experimental import pallas as pl
from jax.experimental.pallas import tpu as pltpu

F32 = jnp.float32
BF16 = jnp.bfloat16
I32 = jnp.int32

D_MODEL = 1024
EPS = 1e-6
CHUNK = 64
CONV_W = 4
DN_HEADS, DN_DK, DN_DV = 4, 128, 128
DN_QK = DN_HEADS * DN_DK
DN_V = DN_HEADS * DN_DV
CONV_CH = 2 * DN_QK + DN_V
GLA_HEADS, GLA_DK, GLA_DV = 4, 64, 128
GLA_K = GLA_HEADS * GLA_DK
GLA_V = GLA_HEADS * GLA_DV
GLA_RANK = 16
GLA_NORMALIZER = 16.0
AB_SPLITS = (CONV_CH, DN_HEADS, DN_HEADS, DN_V, GLA_K, GLA_K, GLA_V, GLA_RANK, GLA_V)
HG_HEADS, HG_DK, HG_DV = 8, 128, 128
HG_F = HG_HEADS * HG_DK
N_GROUPS, EXPERTS_PER_GROUP = 4, 8
N_EXPERTS = N_GROUPS * EXPERTS_PER_GROUP
TOP_K = 2
D_EXPERT = 512

LANES = 128
SUBLANES = 8
VMEM_LIMIT_BYTES = 56 * 1024 * 1024

AB_U, AB_Z, AB_GQ, AB_GK, AB_GV, AB_GG, AB_MISC = 0, 1536, 2048, 2304, 2560, 3072, 3584
AB_COLS = AB_MISC + LANES
MISC_A, MISC_B, MISC_LR = 0, DN_HEADS, 2 * DN_HEADS

INFO_E, INFO_W, INFO_R = 0, TOP_K, 2 * TOP_K

TOKEN_TILE = 256
MOE_TILE = 256
NEG = -1e30


def _params(sem):
    return pltpu.CompilerParams(dimension_semantics=sem, vmem_limit_bytes=VMEM_LIMIT_BYTES)


def _mm(a, b):
    return jnp.dot(a.astype(BF16), b.astype(BF16), preferred_element_type=F32)


def _mm_nt(a, b):
    return lax.dot_general(a.astype(BF16), b.astype(BF16), (((1,), (1,)), ((), ())), preferred_element_type=F32)


def _mm_tn(a, b):
    return lax.dot_general(a.astype(BF16), b.astype(BF16), (((0,), (0,)), ((), ())), preferred_element_type=F32)


def _mm_hi(a, b):
    return jnp.dot(a, b, preferred_element_type=F32, precision=lax.Precision.HIGHEST)


def _sigmoid(x):
    return 1.0 / (1.0 + jnp.exp(-x))


def _softplus(x):
    return jnp.maximum(x, 0.0) + jnp.log(1.0 + jnp.exp(-jnp.abs(x)))


def _rms(x, gain):
    return x * lax.rsqrt(jnp.mean(x * x, axis=-1, keepdims=True) + EPS) * gain


def _gated_rms(o, gate, gain):
    return _rms(o, gain) * (gate * _sigmoid(gate))


def _row_to_col(row):
    n = row.shape[1]
    r = lax.broadcasted_iota(I32, (n, n), 0)
    c = lax.broadcasted_iota(I32, (n, n), 1)
    return jnp.sum(jnp.where(r == c, jnp.broadcast_to(row, (n, n)), 0.0), axis=1, keepdims=True)


def _unit_lower_inverse(nm):
    c = nm.shape[0]
    r = lax.broadcasted_iota(I32, (c, c), 0)
    cc = lax.broadcasted_iota(I32, (c, c), 1)
    t = jnp.where(r == cc, 1.0, 0.0) - nm
    p = nm
    k = 2
    while k < c:
        p = _mm_hi(p, p)
        t = t + _mm_hi(t, p)
        k *= 2
    return t


def _delta_chunk(q, k, v, gc, gr, beta, s):
    c = q.shape[0]
    ti = lax.broadcasted_iota(I32, (c, c), 0)
    tj = lax.broadcasted_iota(I32, (c, c), 1)
    decay = jnp.exp(jnp.where(ti >= tj, gc - gr, NEG))
    qs = q * (DN_DK ** -0.5)
    kk = _mm_nt(k, k)
    t_inv = _unit_lower_inverse(jnp.where(ti > tj, beta * kk * decay, 0.0))
    eg = jnp.exp(gc)
    rhs = jnp.concatenate([beta * v, (beta * eg) * k], axis=1)
    sol = _mm_hi(t_inv, rhs)
    u0, w = sol[:, :DN_DV], sol[:, DN_DV:]
    qk = _mm_nt(qs, k) * decay
    gl = gc[c - 1:c]
    u = u0 - _mm(w, s)
    o = _mm(qs * eg, s) + _mm(qk, u)
    s_new = s * jnp.exp(gl) + _mm_tn(k * jnp.exp(gl - gc), u)
    return o, s_new


def _gla_chunk(q, k, v, g, s, m):
    c, dk = q.shape
    o = _mm(q * jnp.exp(g), s)
    jrow = lax.broadcasted_iota(I32, (c, 1), 0)
    lane = lax.broadcasted_iota(I32, (m, c), 1)
    sub = lax.broadcasted_iota(I32, (m, 1), 0)
    blocks = []
    for blk in range(c // m):
        r0 = blk * m
        qb, gb, kb = q[r0:r0 + m], g[r0:r0 + m], k[r0:r0 + m]
        if blk > 0:
            base = g[r0 - 1:r0]
            kt = k * jnp.exp(jnp.where(jrow < r0, base - g, NEG))
            att = _mm_nt(qb * jnp.exp(gb - base), kt)
        else:
            att = jnp.zeros((m, c), F32)
        for j in range(m):
            e = jnp.exp(jnp.where(sub >= j, gb - gb[j:j + 1], NEG))
            col = jnp.sum(qb * kb[j:j + 1] * e, axis=1, keepdims=True)
            att = jnp.where(lane == r0 + j, col, att)
        blocks.append(att)
    att = blocks[0] if len(blocks) == 1 else jnp.concatenate(blocks, axis=0)
    o = o + _mm(att, v)
    gl = g[c - 1:c]
    s_new = s * _row_to_col(jnp.exp(gl)) + _mm_tn(k * jnp.exp(gl - g), v)
    return o, s_new


def _tri_incl(c):
    r = lax.broadcasted_iota(I32, (c, c), 0)
    cc = lax.broadcasted_iota(I32, (c, c), 1)
    return jnp.where(r >= cc, 1.0, 0.0).astype(F32)


def _norm_proj_kernel(x_ref, gain_ref, w_ref, p_ref):
    h = _rms(x_ref[...], gain_ref[...])
    p_ref[...] = jnp.dot(h.astype(BF16), w_ref[...], preferred_element_type=F32)


def _norm_proj(x, gain, w):
    n, d = x.shape
    m = w.shape[1]
    tm = TOKEN_TILE
    return pl.pallas_call(
        _norm_proj_kernel,
        grid=(n // tm,),
        in_specs=[pl.BlockSpec((tm, d), lambda i: (i, 0)), pl.BlockSpec((1, d), lambda i: (0, 0)),
                  pl.BlockSpec((d, m), lambda i: (0, 0))],
        out_specs=pl.BlockSpec((tm, m), lambda i: (i, 0)),
        out_shape=jax.ShapeDtypeStruct((n, m), F32),
        compiler_params=_params(("parallel",)),
        name="norm_proj",
    )(x, gain, w)


def _combine_norm_kernel(pos_ref, *refs, project):
    if project:
        x_ref, info_ref, gain_ref, w_ref, ys_hbm, xs_ref, out_ref, ybuf, sem = refs
    else:
        x_ref, info_ref, gain_ref, ys_hbm, out_ref, ybuf, sem = refs
    i = pl.program_id(0)
    n = pl.num_programs(0)
    tm = x_ref.shape[0]
    slot = i % 2

    def gather_start(tile, s):
        def body(r, carry):
            for k in range(TOP_K):
                row = pos_ref[TOP_K * (tile * tm + r) + k]
                pltpu.make_async_copy(ys_hbm.at[pl.ds(row, 1)], ybuf.at[s, k, pl.ds(r, 1)], sem.at[s]).start()
            return carry
        lax.fori_loop(0, tm, body, 0, unroll=8)

    @pl.when(i == 0)
    def _():
        gather_start(0, 0)

    @pl.when(i + 1 < n)
    def _():
        gather_start(i + 1, 1 - slot)

    for k in range(TOP_K):
        pltpu.make_async_copy(ys_hbm.at[pl.ds(0, tm)], ybuf.at[slot, k], sem.at[slot]).wait()
    info = info_ref[...]
    x = x_ref[...] + (info[:, INFO_W:INFO_W + 1] * ybuf[slot, 0] + info[:, INFO_W + 1:INFO_W + 2] * ybuf[slot, 1])
    h = _rms(x, gain_ref[...])
    if project:
        xs_ref[...] = x
        out_ref[...] = jnp.dot(h.astype(BF16), w_ref[...], preferred_element_type=F32)
    else:
        out_ref[...] = h


def _combine_norm(x, info, pos, ys, gain, w):
    n, d = x.shape
    tm = TOKEN_TILE
    project = w is not None
    row = lambda width: pl.BlockSpec((tm, width), lambda i, pos: (i, 0))
    in_specs = [row(d), row(LANES), pl.BlockSpec((1, d), lambda i, pos: (0, 0))]
    args = [x, info, gain]
    if project:
        m = w.shape[1]
        in_specs.append(pl.BlockSpec((d, m), lambda i, pos: (0, 0)))
        args.append(w)
        out_specs = [row(d), row(m)]
        out_shape = [jax.ShapeDtypeStruct((n, d), F32), jax.ShapeDtypeStruct((n, m), F32)]
    else:
        out_specs = row(d)
        out_shape = jax.ShapeDtypeStruct((n, d), F32)
    in_specs.append(pl.BlockSpec(memory_space=pl.ANY))
    args.append(ys)
    grid_spec = pltpu.PrefetchScalarGridSpec(
        num_scalar_prefetch=1, grid=(n // tm,), in_specs=in_specs, out_specs=out_specs,
        scratch_shapes=[pltpu.VMEM((2, TOP_K, tm, d), F32), pltpu.SemaphoreType.DMA((2,))])
    return pl.pallas_call(
        functools.partial(_combine_norm_kernel, project=project),
        grid_spec=grid_spec,
        out_shape=out_shape,
        compiler_params=_params(("arbitrary",)),
        name="combine_norm",
    )(pos, *args)


def _ab_mixer_kernel(*refs, c, nseq, zero_init):
    if zero_init:
        p_ref, cw_ref, prm_ref, gkw_ref, gkb_ref, o_ref, conv_ref, sdn_ref, sgla_ref = refs
    else:
        (p_ref, cw_ref, prm_ref, gkw_ref, gkb_ref, conv_in, sdn_in, sgla_in,
         o_ref, conv_ref, sdn_ref, sgla_ref) = refs

    @pl.when(pl.program_id(1) == 0)
    def _():
        if zero_init:
            conv_ref[...] = jnp.zeros_like(conv_ref)
            sdn_ref[...] = jnp.zeros_like(sdn_ref)
            sgla_ref[...] = jnp.zeros_like(sgla_ref)
        else:
            conv_ref[...] = conv_in[...]
            sdn_ref[...] = sdn_in[...]
            sgla_ref[...] = sgla_in[...]

    m = min(16, c)
    tri = _tri_incl(c)
    cw = cw_ref[...]
    a_log, dt_bias = prm_ref[0:1], prm_ref[1:2]
    dn_gain, gla_gain = prm_ref[2:3], prm_ref[3:4]
    for b in range(nseq):
        r0 = b * c
        u = p_ref[r0:r0 + c, AB_U:AB_U + CONV_CH]
        ucat = jnp.concatenate([conv_ref[b], u], axis=0)
        acc = u * cw[CONV_W - 1:CONV_W]
        for j in range(1, CONV_W):
            acc = acc + pltpu.roll(ucat, j, 0)[SUBLANES:SUBLANES + c] * cw[CONV_W - 1 - j:CONV_W - j]
        conv_ref[b] = ucat[c:c + SUBLANES]
        qkv = acc * _sigmoid(acc)

        misc = p_ref[r0:r0 + c, AB_MISC:AB_MISC + LANES]
        g_all = -jnp.exp(a_log) * _softplus(misc + dt_bias)
        beta_all = _sigmoid(misc)
        gcum = _mm_hi(tri, g_all)
        gcum_t = jnp.concatenate([gcum, jnp.zeros((LANES - c, LANES), F32)], axis=0).T

        for h in range(DN_HEADS):
            lo = h * DN_DK
            q = qkv[:, lo:lo + DN_DK]
            k = qkv[:, DN_QK + lo:DN_QK + lo + DN_DK]
            v = qkv[:, 2 * DN_QK + h * DN_DV:2 * DN_QK + (h + 1) * DN_DV]
            q = q * lax.rsqrt(jnp.sum(q * q, axis=-1, keepdims=True) + 1e-6)
            k = k * lax.rsqrt(jnp.sum(k * k, axis=-1, keepdims=True) + 1e-6)
            o, s_new = _delta_chunk(q, k, v, gcum[:, MISC_A + h:MISC_A + h + 1], gcum_t[MISC_A + h:MISC_A + h + 1, :c],
                                    beta_all[:, MISC_B + h:MISC_B + h + 1], sdn_ref[b, h])
            sdn_ref[b, h] = s_new
            z = p_ref[r0:r0 + c, AB_Z + h * DN_DV:AB_Z + (h + 1) * DN_DV]
            o_ref[r0:r0 + c, h * DN_DV:(h + 1) * DN_DV] = _gated_rms(o, z, dn_gain).astype(o_ref.dtype)

        logits = _mm_hi(misc, gkw_ref[...]) + gkb_ref[...]
        logf = -_softplus(-logits) * (1.0 / GLA_NORMALIZER)
        gall = _mm_hi(tri, logf)
        for h in range(GLA_HEADS):
            lo = h * GLA_DK
            q = p_ref[r0:r0 + c, AB_GQ + lo:AB_GQ + lo + GLA_DK] * (GLA_DK ** -0.5)
            k = p_ref[r0:r0 + c, AB_GK + lo:AB_GK + lo + GLA_DK]
            v = p_ref[r0:r0 + c, AB_GV + h * GLA_DV:AB_GV + (h + 1) * GLA_DV]
            o, s_new = _gla_chunk(q, k, v, gall[:, lo:lo + GLA_DK], sgla_ref[b, h], m)
            sgla_ref[b, h] = s_new
            gate = p_ref[r0:r0 + c, AB_GG + h * GLA_DV:AB_GG + (h + 1) * GLA_DV]
            o_ref[r0:r0 + c, DN_V + h * GLA_DV:DN_V + (h + 1) * GLA_DV] = _gated_rms(o, gate, gla_gain).astype(o_ref.dtype)


def _ab_mixer(p, o_alias_rows, row0, nb, t, nseq, cw, prm, gkw, gkb, states):
    c = min(CHUNK, t)
    nchunk = t // c
    rows = nseq * c
    blk0 = row0 // rows
    n_groups = nb // nseq
    zero_init = states is None
    row_map = lambda g, i: (blk0 + g * nchunk + i, 0)
    st_map3 = lambda g, i: (g, 0, 0)
    st_map4 = lambda g, i: (g, 0, 0, 0)
    const = lambda shape: pl.BlockSpec(shape, lambda g, i: (0,) * len(shape))
    in_specs = [pl.BlockSpec((rows, AB_COLS), row_map), const(cw.shape), const(prm.shape), const(gkw.shape),
                const(gkb.shape)]
    args = [p, cw, prm, gkw, gkb]
    st_specs = [pl.BlockSpec((nseq, SUBLANES, CONV_CH), st_map3),
                pl.BlockSpec((nseq, DN_HEADS, DN_DK, DN_DV), st_map4),
                pl.BlockSpec((nseq, GLA_HEADS, GLA_DK, GLA_DV), st_map4)]
    if not zero_init:
        in_specs += st_specs
        args += list(states)
    out_shape = [jax.ShapeDtypeStruct((o_alias_rows, D_MODEL), BF16),
                 jax.ShapeDtypeStruct((nb, SUBLANES, CONV_CH), F32),
                 jax.ShapeDtypeStruct((nb, DN_HEADS, DN_DK, DN_DV), F32),
                 jax.ShapeDtypeStruct((nb, GLA_HEADS, GLA_DK, GLA_DV), F32)]
    out_specs = [pl.BlockSpec((rows, D_MODEL), lambda g, i: (g * nchunk + i, 0))] + st_specs
    return pl.pallas_call(
        functools.partial(_ab_mixer_kernel, c=c, nseq=nseq, zero_init=zero_init),
        grid=(n_groups, nchunk),
        in_specs=in_specs,
        out_specs=out_specs,
        out_shape=out_shape,
        compiler_params=_params(("parallel", "arbitrary")),
        name="ab_mixer",
    )(*args)


def _hgrn_mixer_kernel(*refs, c, nseq, zero_init, layer):
    if zero_init:
        p_ref, lb_ref, gain_ref, o_ref, s_ref = refs
    else:
        p_ref, lb_ref, gain_ref, s_in, o_ref, s_ref = refs

    @pl.when(pl.program_id(1) == 0)
    def _():
        if zero_init:
            s_ref[...] = jnp.zeros_like(s_ref)
        else:
            s_ref[...] = s_in[...]

    raw = lb_ref[...]
    e = jnp.exp(raw - jnp.max(raw, axis=0, keepdims=True))
    sm = e / jnp.sum(e, axis=0, keepdims=True)
    cum = sm[0:1]
    for l in range(1, layer + 1):
        cum = cum + sm[l:l + 1]
    lb = cum - sm[0:1]

    m = min(16, c)
    tri = _tri_incl(c)
    gain = gain_ref[...]
    for b in range(nseq):
        r0 = b * c
        fr = p_ref[r0:r0 + c, HG_F:2 * HG_F]
        logf = jnp.log(lb + (1.0 - lb) * _sigmoid(fr))
        kall = (1.0 - lb) * _sigmoid(-fr)
        gall = _mm_hi(tri, logf)
        for h in range(HG_HEADS):
            lo = h * HG_DK
            q = p_ref[r0:r0 + c, lo:lo + HG_DK] * (HG_DK ** -0.5)
            v = p_ref[r0:r0 + c, 2 * HG_F + h * HG_DV:2 * HG_F + (h + 1) * HG_DV]
            o, s_new = _gla_chunk(q, kall[:, lo:lo + HG_DK], v, gall[:, lo:lo + HG_DK], s_ref[b, h], m)
            s_ref[b, h] = s_new
            gate = p_ref[r0:r0 + c, 2 * HG_F + D_MODEL + h * HG_DV:2 * HG_F + D_MODEL + (h + 1) * HG_DV]
            o_ref[r0:r0 + c, h * HG_DV:(h + 1) * HG_DV] = _gated_rms(o, gate, gain).astype(o_ref.dtype)


def _hgrn_mixer(p, row0, nb, t, nseq, lb, gain, state, layer):
    c = min(CHUNK, t)
    nchunk = t // c
    rows = nseq * c
    blk0 = row0 // rows
    zero_init = state is None
    cols = p.shape[1]
    const = lambda shape: pl.BlockSpec(shape, lambda g, i: (0,) * len(shape))
    st_spec = pl.BlockSpec((nseq, HG_HEADS, HG_DK, HG_DV), lambda g, i: (g, 0, 0, 0))
    in_specs = [pl.BlockSpec((rows, cols), lambda g, i: (blk0 + g * nchunk + i, 0)), const(lb.shape), const(gain.shape)]
    args = [p, lb, gain]
    if not zero_init:
        in_specs.append(st_spec)
        args.append(state)
    return pl.pallas_call(
        functools.partial(_hgrn_mixer_kernel, c=c, nseq=nseq, zero_init=zero_init, layer=layer),
        grid=(nb // nseq, nchunk),
        in_specs=in_specs,
        out_specs=[pl.BlockSpec((rows, D_MODEL), lambda g, i: (g * nchunk + i, 0)), st_spec],
        out_shape=[jax.ShapeDtypeStruct((nb * t, D_MODEL), BF16),
                   jax.ShapeDtypeStruct((nb, HG_HEADS, HG_DK, HG_DV), F32)],
        compiler_params=_params(("parallel", "arbitrary")),
        name="hgrn_mixer",
    )(*args)


def _outproj_router_kernel(o_ref, x_ref, wout_ref, gain_ref, wr_ref, br_ref, x1_ref, h_ref, info_ref, cnt_ref, run_ref):
    @pl.when(pl.program_id(0) == 0)
    def _():
        run_ref[...] = jnp.zeros_like(run_ref)

    x1 = x_ref[...] + jnp.dot(o_ref[...], wout_ref[...], preferred_element_type=F32)
    x1_ref[...] = x1
    h = _rms(x1, gain_ref[...])
    h_ref[...] = h
    logit = _mm_hi(h, wr_ref[...]) + br_ref[...]
    lane = lax.broadcasted_iota(I32, logit.shape, 1)
    far = jnp.int32(LANES)
    lg = jnp.where(lane < N_GROUPS, logit, NEG)
    mg = jnp.max(lg, axis=-1, keepdims=True)
    p_g = 1.0 / jnp.sum(jnp.exp(lg - mg), axis=-1, keepdims=True)
    g_top = jnp.min(jnp.where(lg == mg, lane, far), axis=-1, keepdims=True)
    lo = N_GROUPS + g_top * EXPERTS_PER_GROUP
    sel = jnp.logical_and(lane >= lo, lane < lo + EXPERTS_PER_GROUP)
    le = jnp.where(sel, logit, NEG)
    ee = jnp.exp(le - jnp.max(le, axis=-1, keepdims=True))
    pe = jnp.where(sel, ee / jnp.sum(ee, axis=-1, keepdims=True), -1.0)
    p1 = jnp.max(pe, axis=-1, keepdims=True)
    i1 = jnp.min(jnp.where(pe == p1, lane, far), axis=-1, keepdims=True)
    pe2 = jnp.where(lane == i1, -1.0, pe)
    p2 = jnp.max(pe2, axis=-1, keepdims=True)
    i2 = jnp.min(jnp.where(pe2 == p2, lane, far), axis=-1, keepdims=True)
    denom = p1 + p2
    w1 = p_g * (p1 / denom)
    w2 = p_g * (p2 / denom)
    tm = logit.shape[0]
    oh1 = jnp.where(lane == i1, 1.0, 0.0)
    oh2 = jnp.where(lane == i2, 1.0, 0.0)
    both = oh1 + oh2
    tr = lax.broadcasted_iota(I32, (tm, tm), 0)
    tc = lax.broadcasted_iota(I32, (tm, tm), 1)
    before = run_ref[...] + _mm(jnp.where(tr > tc, 1.0, 0.0), both)
    r1 = jnp.sum(oh1 * before, axis=-1, keepdims=True)
    r2 = jnp.sum(oh2 * before, axis=-1, keepdims=True)
    run = run_ref[...] + jnp.sum(both, axis=0, keepdims=True)
    run_ref[...] = run
    cnt_ref[...] = jnp.broadcast_to(run, cnt_ref.shape)
    vals = ((i1 - N_GROUPS).astype(F32), (i2 - N_GROUPS).astype(F32), w1, w2, r1, r2)
    info = jnp.zeros_like(logit)
    for idx, val in enumerate(vals):
        info = jnp.where(lane == idx, val, info)
    info_ref[...] = info


def _outproj_router(o, x, wout, gain, wr, br):
    n, d = x.shape
    tm = TOKEN_TILE
    row = lambda w: pl.BlockSpec((tm, w), lambda i: (i, 0))
    const = lambda shape: pl.BlockSpec(shape, lambda i: (0,) * len(shape))
    return pl.pallas_call(
        _outproj_router_kernel,
        grid=(n // tm,),
        in_specs=[row(d), row(d), const((d, d)), const((1, d)), const((d, LANES)), const((1, LANES))],
        out_specs=[row(d), row(d), row(LANES), const((SUBLANES, LANES))],
        out_shape=[jax.ShapeDtypeStruct((n, d), F32), jax.ShapeDtypeStruct((n, d), F32),
                   jax.ShapeDtypeStruct((n, LANES), F32), jax.ShapeDtypeStruct((SUBLANES, LANES), F32)],
        scratch_shapes=[pltpu.VMEM((1, LANES), F32)],
        compiler_params=_params(("arbitrary",)),
        name="outproj_router",
    )(o, x, wout, gain, wr, br)


def _dispatch_kernel(pos_ref, lo_ref, hi_ref, h_ref, xs_hbm, buf, zrow, sem, zsem):
    i = pl.program_id(0)
    n = pl.num_programs(0)
    tm = h_ref.shape[0]
    slot = i % 2

    def tile_wait(s):
        for _ in range(TOP_K):
            pltpu.make_async_copy(buf.at[s], xs_hbm.at[pl.ds(0, tm)], sem.at[s]).wait()

    def pad_rows(fn, tile_fn):
        for e in range(N_EXPERTS):
            lax.fori_loop(lo_ref[e], hi_ref[e], fn, 0)
        lax.fori_loop(hi_ref[N_EXPERTS - 1] // tm, xs_hbm.shape[0] // tm, tile_fn, 0)

    def pad_start(r, carry):
        pltpu.make_async_copy(zrow.at[pl.ds(0, 1)], xs_hbm.at[pl.ds(r, 1)], zsem.at[0]).start()
        return carry

    def pad_wait(r, carry):
        pltpu.make_async_copy(zrow.at[pl.ds(0, 1)], xs_hbm.at[pl.ds(0, 1)], zsem.at[0]).wait()
        return carry

    def pad_tile_start(t, carry):
        pltpu.make_async_copy(zrow, xs_hbm.at[pl.ds(pl.multiple_of(t * tm, tm), tm)], zsem.at[0]).start()
        return carry

    def pad_tile_wait(t, carry):
        pltpu.make_async_copy(zrow, xs_hbm.at[pl.ds(0, tm)], zsem.at[0]).wait()
        return carry

    @pl.when(i == 0)
    def _():
        zrow[...] = jnp.zeros_like(zrow)
        pad_rows(pad_start, pad_tile_start)

    @pl.when(i >= 2)
    def _():
        tile_wait(slot)
    buf[slot] = h_ref[...]

    def body(r, carry):
        for k in range(TOP_K):
            row = pos_ref[TOP_K * (i * tm + r) + k]
            pltpu.make_async_copy(buf.at[slot, pl.ds(r, 1)], xs_hbm.at[pl.ds(row, 1)], sem.at[slot]).start()
        return carry
    lax.fori_loop(0, tm, body, 0, unroll=8)

    @pl.when(i == n - 1)
    def _():
        tile_wait(slot)

        @pl.when(n > 1)
        def _():
            tile_wait(1 - slot)
        pad_rows(pad_wait, pad_tile_wait)


def _dispatch(h, pos, pad_lo, pad_hi, n_rows):
    n, d = h.shape
    tm = TOKEN_TILE
    grid_spec = pltpu.PrefetchScalarGridSpec(
        num_scalar_prefetch=3,
        grid=(n // tm,),
        in_specs=[pl.BlockSpec((tm, d), lambda i, pos, lo, hi: (i, 0))],
        out_specs=pl.BlockSpec(memory_space=pl.ANY),
        scratch_shapes=[pltpu.VMEM((2, tm, d), F32), pltpu.VMEM((tm, d), F32),
                        pltpu.SemaphoreType.DMA((2,)), pltpu.SemaphoreType.DMA((1,))],
    )
    return pl.pallas_call(
        _dispatch_kernel,
        grid_spec=grid_spec,
        out_shape=jax.ShapeDtypeStruct((n_rows, d), F32),
        compiler_params=_params(("arbitrary",)),
        name="moe_dispatch",
    )(pos, pad_lo, pad_hi, h)


def _moe_kernel(te_ref, nact_ref, x_ref, w1_ref, w3_ref, w2_ref, y_ref, w1b, w3b, w2b):
    i = pl.program_id(0)

    @pl.when(i < nact_ref[0])
    def _():
        @pl.when(jnp.logical_or(i == 0, te_ref[i] != te_ref[jnp.maximum(i - 1, 0)]))
        def _():
            w1b[...] = w1_ref[0].astype(BF16)
            w3b[...] = w3_ref[0].astype(BF16)
            w2b[...] = w2_ref[0].astype(BF16)

        xb = x_ref[...].astype(BF16)
        h1 = jnp.dot(xb, w1b[...], preferred_element_type=F32)
        h3 = jnp.dot(xb, w3b[...], preferred_element_type=F32)
        act = (h1 * _sigmoid(h1)) * h3
        y_ref[...] = jnp.dot(act.astype(BF16), w2b[...], preferred_element_type=F32)

    @pl.when(i >= nact_ref[0])
    def _():
        y_ref[...] = jnp.zeros_like(y_ref)


def _moe(xs, te, nact, w1, w3, w2):
    p_rows, d = xs.shape
    f = w1.shape[2]
    tm = MOE_TILE
    last = lambda i, nact: jnp.minimum(i, nact[0] - 1)
    grid_spec = pltpu.PrefetchScalarGridSpec(
        num_scalar_prefetch=2,
        grid=(p_rows // tm,),
        in_specs=[
            pl.BlockSpec((tm, d), lambda i, te, nact: (last(i, nact), 0)),
            pl.BlockSpec((1, d, f), lambda i, te, nact: (te[i], 0, 0)),
            pl.BlockSpec((1, d, f), lambda i, te, nact: (te[i], 0, 0)),
            pl.BlockSpec((1, f, d), lambda i, te, nact: (te[i], 0, 0)),
        ],
        out_specs=pl.BlockSpec((tm, d), lambda i, te, nact: (i, 0)),
        scratch_shapes=[pltpu.VMEM((d, f), BF16), pltpu.VMEM((d, f), BF16), pltpu.VMEM((f, d), BF16)],
    )
    return pl.pallas_call(
        _moe_kernel,
        grid_spec=grid_spec,
        out_shape=jax.ShapeDtypeStruct((p_rows, d), F32),
        compiler_params=_params(("arbitrary",)),
        name="moe_experts",
    )(te, nact, xs, w1, w3, w2)


def _moe_rows(n):
    return ((TOP_K * n + N_EXPERTS * (MOE_TILE - 1)) // MOE_TILE) * MOE_TILE


def _route_tables(info, counts_row, n):
    tm = MOE_TILE
    n_tiles = _moe_rows(n) // tm
    ids = jnp.arange(N_EXPERTS, dtype=I32)
    counts = counts_row[0, N_GROUPS:N_GROUPS + N_EXPERTS].astype(I32)
    padded = ((counts + tm - 1) // tm) * tm
    cum = jnp.cumsum(padded)
    off = cum - padded
    e = info[:, INFO_E:INFO_E + TOP_K].astype(I32)
    rank = info[:, INFO_R:INFO_R + TOP_K].astype(I32)
    pos = (jnp.sum(jnp.where(e[:, :, None] == ids, off, 0), axis=-1) + rank).reshape(TOP_K * n)
    tile_start = jnp.arange(n_tiles, dtype=I32) * tm
    nact = cum[-1] // tm
    te_raw = jnp.sum((cum[None, :] <= tile_start[:, None]).astype(I32), axis=1)
    te_last = jnp.sum((cum <= (nact - 1) * tm).astype(I32))
    te = jnp.where(tile_start < cum[-1], jnp.minimum(te_raw, N_EXPERTS - 1), te_last)
    return pos, off + counts, cum, te, nact.reshape(1)


def _pad_lanes(a, width=LANES):
    return jnp.pad(a, [(0, 0)] * (a.ndim - 1) + [(0, width - a.shape[-1])])


def _sample_group(nb):
    g = 8
    while nb % g:
        g //= 2
    return g


def kernel(x_prompt, x_sample, state_conv_ab, state_delta_ab, state_gla_ab, state_hgrn_c, mix_norm, ab_w_in, ab_conv_w, ab_a_log, ab_dt_bias, ab_dn_norm, ab_gk_w2, ab_gk_b, ab_gla_norm, ab_w_out, c_w_in, c_lower_bounds, c_norm, c_w_out, ffn_norm, moe_w_group, moe_b_group, moe_w_expert, moe_b_expert, moe_w1, moe_w3, moe_w2, final_norm):
    bp, tp, d = x_prompt.shape
    bs, ts, _ = x_sample.shape
    n_p, n_s = bp * tp, bs * ts
    n = n_p + n_s
    depth = mix_norm.shape[0]
    x = jnp.concatenate([x_prompt.reshape(n_p, d), x_sample.reshape(n_s, d)], axis=0)
    sg = _sample_group(bs)

    info = pos = ys = None
    conv_p, delta_p, gla_p, hgrn_p = [], [], [], []
    conv_s, delta_s, gla_s, hgrn_s = [], [], [], []
    for layer in range(depth):
        j = layer // 2
        gain = mix_norm[layer].reshape(1, d)
        if layer % 2 == 0:
            u_w, a_w, b_w, z_w, gq_w, gk_w, gv_w, lr_w, gg_w = jnp.split(ab_w_in[j], _split_points(AB_SPLITS), axis=1)
            w_in = jnp.concatenate([u_w, z_w, gq_w, gk_w, gv_w, gg_w, _pad_lanes(jnp.concatenate([a_w, b_w, lr_w], axis=1))],
                                   axis=1).astype(BF16)
        else:
            w_in = c_w_in[j].astype(BF16)
        if ys is None:
            p = _norm_proj(x, gain, w_in)
        else:
            x, p = _combine_norm(x, info, pos, ys, gain, w_in)

        if layer % 2 == 0:
            prm = jnp.concatenate([_pad_lanes(ab_a_log[j].reshape(1, -1)), _pad_lanes(ab_dt_bias[j].reshape(1, -1)),
                                   ab_dn_norm[j].reshape(1, -1), ab_gla_norm[j].reshape(1, -1),
                                   jnp.zeros((4, LANES), F32)], axis=0)
            gkw = jnp.zeros((LANES, GLA_K), F32).at[MISC_LR:MISC_LR + GLA_RANK].set(ab_gk_w2[j])
            gkb = ab_gk_b[j].reshape(1, GLA_K)
            cw = ab_conv_w[j]
            o_p, c8, sd, sgl = _ab_mixer(p, n_p, 0, bp, tp, 1, cw, prm, gkw, gkb, None)
            conv_p.append(c8[:, SUBLANES - (CONV_W - 1):])
            delta_p.append(sd)
            gla_p.append(sgl)
            conv8 = jnp.pad(state_conv_ab[j], ((0, 0), (SUBLANES - (CONV_W - 1), 0), (0, 0)))
            o_s, c8, sd, sgl = _ab_mixer(p, n_s, n_p, bs, ts, sg, cw, prm, gkw, gkb,
                                         (conv8, state_delta_ab[j], state_gla_ab[j]))
            conv_s.append(c8[:, SUBLANES - (CONV_W - 1):])
            delta_s.append(sd)
            gla_s.append(sgl)
            w_out = ab_w_out[j]
        else:
            hg_gain = c_norm[j].reshape(1, -1)
            o_p, sh = _hgrn_mixer(p, 0, bp, tp, 1, c_lower_bounds, hg_gain, None, layer)
            hgrn_p.append(sh)
            o_s, sh = _hgrn_mixer(p, n_p, bs, ts, sg, c_lower_bounds, hg_gain, state_hgrn_c[j], layer)
            hgrn_s.append(sh)
            w_out = c_w_out[j]
        o = jnp.concatenate([o_p, o_s], axis=0)

        wr = _pad_lanes(jnp.concatenate([moe_w_group[layer], moe_w_expert[layer]], axis=1))
        br = _pad_lanes(jnp.concatenate([moe_b_group[layer], moe_b_expert[layer]]).reshape(1, -1))
        x, hn, info, counts = _outproj_router(o, x, w_out.astype(BF16), ffn_norm[layer].reshape(1, d), wr, br)
        pos, pad_lo, pad_hi, te, nact = _route_tables(info, counts, n)
        xs = _dispatch(hn, pos, pad_lo, pad_hi, _moe_rows(n))
        ys = _moe(xs, te, nact, moe_w1[layer], moe_w3[layer], moe_w2[layer])

    y = _combine_norm(x, info, pos, ys, final_norm.reshape(1, d), None)
    y_prompt = y[:n_p].reshape(bp, tp, d)
    y_sample = y[n_p:].reshape(bs, ts, d)
    return (y_prompt, y_sample, jnp.stack(conv_p), jnp.stack(delta_p), jnp.stack(gla_p), jnp.stack(hgrn_p),
            jnp.stack(conv_s), jnp.stack(delta_s), jnp.stack(gla_s), jnp.stack(hgrn_s))


def _split_points(sizes):
    pts, acc = [], 0
    for s in sizes[:-1]:
        acc += s
        pts.append(acc)
    return pts
```

```python
import functools

import jax
import jax.numpy as jnp
from jax import lax
from jax.experimental import pallas as pl
from jax.experimental.pallas import tpu as pltpu

F32 = jnp.float32
BF16 = jnp.bfloat16
I32 = jnp.int32

D_MODEL = 1024
EPS = 1e-6
CHUNK = 64
CONV_W = 4
DN_HEADS, DN_DK, DN_DV = 4, 128, 128
DN_QK = DN_HEADS * DN_DK
DN_V = DN_HEADS * DN_DV
CONV_CH = 2 * DN_QK + DN_V
GLA_HEADS, GLA_DK, GLA_DV = 4, 64, 128
GLA_K = GLA_HEADS * GLA_DK
GLA_V = GLA_HEADS * GLA_DV
GLA_RANK = 16
GLA_NORMALIZER = 16.0
GLA_PACK = 128 // GLA_DK
AB_SPLITS = (CONV_CH, DN_HEADS, DN_HEADS, DN_V, GLA_K, GLA_K, GLA_V, GLA_RANK, GLA_V)
HG_HEADS, HG_DK, HG_DV = 8, 128, 128
HG_F = HG_HEADS * HG_DK
N_GROUPS, EXPERTS_PER_GROUP = 4, 8
N_EXPERTS = N_GROUPS * EXPERTS_PER_GROUP
TOP_K = 2
D_EXPERT = 512

LANES = 128
SUBLANES = 8
VMEM_LIMIT_BYTES = 56 * 1024 * 1024

AB_U, AB_Z, AB_GQ, AB_GK, AB_GV, AB_GG, AB_MISC = 0, 1536, 2048, 2304, 2560, 3072, 3584
AB_COLS = AB_MISC + LANES
MISC_A, MISC_B, MISC_LR = 0, DN_HEADS, 2 * DN_HEADS

INFO_E, INFO_W, INFO_R = 0, TOP_K, 2 * TOP_K

TOKEN_TILE = 256
MOE_TILE = 256
NEG = -1e30


def _params(sem):
    return pltpu.CompilerParams(dimension_semantics=sem, vmem_limit_bytes=VMEM_LIMIT_BYTES)


def _mm(a, b):
    return jnp.dot(a.astype(BF16), b.astype(BF16), preferred_element_type=F32)


def _mm_nt(a, b):
    return lax.dot_general(a.astype(BF16), b.astype(BF16), (((1,), (1,)), ((), ())), preferred_element_type=F32)


def _mm_tn(a, b):
    return lax.dot_general(a.astype(BF16), b.astype(BF16), (((0,), (0,)), ((), ())), preferred_element_type=F32)


def _mm_hi(a, b):
    return jnp.dot(a, b, preferred_element_type=F32, precision=lax.Precision.HIGHEST)


def _sigmoid(x):
    return 1.0 / (1.0 + jnp.exp(-x))


def _softplus(x):
    return jnp.maximum(x, 0.0) + jnp.log(1.0 + jnp.exp(-jnp.abs(x)))


def _rms(x, gain):
    return x * lax.rsqrt(jnp.mean(x * x, axis=-1, keepdims=True) + EPS) * gain


def _gated_rms(o, gate, gain):
    return _rms(o, gain) * (gate * _sigmoid(gate))


def _row_to_col(row):
    n = row.shape[1]
    r = lax.broadcasted_iota(I32, (n, n), 0)
    c = lax.broadcasted_iota(I32, (n, n), 1)
    return jnp.sum(jnp.where(r == c, jnp.broadcast_to(row, (n, n)), 0.0), axis=1, keepdims=True)


def _split2(x):
    hi = x.astype(BF16)
    return hi, (x - hi.astype(F32)).astype(BF16)


def _split3(x):
    hi = x.astype(BF16)
    r = x - hi.astype(F32)
    mid = r.astype(BF16)
    return hi, mid, (r - mid.astype(F32)).astype(BF16)


def _mm_cum(tri, x):
    hi, mid, lo = _split3(x)
    dot = lambda p: jnp.dot(tri, p, preferred_element_type=F32)
    return dot(hi) + (dot(mid) + dot(lo))


def _mm_split(a, b):
    ah, al = _split2(a)
    bh, bl = _split2(b)
    dot = lambda x, y: jnp.dot(x, y, preferred_element_type=F32)
    return dot(ah, bh) + (dot(ah, bl) + dot(al, bh))


def _delta_chunks(qs_, ks_, vs_, gcs_, grs_, betas_, ss_):
    heads = range(len(qs_))
    c = qs_[0].shape[0]
    ti = lax.broadcasted_iota(I32, (c, c), 0)
    tj = lax.broadcasted_iota(I32, (c, c), 1)
    eye = jnp.where(ti == tj, 1.0, 0.0)
    decay = [jnp.exp(jnp.where(ti >= tj, gcs_[h] - grs_[h], NEG)) for h in heads]
    qs = [qs_[h] * (DN_DK ** -0.5) for h in heads]
    kk = [_mm_nt(ks_[h], ks_[h]) for h in heads]
    nm = [jnp.where(ti > tj, betas_[h] * kk[h] * decay[h], 0.0) for h in heads]
    t = [eye - nm[h] for h in heads]
    p = nm
    step = 2
    while step < c:
        p = [_mm(p[h], p[h]) for h in heads]
        t = [t[h] + _mm(t[h], p[h]) for h in heads]
        step *= 2
    eg = [jnp.exp(gcs_[h]) for h in heads]
    rhs = [jnp.concatenate([betas_[h] * vs_[h], (betas_[h] * eg[h]) * ks_[h]], axis=1) for h in heads]
    sol = [_mm(t[h], rhs[h]) for h in heads]
    resid = [rhs[h] - (sol[h] + _mm_split(nm[h], sol[h])) for h in heads]
    sol = [sol[h] + _mm(t[h], resid[h]) for h in heads]
    qk = [_mm_nt(qs[h], ks_[h]) * decay[h] for h in heads]
    u = [sol[h][:, :DN_DV] - _mm(sol[h][:, DN_DV:], ss_[h]) for h in heads]
    o = [_mm(qs[h] * eg[h], ss_[h]) + _mm(qk[h], u[h]) for h in heads]
    gl = [gcs_[h][c - 1:c] for h in heads]
    s_new = [ss_[h] * jnp.exp(gl[h]) + _mm_tn(ks_[h] * jnp.exp(gl[h] - gcs_[h]), u[h]) for h in heads]
    return o, s_new


def _gla_chunk(q, k, g, vs, s, m):
    c = q.shape[0]
    nh = len(vs)
    dk = LANES // nh
    dlane = lax.broadcasted_iota(I32, (1, LANES), 1)
    if nh == 1:
        mask = lambda x, h: x
    else:
        hm = [jnp.where(jnp.logical_and(dlane >= h * dk, dlane < (h + 1) * dk), 1.0, 0.0) for h in range(nh)]
        mask = lambda x, h: x * hm[h]
    qeg = q * jnp.exp(g)
    o = [_mm(mask(qeg, h), s) for h in range(nh)]
    jrow = lax.broadcasted_iota(I32, (c, 1), 0)
    lane = lax.broadcasted_iota(I32, (m, c), 1)
    sub = lax.broadcasted_iota(I32, (m, 1), 0)
    blocks = [[] for _ in range(nh)]
    for blk in range(c // m):
        r0 = blk * m
        qb, gb, kb = q[r0:r0 + m], g[r0:r0 + m], k[r0:r0 + m]
        if blk > 0:
            base = g[r0 - 1:r0]
            kt = k * jnp.exp(jnp.where(jrow < r0, base - g, NEG))
            qt = qb * jnp.exp(gb - base)
            att = [_mm_nt(mask(qt, h), kt) for h in range(nh)]
        else:
            att = [jnp.zeros((m, c), F32) for _ in range(nh)]
        for j in range(m):
            e = jnp.exp(jnp.where(sub >= j, gb - gb[j:j + 1], NEG))
            prod = qb * (kb[j:j + 1] * e)
            for h in range(nh):
                col = jnp.sum(mask(prod, h), axis=1, keepdims=True)
                att[h] = jnp.where(lane == r0 + j, col, att[h])
        for h in range(nh):
            blocks[h].append(att[h])
    gl = g[c - 1:c]
    kg = k * jnp.exp(gl - g)
    s_new = s * _row_to_col(jnp.exp(gl))
    for h in range(nh):
        att = blocks[h][0] if len(blocks[h]) == 1 else jnp.concatenate(blocks[h], axis=0)
        o[h] = o[h] + _mm(att, vs[h])
        s_new = s_new + _mm_tn(mask(kg, h), vs[h])
    return o, s_new


def _tri_incl(c):
    r = lax.broadcasted_iota(I32, (c, c), 0)
    cc = lax.broadcasted_iota(I32, (c, c), 1)
    return jnp.where(r >= cc, 1.0, 0.0).astype(BF16)


def _norm_proj_kernel(x_ref, gain_ref, w_ref, p_ref):
    h = _rms(x_ref[...], gain_ref[...])
    p_ref[...] = jnp.dot(h.astype(BF16), w_ref[...], preferred_element_type=F32)


def _norm_proj(x, gain, w):
    n, d = x.shape
    m = w.shape[1]
    tm = TOKEN_TILE
    return pl.pallas_call(
        _norm_proj_kernel,
        grid=(n // tm,),
        in_specs=[pl.BlockSpec((tm, d), lambda i: (i, 0)), pl.BlockSpec((1, d), lambda i: (0, 0)),
                  pl.BlockSpec((d, m), lambda i: (0, 0))],
        out_specs=pl.BlockSpec((tm, m), lambda i: (i, 0)),
        out_shape=jax.ShapeDtypeStruct((n, m), F32),
        compiler_params=_params(("parallel",)),
        name="norm_proj",
    )(x, gain, w)


def _combine_norm_kernel(pos_ref, *refs, project):
    if project:
        x_ref, info_ref, gain_ref, w_ref, ys_hbm, xs_ref, out_ref, ybuf, sem = refs
    else:
        x_ref, info_ref, gain_ref, ys_hbm, out_ref, ybuf, sem = refs
    i = pl.program_id(0)
    n = pl.num_programs(0)
    tm = x_ref.shape[0]
    slot = i % 2

    def gather_start(tile, s):
        def body(r, carry):
            for k in range(TOP_K):
                row = pos_ref[TOP_K * (tile * tm + r) + k]
                pltpu.make_async_copy(ys_hbm.at[pl.ds(row, 1)], ybuf.at[s, k, pl.ds(r, 1)], sem.at[s]).start()
            return carry
        lax.fori_loop(0, tm, body, 0, unroll=8)

    @pl.when(i == 0)
    def _():
        gather_start(0, 0)

    @pl.when(i + 1 < n)
    def _():
        gather_start(i + 1, 1 - slot)

    for k in range(TOP_K):
        pltpu.make_async_copy(ys_hbm.at[pl.ds(0, tm)], ybuf.at[slot, k], sem.at[slot]).wait()
    info = info_ref[...]
    x = x_ref[...] + (info[:, INFO_W:INFO_W + 1] * ybuf[slot, 0] + info[:, INFO_W + 1:INFO_W + 2] * ybuf[slot, 1])
    h = _rms(x, gain_ref[...])
    if project:
        xs_ref[...] = x
        out_ref[...] = jnp.dot(h.astype(BF16), w_ref[...], preferred_element_type=F32)
    else:
        out_ref[...] = h


def _combine_norm(x, info, pos, ys, gain, w):
    n, d = x.shape
    tm = TOKEN_TILE
    project = w is not None
    row = lambda width: pl.BlockSpec((tm, width), lambda i, pos: (i, 0))
    in_specs = [row(d), row(LANES), pl.BlockSpec((1, d), lambda i, pos: (0, 0))]
    args = [x, info, gain]
    if project:
        m = w.shape[1]
        in_specs.append(pl.BlockSpec((d, m), lambda i, pos: (0, 0)))
        args.append(w)
        out_specs = [row(d), row(m)]
        out_shape = [jax.ShapeDtypeStruct((n, d), F32), jax.ShapeDtypeStruct((n, m), F32)]
    else:
        out_specs = row(d)
        out_shape = jax.ShapeDtypeStruct((n, d), F32)
    in_specs.append(pl.BlockSpec(memory_space=pl.ANY))
    args.append(ys)
    grid_spec = pltpu.PrefetchScalarGridSpec(
        num_scalar_prefetch=1, grid=(n // tm,), in_specs=in_specs, out_specs=out_specs,
        scratch_shapes=[pltpu.VMEM((2, TOP_K, tm, d), F32), pltpu.SemaphoreType.DMA((2,))])
    return pl.pallas_call(
        functools.partial(_combine_norm_kernel, project=project),
        grid_spec=grid_spec,
        out_shape=out_shape,
        compiler_params=_params(("arbitrary",)),
        name="combine_norm",
    )(pos, *args)


def _ab_mixer_kernel(*refs, c, nseq, zero_init):
    if zero_init:
        p_ref, cw_ref, prm_ref, gkw_ref, gkb_ref, o_ref, conv_ref, sdn_ref, sgla_ref = refs
    else:
        (p_ref, cw_ref, prm_ref, gkw_ref, gkb_ref, conv_in, sdn_in, sgla_in,
         o_ref, conv_ref, sdn_ref, sgla_ref) = refs

    @pl.when(pl.program_id(1) == 0)
    def _():
        if zero_init:
            conv_ref[...] = jnp.zeros_like(conv_ref)
            sdn_ref[...] = jnp.zeros_like(sdn_ref)
            sgla_ref[...] = jnp.zeros_like(sgla_ref)
        else:
            conv_ref[...] = conv_in[...]
            sdn_ref[...] = sdn_in[...]
            sgla_ref[...] = sgla_in[...]

    m = min(16, c)
    tri = _tri_incl(c)
    cw = cw_ref[...]
    a_log, dt_bias = prm_ref[0:1], prm_ref[1:2]
    dn_gain, gla_gain = prm_ref[2:3], prm_ref[3:4]
    for b in range(nseq):
        r0 = b * c
        u = p_ref[r0:r0 + c, AB_U:AB_U + CONV_CH]
        ucat = jnp.concatenate([conv_ref[b], u], axis=0)
        acc = u * cw[CONV_W - 1:CONV_W]
        for j in range(1, CONV_W):
            acc = acc + pltpu.roll(ucat, j, 0)[SUBLANES:SUBLANES + c] * cw[CONV_W - 1 - j:CONV_W - j]
        conv_ref[b] = ucat[c:c + SUBLANES]
        qkv = acc * _sigmoid(acc)

        misc = p_ref[r0:r0 + c, AB_MISC:AB_MISC + LANES]
        g_all = -jnp.exp(a_log) * _softplus(misc + dt_bias)
        beta_all = _sigmoid(misc)
        gcum = _mm_cum(tri, g_all)
        gcum_t = jnp.concatenate([gcum, jnp.zeros((LANES - c, LANES), F32)], axis=0).T

        qs, ks, vs = [], [], []
        for h in range(DN_HEADS):
            lo = h * DN_DK
            q = qkv[:, lo:lo + DN_DK]
            k = qkv[:, DN_QK + lo:DN_QK + lo + DN_DK]
            qs.append(q * lax.rsqrt(jnp.sum(q * q, axis=-1, keepdims=True) + 1e-6))
            ks.append(k * lax.rsqrt(jnp.sum(k * k, axis=-1, keepdims=True) + 1e-6))
            vs.append(qkv[:, 2 * DN_QK + h * DN_DV:2 * DN_QK + (h + 1) * DN_DV])
        os_, ss_ = _delta_chunks(
            qs, ks, vs,
            [gcum[:, MISC_A + h:MISC_A + h + 1] for h in range(DN_HEADS)],
            [gcum_t[MISC_A + h:MISC_A + h + 1, :c] for h in range(DN_HEADS)],
            [beta_all[:, MISC_B + h:MISC_B + h + 1] for h in range(DN_HEADS)],
            [sdn_ref[b, h] for h in range(DN_HEADS)])
        for h in range(DN_HEADS):
            sdn_ref[b, h] = ss_[h]
            z = p_ref[r0:r0 + c, AB_Z + h * DN_DV:AB_Z + (h + 1) * DN_DV]
            o_ref[r0:r0 + c, h * DN_DV:(h + 1) * DN_DV] = _gated_rms(os_[h], z, dn_gain).astype(o_ref.dtype)

        logits = _mm_split(misc, gkw_ref[...]) + gkb_ref[...]
        logf = -_softplus(-logits) * (1.0 / GLA_NORMALIZER)
        gall = _mm_cum(tri, logf)
        for pair in range(GLA_HEADS // GLA_PACK):
            lo = pair * LANES
            heads = [pair * GLA_PACK + i for i in range(GLA_PACK)]
            q = p_ref[r0:r0 + c, AB_GQ + lo:AB_GQ + lo + LANES] * (GLA_DK ** -0.5)
            k = p_ref[r0:r0 + c, AB_GK + lo:AB_GK + lo + LANES]
            vs = [p_ref[r0:r0 + c, AB_GV + h * GLA_DV:AB_GV + (h + 1) * GLA_DV] for h in heads]
            os_, s_new = _gla_chunk(q, k, gall[:, lo:lo + LANES], vs, sgla_ref[b, pair], m)
            sgla_ref[b, pair] = s_new
            for o, h in zip(os_, heads):
                gate = p_ref[r0:r0 + c, AB_GG + h * GLA_DV:AB_GG + (h + 1) * GLA_DV]
                o_ref[r0:r0 + c, DN_V + h * GLA_DV:DN_V + (h + 1) * GLA_DV] = _gated_rms(o, gate, gla_gain).astype(o_ref.dtype)


def _ab_mixer(p, o_alias_rows, row0, nb, t, nseq, cw, prm, gkw, gkb, states):
    c = min(CHUNK, t)
    nchunk = t // c
    rows = nseq * c
    blk0 = row0 // rows
    n_groups = nb // nseq
    zero_init = states is None
    row_map = lambda g, i: (blk0 + g * nchunk + i, 0)
    st_map3 = lambda g, i: (g, 0, 0)
    st_map4 = lambda g, i: (g, 0, 0, 0)
    const = lambda shape: pl.BlockSpec(shape, lambda g, i: (0,) * len(shape))
    in_specs = [pl.BlockSpec((rows, AB_COLS), row_map), const(cw.shape), const(prm.shape), const(gkw.shape),
                const(gkb.shape)]
    args = [p, cw, prm, gkw, gkb]
    gla_packed = (GLA_HEADS // GLA_PACK, GLA_PACK * GLA_DK, GLA_DV)
    st_specs = [pl.BlockSpec((nseq, SUBLANES, CONV_CH), st_map3),
                pl.BlockSpec((nseq, DN_HEADS, DN_DK, DN_DV), st_map4),
                pl.BlockSpec((nseq,) + gla_packed, st_map4)]
    if not zero_init:
        conv8, sdn, sgla = states
        in_specs += st_specs
        args += [conv8, sdn, sgla.reshape((nb,) + gla_packed)]
    out_shape = [jax.ShapeDtypeStruct((o_alias_rows, D_MODEL), BF16),
                 jax.ShapeDtypeStruct((nb, SUBLANES, CONV_CH), F32),
                 jax.ShapeDtypeStruct((nb, DN_HEADS, DN_DK, DN_DV), F32),
                 jax.ShapeDtypeStruct((nb,) + gla_packed, F32)]
    out_specs = [pl.BlockSpec((rows, D_MODEL), lambda g, i: (g * nchunk + i, 0))] + st_specs
    o, conv_out, sdn_out, sgla_out = pl.pallas_call(
        functools.partial(_ab_mixer_kernel, c=c, nseq=nseq, zero_init=zero_init),
        grid=(n_groups, nchunk),
        in_specs=in_specs,
        out_specs=out_specs,
        out_shape=out_shape,
        compiler_params=_params(("parallel", "arbitrary")),
        name="ab_mixer",
    )(*args)
    return o, conv_out, sdn_out, sgla_out.reshape(nb, GLA_HEADS, GLA_DK, GLA_DV)


def _hgrn_mixer_kernel(*refs, c, nseq, zero_init, layer):
    if zero_init:
        p_ref, lb_ref, gain_ref, o_ref, s_ref = refs
    else:
        p_ref, lb_ref, gain_ref, s_in, o_ref, s_ref = refs

    @pl.when(pl.program_id(1) == 0)
    def _():
        if zero_init:
            s_ref[...] = jnp.zeros_like(s_ref)
        else:
            s_ref[...] = s_in[...]

    raw = lb_ref[...]
    e = jnp.exp(raw - jnp.max(raw, axis=0, keepdims=True))
    sm = e / jnp.sum(e, axis=0, keepdims=True)
    cum = sm[0:1]
    for l in range(1, layer + 1):
        cum = cum + sm[l:l + 1]
    lb = cum - sm[0:1]

    m = min(16, c)
    tri = _tri_incl(c)
    gain = gain_ref[...]
    for b in range(nseq):
        r0 = b * c
        fr = p_ref[r0:r0 + c, HG_F:2 * HG_F]
        logf = jnp.log(lb + (1.0 - lb) * _sigmoid(fr))
        kall = (1.0 - lb) * _sigmoid(-fr)
        gall = _mm_cum(tri, logf)
        for h in range(HG_HEADS):
            lo = h * HG_DK
            q = p_ref[r0:r0 + c, lo:lo + HG_DK] * (HG_DK ** -0.5)
            v = p_ref[r0:r0 + c, 2 * HG_F + h * HG_DV:2 * HG_F + (h + 1) * HG_DV]
            (o,), s_new = _gla_chunk(q, kall[:, lo:lo + HG_DK], gall[:, lo:lo + HG_DK], [v], s_ref[b, h], m)
            s_ref[b, h] = s_new
            gate = p_ref[r0:r0 + c, 2 * HG_F + D_MODEL + h * HG_DV:2 * HG_F + D_MODEL + (h + 1) * HG_DV]
            o_ref[r0:r0 + c, h * HG_DV:(h + 1) * HG_DV] = _gated_rms(o, gate, gain).astype(o_ref.dtype)


def _hgrn_mixer(p, row0, nb, t, nseq, lb, gain, state, layer):
    c = min(CHUNK, t)
    nchunk = t // c
    rows = nseq * c
    blk0 = row0 // rows
    zero_init = state is None
    cols = p.shape[1]
    const = lambda shape: pl.BlockSpec(shape, lambda g, i: (0,) * len(shape))
    st_spec = pl.BlockSpec((nseq, HG_HEADS, HG_DK, HG_DV), lambda g, i: (g, 0, 0, 0))
    in_specs = [pl.BlockSpec((rows, cols), lambda g, i: (blk0 + g * nchunk + i, 0)), const(lb.shape), const(gain.shape)]
    args = [p, lb, gain]
    if not zero_init:
        in_specs.append(st_spec)
        args.append(state)
    return pl.pallas_call(
        functools.partial(_hgrn_mixer_kernel, c=c, nseq=nseq, zero_init=zero_init, layer=layer),
        grid=(nb // nseq, nchunk),
        in_specs=in_specs,
        out_specs=[pl.BlockSpec((rows, D_MODEL), lambda g, i: (g * nchunk + i, 0)), st_spec],
        out_shape=[jax.ShapeDtypeStruct((nb * t, D_MODEL), BF16),
                   jax.ShapeDtypeStruct((nb, HG_HEADS, HG_DK, HG_DV), F32)],
        compiler_params=_params(("parallel", "arbitrary")),
        name="hgrn_mixer",
    )(*args)


def _outproj_router_kernel(o_ref, x_ref, wout_ref, gain_ref, wr_ref, br_ref, x1_ref, h_ref, info_ref, cnt_ref, run_ref):
    @pl.when(pl.program_id(0) == 0)
    def _():
        run_ref[...] = jnp.zeros_like(run_ref)

    x1 = x_ref[...] + jnp.dot(o_ref[...], wout_ref[...], preferred_element_type=F32)
    x1_ref[...] = x1
    h = _rms(x1, gain_ref[...])
    h_ref[...] = h
    logit = _mm_hi(h, wr_ref[...]) + br_ref[...]
    lane = lax.broadcasted_iota(I32, logit.shape, 1)
    far = jnp.int32(LANES)
    lg = jnp.where(lane < N_GROUPS, logit, NEG)
    mg = jnp.max(lg, axis=-1, keepdims=True)
    p_g = 1.0 / jnp.sum(jnp.exp(lg - mg), axis=-1, keepdims=True)
    g_top = jnp.min(jnp.where(lg == mg, lane, far), axis=-1, keepdims=True)
    lo = N_GROUPS + g_top * EXPERTS_PER_GROUP
    sel = jnp.logical_and(lane >= lo, lane < lo + EXPERTS_PER_GROUP)
    le = jnp.where(sel, logit, NEG)
    ee = jnp.exp(le - jnp.max(le, axis=-1, keepdims=True))
    pe = jnp.where(sel, ee / jnp.sum(ee, axis=-1, keepdims=True), -1.0)
    p1 = jnp.max(pe, axis=-1, keepdims=True)
    i1 = jnp.min(jnp.where(pe == p1, lane, far), axis=-1, keepdims=True)
    pe2 = jnp.where(lane == i1, -1.0, pe)
    p2 = jnp.max(pe2, axis=-1, keepdims=True)
    i2 = jnp.min(jnp.where(pe2 == p2, lane, far), axis=-1, keepdims=True)
    denom = p1 + p2
    w1 = p_g * (p1 / denom)
    w2 = p_g * (p2 / denom)
    tm = logit.shape[0]
    oh1 = jnp.where(lane == i1, 1.0, 0.0)
    oh2 = jnp.where(lane == i2, 1.0, 0.0)
    both = oh1 + oh2
    tr = lax.broadcasted_iota(I32, (tm, tm), 0)
    tc = lax.broadcasted_iota(I32, (tm, tm), 1)
    before = run_ref[...] + _mm(jnp.where(tr > tc, 1.0, 0.0), both)
    r1 = jnp.sum(oh1 * before, axis=-1, keepdims=True)
    r2 = jnp.sum(oh2 * before, axis=-1, keepdims=True)
    run = run_ref[...] + jnp.sum(both, axis=0, keepdims=True)
    run_ref[...] = run
    cnt_ref[...] = jnp.broadcast_to(run, cnt_ref.shape)
    vals = ((i1 - N_GROUPS).astype(F32), (i2 - N_GROUPS).astype(F32), w1, w2, r1, r2)
    info = jnp.zeros_like(logit)
    for idx, val in enumerate(vals):
        info = jnp.where(lane == idx, val, info)
    info_ref[...] = info


def _outproj_router(o, x, wout, gain, wr, br):
    n, d = x.shape
    tm = TOKEN_TILE
    row = lambda w: pl.BlockSpec((tm, w), lambda i: (i, 0))
    const = lambda shape: pl.BlockSpec(shape, lambda i: (0,) * len(shape))
    return pl.pallas_call(
        _outproj_router_kernel,
        grid=(n // tm,),
        in_specs=[row(d), row(d), const((d, d)), const((1, d)), const((d, LANES)), const((1, LANES))],
        out_specs=[row(d), row(d), row(LANES), const((SUBLANES, LANES))],
        out_shape=[jax.ShapeDtypeStruct((n, d), F32), jax.ShapeDtypeStruct((n, d), F32),
                   jax.ShapeDtypeStruct((n, LANES), F32), jax.ShapeDtypeStruct((SUBLANES, LANES), F32)],
        scratch_shapes=[pltpu.VMEM((1, LANES), F32)],
        compiler_params=_params(("arbitrary",)),
        name="outproj_router",
    )(o, x, wout, gain, wr, br)


def _dispatch_kernel(pos_ref, lo_ref, hi_ref, h_ref, xs_hbm, buf, zrow, sem, zsem):
    i = pl.program_id(0)
    n = pl.num_programs(0)
    tm = h_ref.shape[0]
    slot = i % 2

    def tile_wait(s):
        for _ in range(TOP_K):
            pltpu.make_async_copy(buf.at[s], xs_hbm.at[pl.ds(0, tm)], sem.at[s]).wait()

    def pad_rows(fn, tile_fn):
        for e in range(N_EXPERTS):
            lax.fori_loop(lo_ref[e], hi_ref[e], fn, 0)
        lax.fori_loop(hi_ref[N_EXPERTS - 1] // tm, xs_hbm.shape[0] // tm, tile_fn, 0)

    def pad_start(r, carry):
        pltpu.make_async_copy(zrow.at[pl.ds(0, 1)], xs_hbm.at[pl.ds(r, 1)], zsem.at[0]).start()
        return carry

    def pad_wait(r, carry):
        pltpu.make_async_copy(zrow.at[pl.ds(0, 1)], xs_hbm.at[pl.ds(0, 1)], zsem.at[0]).wait()
        return carry

    def pad_tile_start(t, carry):
        pltpu.make_async_copy(zrow, xs_hbm.at[pl.ds(pl.multiple_of(t * tm, tm), tm)], zsem.at[0]).start()
        return carry

    def pad_tile_wait(t, carry):
        pltpu.make_async_copy(zrow, xs_hbm.at[pl.ds(0, tm)], zsem.at[0]).wait()
        return carry

    @pl.when(i == 0)
    def _():
        zrow[...] = jnp.zeros_like(zrow)
        pad_rows(pad_start, pad_tile_start)

    @pl.when(i >= 2)
    def _():
        tile_wait(slot)
    buf[slot] = h_ref[...]

    def body(r, carry):
        for k in range(TOP_K):
            row = pos_ref[TOP_K * (i * tm + r) + k]
            pltpu.make_async_copy(buf.at[slot, pl.ds(r, 1)], xs_hbm.at[pl.ds(row, 1)], sem.at[slot]).start()
        return carry
    lax.fori_loop(0, tm, body, 0, unroll=8)

    @pl.when(i == n - 1)
    def _():
        tile_wait(slot)

        @pl.when(n > 1)
        def _():
            tile_wait(1 - slot)
        pad_rows(pad_wait, pad_tile_wait)


def _dispatch(h, pos, pad_lo, pad_hi, n_rows):
    n, d = h.shape
    tm = TOKEN_TILE
    grid_spec = pltpu.PrefetchScalarGridSpec(
        num_scalar_prefetch=3,
        grid=(n // tm,),
        in_specs=[pl.BlockSpec((tm, d), lambda i, pos, lo, hi: (i, 0))],
        out_specs=pl.BlockSpec(memory_space=pl.ANY),
        scratch_shapes=[pltpu.VMEM((2, tm, d), F32), pltpu.VMEM((tm, d), F32),
                        pltpu.SemaphoreType.DMA((2,)), pltpu.SemaphoreType.DMA((1,))],
    )
    return pl.pallas_call(
        _dispatch_kernel,
        grid_spec=grid_spec,
        out_shape=jax.ShapeDtypeStruct((n_rows, d), F32),
        compiler_params=_params(("arbitrary",)),
        name="moe_dispatch",
    )(pos, pad_lo, pad_hi, h)


def _moe_kernel(te_ref, nact_ref, x_ref, w1_ref, w3_ref, w2_ref, y_ref, w1b, w3b, w2b):
    i = pl.program_id(0)

    @pl.when(i < nact_ref[0])
    def _():
        @pl.when(jnp.logical_or(i == 0, te_ref[i] != te_ref[jnp.maximum(i - 1, 0)]))
        def _():
            w1b[...] = w1_ref[0].astype(BF16)
            w3b[...] = w3_ref[0].astype(BF16)
            w2b[...] = w2_ref[0].astype(BF16)

        xb = x_ref[...].astype(BF16)
        h1 = jnp.dot(xb, w1b[...], preferred_element_type=F32)
        h3 = jnp.dot(xb, w3b[...], preferred_element_type=F32)
        act = (h1 * _sigmoid(h1)) * h3
        y_ref[...] = jnp.dot(act.astype(BF16), w2b[...], preferred_element_type=F32)

    @pl.when(i >= nact_ref[0])
    def _():
        y_ref[...] = jnp.zeros_like(y_ref)


def _moe(xs, te, nact, w1, w3, w2):
    p_rows, d = xs.shape
    f = w1.shape[2]
    tm = MOE_TILE
    last = lambda i, nact: jnp.maximum(jnp.minimum(i, nact[0] - 1), 0)
    grid_spec = pltpu.PrefetchScalarGridSpec(
        num_scalar_prefetch=2,
        grid=(p_rows // tm,),
        in_specs=[
            pl.BlockSpec((tm, d), lambda i, te, nact: (last(i, nact), 0)),
            pl.BlockSpec((1, d, f), lambda i, te, nact: (te[i], 0, 0)),
            pl.BlockSpec((1, d, f), lambda i, te, nact: (te[i], 0, 0)),
            pl.BlockSpec((1, f, d), lambda i, te, nact: (te[i], 0, 0)),
        ],
        out_specs=pl.BlockSpec((tm, d), lambda i, te, nact: (i, 0)),
        scratch_shapes=[pltpu.VMEM((d, f), BF16), pltpu.VMEM((d, f), BF16), pltpu.VMEM((f, d), BF16)],
    )
    return pl.pallas_call(
        _moe_kernel,
        grid_spec=grid_spec,
        out_shape=jax.ShapeDtypeStruct((p_rows, d), F32),
        compiler_params=_params(("arbitrary",)),
        name="moe_experts",
    )(te, nact, xs, w1, w3, w2)


def _moe_rows(n):
    return ((TOP_K * n + N_EXPERTS * (MOE_TILE - 1)) // MOE_TILE) * MOE_TILE


def _route_tables(info, counts_row, n):
    tm = MOE_TILE
    n_tiles = _moe_rows(n) // tm
    ids = jnp.arange(N_EXPERTS, dtype=I32)
    counts = counts_row[0, N_GROUPS:N_GROUPS + N_EXPERTS].astype(I32)
    padded = ((counts + tm - 1) // tm) * tm
    cum = jnp.cumsum(padded)
    off = cum - padded
    e = info[:, INFO_E:INFO_E + TOP_K].astype(I32)
    rank = info[:, INFO_R:INFO_R + TOP_K].astype(I32)
    pos = (jnp.sum(jnp.where(e[:, :, None] == ids, off, 0), axis=-1) + rank).reshape(TOP_K * n)
    tile_start = jnp.arange(n_tiles, dtype=I32) * tm
    nact = cum[-1] // tm
    te_raw = jnp.sum((cum[None, :] <= tile_start[:, None]).astype(I32), axis=1)
    te_last = jnp.sum((cum <= (nact - 1) * tm).astype(I32))
    te = jnp.where(tile_start < cum[-1], jnp.minimum(te_raw, N_EXPERTS - 1), te_last)
    return pos, off + counts, cum, te, nact.reshape(1)


def _pad_lanes(a, width=LANES):
    return jnp.pad(a, [(0, 0)] * (a.ndim - 1) + [(0, width - a.shape[-1])])


def _sample_group(nb):
    g = 8
    while nb % g:
        g //= 2
    return g


def kernel(x_prompt, x_sample, state_conv_ab, state_delta_ab, state_gla_ab, state_hgrn_c, mix_norm, ab_w_in, ab_conv_w, ab_a_log, ab_dt_bias, ab_dn_norm, ab_gk_w2, ab_gk_b, ab_gla_norm, ab_w_out, c_w_in, c_lower_bounds, c_norm, c_w_out, ffn_norm, moe_w_group, moe_b_group, moe_w_expert, moe_b_expert, moe_w1, moe_w3, moe_w2, final_norm):
    bp, tp, d = x_prompt.shape
    bs, ts, _ = x_sample.shape
    n_p, n_s = bp * tp, bs * ts
    n = n_p + n_s
    depth = mix_norm.shape[0]
    x = jnp.concatenate([x_prompt.reshape(n_p, d), x_sample.reshape(n_s, d)], axis=0)
    sg = _sample_group(bs)

    info = pos = ys = None
    conv_p, delta_p, gla_p, hgrn_p = [], [], [], []
    conv_s, delta_s, gla_s, hgrn_s = [], [], [], []
    for layer in range(depth):
        j = layer // 2
        gain = mix_norm[layer].reshape(1, d)
        if layer % 2 == 0:
            u_w, a_w, b_w, z_w, gq_w, gk_w, gv_w, lr_w, gg_w = jnp.split(ab_w_in[j], _split_points(AB_SPLITS), axis=1)
            w_in = jnp.concatenate([u_w, z_w, gq_w, gk_w, gv_w, gg_w, _pad_lanes(jnp.concatenate([a_w, b_w, lr_w], axis=1))],
                                   axis=1).astype(BF16)
        else:
            w_in = c_w_in[j].astype(BF16)
        if ys is None:
            p = _norm_proj(x, gain, w_in)
        else:
            x, p = _combine_norm(x, info, pos, ys, gain, w_in)

        if layer % 2 == 0:
            prm = jnp.concatenate([_pad_lanes(ab_a_log[j].reshape(1, -1)), _pad_lanes(ab_dt_bias[j].reshape(1, -1)),
                                   ab_dn_norm[j].reshape(1, -1), ab_gla_norm[j].reshape(1, -1),
                                   jnp.zeros((4, LANES), F32)], axis=0)
            gkw = jnp.zeros((LANES, GLA_K), F32).at[MISC_LR:MISC_LR + GLA_RANK].set(ab_gk_w2[j])
            gkb = ab_gk_b[j].reshape(1, GLA_K)
            cw = ab_conv_w[j]
            o_p, c8, sd, sgl = _ab_mixer(p, n_p, 0, bp, tp, 1, cw, prm, gkw, gkb, None)
            conv_p.append(c8[:, SUBLANES - (CONV_W - 1):])
            delta_p.append(sd)
            gla_p.append(sgl)
            conv8 = jnp.pad(state_conv_ab[j], ((0, 0), (SUBLANES - (CONV_W - 1), 0), (0, 0)))
            o_s, c8, sd, sgl = _ab_mixer(p, n_s, n_p, bs, ts, sg, cw, prm, gkw, gkb,
                                         (conv8, state_delta_ab[j], state_gla_ab[j]))
            conv_s.append(c8[:, SUBLANES - (CONV_W - 1):])
            delta_s.append(sd)
            gla_s.append(sgl)
            w_out = ab_w_out[j]
        else:
            hg_gain = c_norm[j].reshape(1, -1)
            o_p, sh = _hgrn_mixer(p, 0, bp, tp, 1, c_lower_bounds, hg_gain, None, layer)
            hgrn_p.append(sh)
            o_s, sh = _hgrn_mixer(p, n_p, bs, ts, sg, c_lower_bounds, hg_gain, state_hgrn_c[j], layer)
            hgrn_s.append(sh)
            w_out = c_w_out[j]
        o = jnp.concatenate([o_p, o_s], axis=0)

        wr = _pad_lanes(jnp.concatenate([moe_w_group[layer], moe_w_expert[layer]], axis=1))
        br = _pad_lanes(jnp.concatenate([moe_b_group[layer], moe_b_expert[layer]]).reshape(1, -1))
        x, hn, info, counts = _outproj_router(o, x, w_out.astype(BF16), ffn_norm[layer].reshape(1, d), wr, br)
        pos, pad_lo, pad_hi, te, nact = _route_tables(info, counts, n)
        xs = _dispatch(hn, pos, pad_lo, pad_hi, _moe_rows(n))
        ys = _moe(xs, te, nact, moe_w1[layer], moe_w3[layer], moe_w2[layer])

    y = _combine_norm(x, info, pos, ys, final_norm.reshape(1, d), None)
    y_prompt = y[:n_p].reshape(bp, tp, d)
    y_sample = y[n_p:].reshape(bs, ts, d)
    return (y_prompt, y_sample, jnp.stack(conv_p), jnp.stack(delta_p), jnp.stack(gla_p), jnp.stack(hgrn_p),
            jnp.stack(conv_s), jnp.stack(delta_s), jnp.stack(gla_s), jnp.stack(hgrn_s))


def _split_points(sizes):
    pts, acc = [], 0
    for s in sizes[:-1]:
        acc += s
        pts.append(acc)
    return pts
```

```python
import functools

import jax
import jax.numpy as jnp
from jax import lax
from jax.experimental import pallas as pl
from jax.experimental.pallas import tpu as pltpu

F32 = jnp.float32
BF16 = jnp.bfloat16
I32 = jnp.int32

D_MODEL = 1024
EPS = 1e-6
CHUNK = 64
CONV_W = 4
DN_HEADS, DN_DK, DN_DV = 4, 128, 128
DN_QK = DN_HEADS * DN_DK
DN_V = DN_HEADS * DN_DV
CONV_CH = 2 * DN_QK + DN_V
GLA_HEADS, GLA_DK, GLA_DV = 4, 64, 128
GLA_K = GLA_HEADS * GLA_DK
GLA_V = GLA_HEADS * GLA_DV
GLA_RANK = 16
GLA_NORMALIZER = 16.0
GLA_PACK = 128 // GLA_DK
AB_SPLITS = (CONV_CH, DN_HEADS, DN_HEADS, DN_V, GLA_K, GLA_K, GLA_V, GLA_RANK, GLA_V)
HG_HEADS, HG_DK, HG_DV = 8, 128, 128
HG_F = HG_HEADS * HG_DK
N_GROUPS, EXPERTS_PER_GROUP = 4, 8
N_EXPERTS = N_GROUPS * EXPERTS_PER_GROUP
TOP_K = 2
D_EXPERT = 512

LANES = 128
SUBLANES = 8
VMEM_LIMIT_BYTES = 56 * 1024 * 1024

AB_U, AB_Z, AB_GQ, AB_GK, AB_GV, AB_GG, AB_MISC = 0, 1536, 2048, 2304, 2560, 3072, 3584
AB_COLS = AB_MISC + LANES
MISC_A, MISC_B, MISC_LR = 0, DN_HEADS, 2 * DN_HEADS

INFO_E, INFO_W, INFO_R = 0, TOP_K, 2 * TOP_K

TOKEN_TILE = 256
MOE_TILE = 256
NEG = -1e30


def _params(sem):
    return pltpu.CompilerParams(dimension_semantics=sem, vmem_limit_bytes=VMEM_LIMIT_BYTES)


def _mm(a, b):
    return jnp.dot(a.astype(BF16), b.astype(BF16), preferred_element_type=F32)


def _mm_nt(a, b):
    return lax.dot_general(a.astype(BF16), b.astype(BF16), (((1,), (1,)), ((), ())), preferred_element_type=F32)


def _mm_tn(a, b):
    return lax.dot_general(a.astype(BF16), b.astype(BF16), (((0,), (0,)), ((), ())), preferred_element_type=F32)


def _mm_hi(a, b):
    return jnp.dot(a, b, preferred_element_type=F32, precision=lax.Precision.HIGHEST)


def _sigmoid(x):
    return 1.0 / (1.0 + jnp.exp(-x))


def _softplus(x):
    return jnp.maximum(x, 0.0) + jnp.log(1.0 + jnp.exp(-jnp.abs(x)))


def _rms(x, gain):
    return x * lax.rsqrt(jnp.mean(x * x, axis=-1, keepdims=True) + EPS) * gain


def _gated_rms(o, gate, gain):
    return _rms(o, gain) * (gate * _sigmoid(gate))


def _row_to_col(row):
    n = row.shape[1]
    r = lax.broadcasted_iota(I32, (n, n), 0)
    c = lax.broadcasted_iota(I32, (n, n), 1)
    return jnp.sum(jnp.where(r == c, jnp.broadcast_to(row, (n, n)), 0.0), axis=1, keepdims=True)


def _split2(x):
    hi = x.astype(BF16)
    return hi, (x - hi.astype(F32)).astype(BF16)


def _split3(x):
    hi = x.astype(BF16)
    r = x - hi.astype(F32)
    mid = r.astype(BF16)
    return hi, mid, (r - mid.astype(F32)).astype(BF16)


def _mm_cum(tri, x):
    hi, mid, lo = _split3(x)
    dot = lambda p: jnp.dot(tri, p, preferred_element_type=F32)
    return dot(hi) + (dot(mid) + dot(lo))


def _mm_split(a, b):
    ah, al = _split2(a)
    bh, bl = _split2(b)
    dot = lambda x, y: jnp.dot(x, y, preferred_element_type=F32)
    return dot(ah, bh) + (dot(ah, bl) + dot(al, bh))


def _delta_chunks(qs_, ks_, vs_, gcs_, grs_, betas_, ss_):
    heads = range(len(qs_))
    c = qs_[0].shape[0]
    ti = lax.broadcasted_iota(I32, (c, c), 0)
    tj = lax.broadcasted_iota(I32, (c, c), 1)
    eye = jnp.where(ti == tj, 1.0, 0.0)
    decay = [jnp.exp(jnp.where(ti >= tj, gcs_[h] - grs_[h], NEG)) for h in heads]
    qs = [qs_[h] * (DN_DK ** -0.5) for h in heads]
    kk = [_mm_nt(ks_[h], ks_[h]) for h in heads]
    nm = [jnp.where(ti > tj, betas_[h] * kk[h] * decay[h], 0.0) for h in heads]
    t = [eye - nm[h] for h in heads]
    p = nm
    step = 2
    while step < c:
        p = [_mm(p[h], p[h]) for h in heads]
        t = [t[h] + _mm(t[h], p[h]) for h in heads]
        step *= 2
    eg = [jnp.exp(gcs_[h]) for h in heads]
    rhs = [jnp.concatenate([betas_[h] * vs_[h], (betas_[h] * eg[h]) * ks_[h]], axis=1) for h in heads]
    sol = [_mm(t[h], rhs[h]) for h in heads]
    resid = [rhs[h] - (sol[h] + _mm_split(nm[h], sol[h])) for h in heads]
    sol = [sol[h] + _mm(t[h], resid[h]) for h in heads]
    qk = [_mm_nt(qs[h], ks_[h]) * decay[h] for h in heads]
    u = [sol[h][:, :DN_DV] - _mm(sol[h][:, DN_DV:], ss_[h]) for h in heads]
    o = [_mm(qs[h] * eg[h], ss_[h]) + _mm(qk[h], u[h]) for h in heads]
    gl = [gcs_[h][c - 1:c] for h in heads]
    s_new = [ss_[h] * jnp.exp(gl[h]) + _mm_tn(ks_[h] * jnp.exp(gl[h] - gcs_[h]), u[h]) for h in heads]
    return o, s_new


def _gla_chunk(q, k, g, vs, s, m):
    c = q.shape[0]
    nh = len(vs)
    dk = LANES // nh
    dlane = lax.broadcasted_iota(I32, (1, LANES), 1)
    if nh == 1:
        mask = lambda x, h: x
    else:
        hm = [jnp.where(jnp.logical_and(dlane >= h * dk, dlane < (h + 1) * dk), 1.0, 0.0) for h in range(nh)]
        mask = lambda x, h: x * hm[h]
    qeg = q * jnp.exp(g)
    o = [_mm(mask(qeg, h), s) for h in range(nh)]
    jrow = lax.broadcasted_iota(I32, (c, 1), 0)
    lane = lax.broadcasted_iota(I32, (m, c), 1)
    sub = lax.broadcasted_iota(I32, (m, 1), 0)
    blocks = [[] for _ in range(nh)]
    for blk in range(c // m):
        r0 = blk * m
        qb, gb, kb = q[r0:r0 + m], g[r0:r0 + m], k[r0:r0 + m]
        if blk > 0:
            base = g[r0 - 1:r0]
            kt = k * jnp.exp(jnp.where(jrow < r0, base - g, NEG))
            qt = qb * jnp.exp(gb - base)
            att = [_mm_nt(mask(qt, h), kt) for h in range(nh)]
        else:
            att = [jnp.zeros((m, c), F32) for _ in range(nh)]
        for j in range(m):
            e = jnp.exp(jnp.where(sub >= j, gb - gb[j:j + 1], NEG))
            prod = qb * (kb[j:j + 1] * e)
            for h in range(nh):
                col = jnp.sum(mask(prod, h), axis=1, keepdims=True)
                att[h] = jnp.where(lane == r0 + j, col, att[h])
        for h in range(nh):
            blocks[h].append(att[h])
    gl = g[c - 1:c]
    kg = k * jnp.exp(gl - g)
    s_new = s * _row_to_col(jnp.exp(gl))
    for h in range(nh):
        att = blocks[h][0] if len(blocks[h]) == 1 else jnp.concatenate(blocks[h], axis=0)
        o[h] = o[h] + _mm(att, vs[h])
        s_new = s_new + _mm_tn(mask(kg, h), vs[h])
    return o, s_new


def _tri_incl(c):
    r = lax.broadcasted_iota(I32, (c, c), 0)
    cc = lax.broadcasted_iota(I32, (c, c), 1)
    return jnp.where(r >= cc, 1.0, 0.0).astype(BF16)


def _norm_proj_kernel(x_ref, gain_ref, w_ref, p_ref):
    h = _rms(x_ref[...], gain_ref[...])
    p_ref[...] = jnp.dot(h.astype(BF16), w_ref[...], preferred_element_type=F32)


def _norm_proj(x, gain, w):
    n, d = x.shape
    m = w.shape[1]
    tm = TOKEN_TILE
    return pl.pallas_call(
        _norm_proj_kernel,
        grid=(n // tm,),
        in_specs=[pl.BlockSpec((tm, d), lambda i: (i, 0)), pl.BlockSpec((1, d), lambda i: (0, 0)),
                  pl.BlockSpec((d, m), lambda i: (0, 0))],
        out_specs=pl.BlockSpec((tm, m), lambda i: (i, 0)),
        out_shape=jax.ShapeDtypeStruct((n, m), F32),
        compiler_params=_params(("parallel",)),
        name="norm_proj",
    )(x, gain, w)


def _combine_norm_kernel(pos_ref, *refs, project, tiles_a):
    if project:
        x_ref, info_ref, gain_ref, w_ref, ys_hbm, xs_ref, out_ref, ybuf, sem = refs
    else:
        x_ref, info_ref, gain_ref, ys_hbm, outa_ref, outb_ref, ybuf, sem = refs
    i = pl.program_id(0)
    n = pl.num_programs(0)
    tm = x_ref.shape[0]
    slot = i % 2

    def gather_start(tile, s):
        def body(r, carry):
            for k in range(TOP_K):
                row = pos_ref[TOP_K * (tile * tm + r) + k]
                pltpu.make_async_copy(ys_hbm.at[pl.ds(row, 1)], ybuf.at[s, k, pl.ds(r, 1)], sem.at[s]).start(priority=k % 2)
            return carry
        lax.fori_loop(0, tm, body, 0, unroll=8)

    @pl.when(i == 0)
    def _():
        gather_start(0, 0)

    @pl.when(i + 1 < n)
    def _():
        gather_start(i + 1, 1 - slot)

    for k in range(TOP_K):
        pltpu.make_async_copy(ys_hbm.at[pl.ds(0, tm)], ybuf.at[slot, k], sem.at[slot]).wait()
    info = info_ref[...]
    x = x_ref[...] + (info[:, INFO_W:INFO_W + 1] * ybuf[slot, 0] + info[:, INFO_W + 1:INFO_W + 2] * ybuf[slot, 1])
    h = _rms(x, gain_ref[...])
    if project:
        xs_ref[...] = x
        out_ref[...] = jnp.dot(h.astype(BF16), w_ref[...], preferred_element_type=F32)
    else:
        @pl.when(i < tiles_a)
        def _():
            outa_ref[...] = h

        @pl.when(i >= tiles_a)
        def _():
            outb_ref[...] = h


def _combine_norm(x, info, pos, ys, gain, w, n_a=None):
    n, d = x.shape
    tm = TOKEN_TILE
    project = w is not None
    tiles_a = None if project else n_a // tm
    row = lambda width: pl.BlockSpec((tm, width), lambda i, pos: (i, 0))
    in_specs = [row(d), row(LANES), pl.BlockSpec((1, d), lambda i, pos: (0, 0))]
    args = [x, info, gain]
    if project:
        m = w.shape[1]
        in_specs.append(pl.BlockSpec((d, m), lambda i, pos: (0, 0)))
        args.append(w)
        out_specs = [row(d), row(m)]
        out_shape = [jax.ShapeDtypeStruct((n, d), F32), jax.ShapeDtypeStruct((n, m), F32)]
    else:
        out_specs = [pl.BlockSpec((tm, d), lambda i, pos: (jnp.minimum(i, tiles_a - 1), 0)),
                     pl.BlockSpec((tm, d), lambda i, pos: (jnp.maximum(i - tiles_a, 0), 0))]
        out_shape = [jax.ShapeDtypeStruct((n_a, d), F32), jax.ShapeDtypeStruct((n - n_a, d), F32)]
    in_specs.append(pl.BlockSpec(memory_space=pl.ANY))
    args.append(ys)
    grid_spec = pltpu.PrefetchScalarGridSpec(
        num_scalar_prefetch=1, grid=(n // tm,), in_specs=in_specs, out_specs=out_specs,
        scratch_shapes=[pltpu.VMEM((2, TOP_K, tm, d), F32), pltpu.SemaphoreType.DMA((2,))])
    return pl.pallas_call(
        functools.partial(_combine_norm_kernel, project=project, tiles_a=tiles_a),
        grid_spec=grid_spec,
        out_shape=out_shape,
        compiler_params=_params(("arbitrary",)),
        name="combine_norm",
    )(pos, *args)


def _ab_mixer_kernel(*refs, c, nseq, zero_init):
    if zero_init:
        p_ref, cw_ref, prm_ref, gkw_ref, gkb_ref, o_ref, conv_ref, sdn_ref, sgla_ref = refs
    else:
        (p_ref, cw_ref, prm_ref, gkw_ref, gkb_ref, conv_in, sdn_in, sgla_in,
         o_ref, conv_ref, sdn_ref, sgla_ref) = refs

    @pl.when(pl.program_id(1) == 0)
    def _():
        if zero_init:
            conv_ref[...] = jnp.zeros_like(conv_ref)
            sdn_ref[...] = jnp.zeros_like(sdn_ref)
            sgla_ref[...] = jnp.zeros_like(sgla_ref)
        else:
            conv_ref[...] = conv_in[...]
            sdn_ref[...] = sdn_in[...]
            sgla_ref[...] = sgla_in[...]

    m = min(16, c)
    tri = _tri_incl(c)
    cw = cw_ref[...]
    a_log, dt_bias = prm_ref[0:1], prm_ref[1:2]
    dn_gain, gla_gain = prm_ref[2:3], prm_ref[3:4]
    for b in range(nseq):
        r0 = b * c
        u = p_ref[r0:r0 + c, AB_U:AB_U + CONV_CH]
        ucat = jnp.concatenate([conv_ref[b], u], axis=0)
        acc = u * cw[CONV_W - 1:CONV_W]
        for j in range(1, CONV_W):
            acc = acc + pltpu.roll(ucat, j, 0)[SUBLANES:SUBLANES + c] * cw[CONV_W - 1 - j:CONV_W - j]
        conv_ref[b] = ucat[c:c + SUBLANES]
        qkv = acc * _sigmoid(acc)

        misc = p_ref[r0:r0 + c, AB_MISC:AB_MISC + LANES]
        g_all = -jnp.exp(a_log) * _softplus(misc + dt_bias)
        beta_all = _sigmoid(misc)
        gcum = _mm_cum(tri, g_all)
        gcum_t = jnp.concatenate([gcum, jnp.zeros((LANES - c, LANES), F32)], axis=0).T

        qs, ks, vs = [], [], []
        for h in range(DN_HEADS):
            lo = h * DN_DK
            q = qkv[:, lo:lo + DN_DK]
            k = qkv[:, DN_QK + lo:DN_QK + lo + DN_DK]
            qs.append(q * lax.rsqrt(jnp.sum(q * q, axis=-1, keepdims=True) + 1e-6))
            ks.append(k * lax.rsqrt(jnp.sum(k * k, axis=-1, keepdims=True) + 1e-6))
            vs.append(qkv[:, 2 * DN_QK + h * DN_DV:2 * DN_QK + (h + 1) * DN_DV])
        os_, ss_ = _delta_chunks(
            qs, ks, vs,
            [gcum[:, MISC_A + h:MISC_A + h + 1] for h in range(DN_HEADS)],
            [gcum_t[MISC_A + h:MISC_A + h + 1, :c] for h in range(DN_HEADS)],
            [beta_all[:, MISC_B + h:MISC_B + h + 1] for h in range(DN_HEADS)],
            [sdn_ref[b, h] for h in range(DN_HEADS)])
        for h in range(DN_HEADS):
            sdn_ref[b, h] = ss_[h]
            z = p_ref[r0:r0 + c, AB_Z + h * DN_DV:AB_Z + (h + 1) * DN_DV]
            o_ref[r0:r0 + c, h * DN_DV:(h + 1) * DN_DV] = _gated_rms(os_[h], z, dn_gain).astype(o_ref.dtype)

        logits = _mm_split(misc, gkw_ref[...]) + gkb_ref[...]
        logf = -_softplus(-logits) * (1.0 / GLA_NORMALIZER)
        gall = _mm_cum(tri, logf)
        for pair in range(GLA_HEADS // GLA_PACK):
            lo = pair * LANES
            heads = [pair * GLA_PACK + i for i in range(GLA_PACK)]
            q = p_ref[r0:r0 + c, AB_GQ + lo:AB_GQ + lo + LANES] * (GLA_DK ** -0.5)
            k = p_ref[r0:r0 + c, AB_GK + lo:AB_GK + lo + LANES]
            vs = [p_ref[r0:r0 + c, AB_GV + h * GLA_DV:AB_GV + (h + 1) * GLA_DV] for h in heads]
            os_, s_new = _gla_chunk(q, k, gall[:, lo:lo + LANES], vs, sgla_ref[b, pair], m)
            sgla_ref[b, pair] = s_new
            for o, h in zip(os_, heads):
                gate = p_ref[r0:r0 + c, AB_GG + h * GLA_DV:AB_GG + (h + 1) * GLA_DV]
                o_ref[r0:r0 + c, DN_V + h * GLA_DV:DN_V + (h + 1) * GLA_DV] = _gated_rms(o, gate, gla_gain).astype(o_ref.dtype)


def _ab_mixer(p, o_alias_rows, row0, nb, t, nseq, cw, prm, gkw, gkb, states):
    c = min(CHUNK, t)
    nchunk = t // c
    rows = nseq * c
    blk0 = row0 // rows
    n_groups = nb // nseq
    zero_init = states is None
    row_map = lambda g, i: (blk0 + g * nchunk + i, 0)
    st_map3 = lambda g, i: (g, 0, 0)
    st_map4 = lambda g, i: (g, 0, 0, 0)
    const = lambda shape: pl.BlockSpec(shape, lambda g, i: (0,) * len(shape))
    in_specs = [pl.BlockSpec((rows, AB_COLS), row_map), const(cw.shape), const(prm.shape), const(gkw.shape),
                const(gkb.shape)]
    args = [p, cw, prm, gkw, gkb]
    gla_packed = (GLA_HEADS // GLA_PACK, GLA_PACK * GLA_DK, GLA_DV)
    st_specs = [pl.BlockSpec((nseq, SUBLANES, CONV_CH), st_map3),
                pl.BlockSpec((nseq, DN_HEADS, DN_DK, DN_DV), st_map4),
                pl.BlockSpec((nseq,) + gla_packed, st_map4)]
    if not zero_init:
        conv8, sdn, sgla = states
        in_specs += st_specs
        args += [conv8, sdn, sgla.reshape((nb,) + gla_packed)]
    out_shape = [jax.ShapeDtypeStruct((o_alias_rows, D_MODEL), BF16),
                 jax.ShapeDtypeStruct((nb, SUBLANES, CONV_CH), F32),
                 jax.ShapeDtypeStruct((nb, DN_HEADS, DN_DK, DN_DV), F32),
                 jax.ShapeDtypeStruct((nb,) + gla_packed, F32)]
    out_specs = [pl.BlockSpec((rows, D_MODEL), lambda g, i: (g * nchunk + i, 0))] + st_specs
    o, conv_out, sdn_out, sgla_out = pl.pallas_call(
        functools.partial(_ab_mixer_kernel, c=c, nseq=nseq, zero_init=zero_init),
        grid=(n_groups, nchunk),
        in_specs=in_specs,
        out_specs=out_specs,
        out_shape=out_shape,
        compiler_params=_params(("parallel", "arbitrary")),
        name="ab_mixer",
    )(*args)
    return o, conv_out, sdn_out, sgla_out.reshape(nb, GLA_HEADS, GLA_DK, GLA_DV)


def _hgrn_mixer_kernel(*refs, c, nseq, zero_init, layer):
    if zero_init:
        p_ref, lb_ref, gain_ref, o_ref, s_ref = refs
    else:
        p_ref, lb_ref, gain_ref, s_in, o_ref, s_ref = refs

    @pl.when(pl.program_id(1) == 0)
    def _():
        if zero_init:
            s_ref[...] = jnp.zeros_like(s_ref)
        else:
            s_ref[...] = s_in[...]

    raw = lb_ref[...]
    e = jnp.exp(raw - jnp.max(raw, axis=0, keepdims=True))
    sm = e / jnp.sum(e, axis=0, keepdims=True)
    cum = sm[0:1]
    for l in range(1, layer + 1):
        cum = cum + sm[l:l + 1]
    lb = cum - sm[0:1]

    m = min(16, c)
    tri = _tri_incl(c)
    gain = gain_ref[...]
    for b in range(nseq):
        r0 = b * c
        fr = p_ref[r0:r0 + c, HG_F:2 * HG_F]
        logf = jnp.log(lb + (1.0 - lb) * _sigmoid(fr))
        kall = (1.0 - lb) * _sigmoid(-fr)
        gall = _mm_cum(tri, logf)
        for h in range(HG_HEADS):
            lo = h * HG_DK
            q = p_ref[r0:r0 + c, lo:lo + HG_DK] * (HG_DK ** -0.5)
            v = p_ref[r0:r0 + c, 2 * HG_F + h * HG_DV:2 * HG_F + (h + 1) * HG_DV]
            (o,), s_new = _gla_chunk(q, kall[:, lo:lo + HG_DK], gall[:, lo:lo + HG_DK], [v], s_ref[b, h], m)
            s_ref[b, h] = s_new
            gate = p_ref[r0:r0 + c, 2 * HG_F + D_MODEL + h * HG_DV:2 * HG_F + D_MODEL + (h + 1) * HG_DV]
            o_ref[r0:r0 + c, h * HG_DV:(h + 1) * HG_DV] = _gated_rms(o, gate, gain).astype(o_ref.dtype)


def _hgrn_mixer(p, row0, nb, t, nseq, lb, gain, state, layer):
    c = min(CHUNK, t)
    nchunk = t // c
    rows = nseq * c
    blk0 = row0 // rows
    zero_init = state is None
    cols = p.shape[1]
    const = lambda shape: pl.BlockSpec(shape, lambda g, i: (0,) * len(shape))
    st_spec = pl.BlockSpec((nseq, HG_HEADS, HG_DK, HG_DV), lambda g, i: (g, 0, 0, 0))
    in_specs = [pl.BlockSpec((rows, cols), lambda g, i: (blk0 + g * nchunk + i, 0)), const(lb.shape), const(gain.shape)]
    args = [p, lb, gain]
    if not zero_init:
        in_specs.append(st_spec)
        args.append(state)
    return pl.pallas_call(
        functools.partial(_hgrn_mixer_kernel, c=c, nseq=nseq, zero_init=zero_init, layer=layer),
        grid=(nb // nseq, nchunk),
        in_specs=in_specs,
        out_specs=[pl.BlockSpec((rows, D_MODEL), lambda g, i: (g * nchunk + i, 0)), st_spec],
        out_shape=[jax.ShapeDtypeStruct((nb * t, D_MODEL), BF16),
                   jax.ShapeDtypeStruct((nb, HG_HEADS, HG_DK, HG_DV), F32)],
        compiler_params=_params(("parallel", "arbitrary")),
        name="hgrn_mixer",
    )(*args)


def _outproj_router_kernel(*refs, tiles_a, split_x):
    if split_x:
        oa_ref, ob_ref, xa_ref, xb_ref = refs[:4]
        refs = refs[4:]
    else:
        oa_ref, ob_ref, x_ref = refs[:3]
        refs = refs[3:]
    wout_ref, gain_ref, wr_ref, br_ref, x1_ref, h_ref, info_ref, cnt_ref, run_ref = refs
    first = pl.program_id(0) < tiles_a

    @pl.when(pl.program_id(0) == 0)
    def _():
        run_ref[...] = jnp.zeros_like(run_ref)

    o = jnp.where(first, oa_ref[...], ob_ref[...])
    x = jnp.where(first, xa_ref[...], xb_ref[...]) if split_x else x_ref[...]
    x1 = x + jnp.dot(o, wout_ref[...], preferred_element_type=F32)
    x1_ref[...] = x1
    h = _rms(x1, gain_ref[...])
    h_ref[...] = h
    logit = _mm_hi(h, wr_ref[...]) + br_ref[...]
    lane = lax.broadcasted_iota(I32, logit.shape, 1)
    far = jnp.int32(LANES)
    lg = jnp.where(lane < N_GROUPS, logit, NEG)
    mg = jnp.max(lg, axis=-1, keepdims=True)
    p_g = 1.0 / jnp.sum(jnp.exp(lg - mg), axis=-1, keepdims=True)
    g_top = jnp.min(jnp.where(lg == mg, lane, far), axis=-1, keepdims=True)
    lo = N_GROUPS + g_top * EXPERTS_PER_GROUP
    sel = jnp.logical_and(lane >= lo, lane < lo + EXPERTS_PER_GROUP)
    le = jnp.where(sel, logit, NEG)
    ee = jnp.exp(le - jnp.max(le, axis=-1, keepdims=True))
    pe = jnp.where(sel, ee / jnp.sum(ee, axis=-1, keepdims=True), -1.0)
    p1 = jnp.max(pe, axis=-1, keepdims=True)
    i1 = jnp.min(jnp.where(pe == p1, lane, far), axis=-1, keepdims=True)
    pe2 = jnp.where(lane == i1, -1.0, pe)
    p2 = jnp.max(pe2, axis=-1, keepdims=True)
    i2 = jnp.min(jnp.where(pe2 == p2, lane, far), axis=-1, keepdims=True)
    denom = p1 + p2
    w1 = p_g * (p1 / denom)
    w2 = p_g * (p2 / denom)
    tm = logit.shape[0]
    oh1 = jnp.where(lane == i1, 1.0, 0.0)
    oh2 = jnp.where(lane == i2, 1.0, 0.0)
    both = oh1 + oh2
    tr = lax.broadcasted_iota(I32, (tm, tm), 0)
    tc = lax.broadcasted_iota(I32, (tm, tm), 1)
    before = run_ref[...] + _mm(jnp.where(tr > tc, 1.0, 0.0), both)
    r1 = jnp.sum(oh1 * before, axis=-1, keepdims=True)
    r2 = jnp.sum(oh2 * before, axis=-1, keepdims=True)
    run = run_ref[...] + jnp.sum(both, axis=0, keepdims=True)
    run_ref[...] = run
    cnt_ref[...] = jnp.broadcast_to(run, cnt_ref.shape)
    vals = ((i1 - N_GROUPS).astype(F32), (i2 - N_GROUPS).astype(F32), w1, w2, r1, r2)
    info = jnp.zeros_like(logit)
    for idx, val in enumerate(vals):
        info = jnp.where(lane == idx, val, info)
    info_ref[...] = info


def _outproj_router(o_a, o_b, xs, wout, gain, wr, br):
    n_a, d = o_a.shape
    n = n_a + o_b.shape[0]
    tm = TOKEN_TILE
    tiles_a = n_a // tm
    row = lambda w: pl.BlockSpec((tm, w), lambda i: (i, 0))
    seg_a = pl.BlockSpec((tm, d), lambda i: (jnp.minimum(i, tiles_a - 1), 0))
    seg_b = pl.BlockSpec((tm, d), lambda i: (jnp.maximum(i - tiles_a, 0), 0))
    const = lambda shape: pl.BlockSpec(shape, lambda i: (0,) * len(shape))
    split_x = len(xs) == 2
    return pl.pallas_call(
        functools.partial(_outproj_router_kernel, tiles_a=tiles_a, split_x=split_x),
        grid=(n // tm,),
        in_specs=[seg_a, seg_b] + ([seg_a, seg_b] if split_x else [row(d)])
        + [const((d, d)), const((1, d)), const((d, LANES)), const((1, LANES))],
        out_specs=[row(d), row(d), row(LANES), const((SUBLANES, LANES))],
        out_shape=[jax.ShapeDtypeStruct((n, d), F32), jax.ShapeDtypeStruct((n, d), F32),
                   jax.ShapeDtypeStruct((n, LANES), F32), jax.ShapeDtypeStruct((SUBLANES, LANES), F32)],
        scratch_shapes=[pltpu.VMEM((1, LANES), F32)],
        compiler_params=_params(("arbitrary",)),
        name="outproj_router",
    )(o_a, o_b, *xs, wout, gain, wr, br)


def _dispatch_kernel(pos_ref, lo_ref, hi_ref, h_ref, xs_hbm, buf, zrow, sem, zsem):
    i = pl.program_id(0)
    n = pl.num_programs(0)
    tm = h_ref.shape[0]
    slot = i % 2

    def tile_wait(s):
        for _ in range(TOP_K):
            pltpu.make_async_copy(buf.at[s], xs_hbm.at[pl.ds(0, tm)], sem.at[s]).wait()

    def pad_rows(fn, tile_fn):
        for e in range(N_EXPERTS):
            lax.fori_loop(lo_ref[e], hi_ref[e], fn, 0)
        lax.fori_loop(hi_ref[N_EXPERTS - 1] // tm, xs_hbm.shape[0] // tm, tile_fn, 0)

    def pad_start(r, carry):
        pltpu.make_async_copy(zrow.at[pl.ds(0, 1)], xs_hbm.at[pl.ds(r, 1)], zsem.at[0]).start()
        return carry

    def pad_wait(r, carry):
        pltpu.make_async_copy(zrow.at[pl.ds(0, 1)], xs_hbm.at[pl.ds(0, 1)], zsem.at[0]).wait()
        return carry

    def pad_tile_start(t, carry):
        pltpu.make_async_copy(zrow, xs_hbm.at[pl.ds(pl.multiple_of(t * tm, tm), tm)], zsem.at[0]).start()
        return carry

    def pad_tile_wait(t, carry):
        pltpu.make_async_copy(zrow, xs_hbm.at[pl.ds(0, tm)], zsem.at[0]).wait()
        return carry

    @pl.when(i == 0)
    def _():
        zrow[...] = jnp.zeros_like(zrow)
        pad_rows(pad_start, pad_tile_start)

    @pl.when(i >= 2)
    def _():
        tile_wait(slot)
    buf[slot] = h_ref[...]

    def body(r, carry):
        for k in range(TOP_K):
            row = pos_ref[TOP_K * (i * tm + r) + k]
            pltpu.make_async_copy(buf.at[slot, pl.ds(r, 1)], xs_hbm.at[pl.ds(row, 1)], sem.at[slot]).start(priority=k % 2)
        return carry
    lax.fori_loop(0, tm, body, 0, unroll=8)

    @pl.when(i == n - 1)
    def _():
        tile_wait(slot)

        @pl.when(n > 1)
        def _():
            tile_wait(1 - slot)
        pad_rows(pad_wait, pad_tile_wait)


def _dispatch(h, pos, pad_lo, pad_hi, n_rows):
    n, d = h.shape
    tm = TOKEN_TILE
    grid_spec = pltpu.PrefetchScalarGridSpec(
        num_scalar_prefetch=3,
        grid=(n // tm,),
        in_specs=[pl.BlockSpec((tm, d), lambda i, pos, lo, hi: (i, 0))],
        out_specs=pl.BlockSpec(memory_space=pl.ANY),
        scratch_shapes=[pltpu.VMEM((2, tm, d), F32), pltpu.VMEM((tm, d), F32),
                        pltpu.SemaphoreType.DMA((2,)), pltpu.SemaphoreType.DMA((1,))],
    )
    return pl.pallas_call(
        _dispatch_kernel,
        grid_spec=grid_spec,
        out_shape=jax.ShapeDtypeStruct((n_rows, d), F32),
        compiler_params=_params(("arbitrary",)),
        name="moe_dispatch",
    )(pos, pad_lo, pad_hi, h)


def _moe_kernel(te_ref, nact_ref, x_ref, w1_ref, w3_ref, w2_ref, y_ref, w1b, w3b, w2b):
    i = pl.program_id(0)

    @pl.when(i < nact_ref[0])
    def _():
        @pl.when(jnp.logical_or(i == 0, te_ref[i] != te_ref[jnp.maximum(i - 1, 0)]))
        def _():
            w1b[...] = w1_ref[0, 0].astype(BF16)
            w3b[...] = w3_ref[0, 0].astype(BF16)
            w2b[...] = w2_ref[0, 0].astype(BF16)

        xb = x_ref[...].astype(BF16)
        h1 = jnp.dot(xb, w1b[...], preferred_element_type=F32)
        h3 = jnp.dot(xb, w3b[...], preferred_element_type=F32)
        act = (h1 * _sigmoid(h1)) * h3
        y_ref[...] = jnp.dot(act.astype(BF16), w2b[...], preferred_element_type=F32)

    @pl.when(i >= nact_ref[0])
    def _():
        y_ref[...] = jnp.zeros_like(y_ref)


def _moe(xs, te, nact, w1, w3, w2, layer):
    p_rows, d = xs.shape
    f = w1.shape[3]
    tm = MOE_TILE
    last = lambda i, nact: jnp.maximum(jnp.minimum(i, nact[0] - 1), 0)
    grid_spec = pltpu.PrefetchScalarGridSpec(
        num_scalar_prefetch=2,
        grid=(p_rows // tm,),
        in_specs=[
            pl.BlockSpec((tm, d), lambda i, te, nact: (last(i, nact), 0)),
            pl.BlockSpec((1, 1, d, f), lambda i, te, nact: (layer, te[i], 0, 0)),
            pl.BlockSpec((1, 1, d, f), lambda i, te, nact: (layer, te[i], 0, 0)),
            pl.BlockSpec((1, 1, f, d), lambda i, te, nact: (layer, te[i], 0, 0)),
        ],
        out_specs=pl.BlockSpec((tm, d), lambda i, te, nact: (i, 0)),
        scratch_shapes=[pltpu.VMEM((d, f), BF16), pltpu.VMEM((d, f), BF16), pltpu.VMEM((f, d), BF16)],
    )
    return pl.pallas_call(
        _moe_kernel,
        grid_spec=grid_spec,
        out_shape=jax.ShapeDtypeStruct((p_rows, d), F32),
        compiler_params=_params(("arbitrary",)),
        name="moe_experts",
    )(te, nact, xs, w1, w3, w2)


def _moe_rows(n):
    return ((TOP_K * n + N_EXPERTS * (MOE_TILE - 1)) // MOE_TILE) * MOE_TILE


def _route_tables(info, counts_row, n):
    tm = MOE_TILE
    n_tiles = _moe_rows(n) // tm
    ids = jnp.arange(N_EXPERTS, dtype=I32)
    counts = counts_row[0, N_GROUPS:N_GROUPS + N_EXPERTS].astype(I32)
    padded = ((counts + tm - 1) // tm) * tm
    cum = jnp.cumsum(padded)
    off = cum - padded
    e = info[:, INFO_E:INFO_E + TOP_K].astype(I32)
    rank = info[:, INFO_R:INFO_R + TOP_K].astype(I32)
    pos = (jnp.sum(jnp.where(e[:, :, None] == ids, off, 0), axis=-1) + rank).reshape(TOP_K * n)
    tile_start = jnp.arange(n_tiles, dtype=I32) * tm
    nact = cum[-1] // tm
    te_raw = jnp.sum((cum[None, :] <= tile_start[:, None]).astype(I32), axis=1)
    te_last = jnp.sum((cum <= (nact - 1) * tm).astype(I32))
    te = jnp.where(tile_start < cum[-1], jnp.minimum(te_raw, N_EXPERTS - 1), te_last)
    return pos, off + counts, cum, te, nact.reshape(1)


def _pad_lanes(a, width=LANES):
    return jnp.pad(a, [(0, 0)] * (a.ndim - 1) + [(0, width - a.shape[-1])])


def _sample_group(nb):
    g = 8
    while nb % g:
        g //= 2
    return g


def kernel(x_prompt, x_sample, state_conv_ab, state_delta_ab, state_gla_ab, state_hgrn_c, mix_norm, ab_w_in, ab_conv_w, ab_a_log, ab_dt_bias, ab_dn_norm, ab_gk_w2, ab_gk_b, ab_gla_norm, ab_w_out, c_w_in, c_lower_bounds, c_norm, c_w_out, ffn_norm, moe_w_group, moe_b_group, moe_w_expert, moe_b_expert, moe_w1, moe_w3, moe_w2, final_norm):
    bp, tp, d = x_prompt.shape
    bs, ts, _ = x_sample.shape
    n_p, n_s = bp * tp, bs * ts
    n = n_p + n_s
    depth = mix_norm.shape[0]
    xs_res = (x_prompt.reshape(n_p, d), x_sample.reshape(n_s, d))
    sg = _sample_group(bs)

    info = pos = ys = None
    conv_p, delta_p, gla_p, hgrn_p = [], [], [], []
    conv_s, delta_s, gla_s, hgrn_s = [], [], [], []
    for layer in range(depth):
        j = layer // 2
        gain = mix_norm[layer].reshape(1, d)
        if layer % 2 == 0:
            u_w, a_w, b_w, z_w, gq_w, gk_w, gv_w, lr_w, gg_w = jnp.split(ab_w_in[j], _split_points(AB_SPLITS), axis=1)
            w_in = jnp.concatenate([u_w, z_w, gq_w, gk_w, gv_w, gg_w, _pad_lanes(jnp.concatenate([a_w, b_w, lr_w], axis=1))],
                                   axis=1).astype(BF16)
        else:
            w_in = c_w_in[j].astype(BF16)
        if ys is None:
            p_p, p_s = (_norm_proj(seg, gain, w_in) for seg in xs_res)
            off_s = 0
        else:
            x, p_p = _combine_norm(xs_res[0], info, pos, ys, gain, w_in)
            xs_res = (x,)
            p_s, off_s = p_p, n_p

        if layer % 2 == 0:
            prm = jnp.concatenate([_pad_lanes(ab_a_log[j].reshape(1, -1)), _pad_lanes(ab_dt_bias[j].reshape(1, -1)),
                                   ab_dn_norm[j].reshape(1, -1), ab_gla_norm[j].reshape(1, -1),
                                   jnp.zeros((4, LANES), F32)], axis=0)
            gkw = jnp.zeros((LANES, GLA_K), F32).at[MISC_LR:MISC_LR + GLA_RANK].set(ab_gk_w2[j])
            gkb = ab_gk_b[j].reshape(1, GLA_K)
            cw = ab_conv_w[j]
            o_p, c8, sd, sgl = _ab_mixer(p_p, n_p, 0, bp, tp, 1, cw, prm, gkw, gkb, None)
            conv_p.append(c8[:, SUBLANES - (CONV_W - 1):])
            delta_p.append(sd)
            gla_p.append(sgl)
            conv8 = jnp.pad(state_conv_ab[j], ((0, 0), (SUBLANES - (CONV_W - 1), 0), (0, 0)))
            o_s, c8, sd, sgl = _ab_mixer(p_s, n_s, off_s, bs, ts, sg, cw, prm, gkw, gkb,
                                         (conv8, state_delta_ab[j], state_gla_ab[j]))
            conv_s.append(c8[:, SUBLANES - (CONV_W - 1):])
            delta_s.append(sd)
            gla_s.append(sgl)
            w_out = ab_w_out[j]
        else:
            hg_gain = c_norm[j].reshape(1, -1)
            o_p, sh = _hgrn_mixer(p_p, 0, bp, tp, 1, c_lower_bounds, hg_gain, None, layer)
            hgrn_p.append(sh)
            o_s, sh = _hgrn_mixer(p_s, off_s, bs, ts, sg, c_lower_bounds, hg_gain, state_hgrn_c[j], layer)
            hgrn_s.append(sh)
            w_out = c_w_out[j]

        wr = _pad_lanes(jnp.concatenate([moe_w_group[layer], moe_w_expert[layer]], axis=1))
        br = _pad_lanes(jnp.concatenate([moe_b_group[layer], moe_b_expert[layer]]).reshape(1, -1))
        x, hn, info, counts = _outproj_router(o_p, o_s, xs_res, w_out.astype(BF16), ffn_norm[layer].reshape(1, d), wr, br)
        xs_res = (x,)
        pos, pad_lo, pad_hi, te, nact = _route_tables(info, counts, n)
        xsort = _dispatch(hn, pos, pad_lo, pad_hi, _moe_rows(n))
        ys = _moe(xsort, te, nact, moe_w1, moe_w3, moe_w2, layer)

    y_p, y_s = _combine_norm(xs_res[0], info, pos, ys, final_norm.reshape(1, d), None, n_a=n_p)
    y_prompt = y_p.reshape(bp, tp, d)
    y_sample = y_s.reshape(bs, ts, d)
    return (y_prompt, y_sample, jnp.stack(conv_p), jnp.stack(delta_p), jnp.stack(gla_p), jnp.stack(hgrn_p),
            jnp.stack(conv_s), jnp.stack(delta_s), jnp.stack(gla_s), jnp.stack(hgrn_s))


def _split_points(sizes):
    pts, acc = [], 0
    for s in sizes[:-1]:
        acc += s
        pts.append(acc)
    return pts
```

```python
import functools

import jax
import jax.numpy as jnp
from jax import lax
from jax.experimental import pallas as pl
from jax.experimental.pallas import tpu as pltpu

F32 = jnp.float32
BF16 = jnp.bfloat16
I32 = jnp.int32

D_MODEL = 1024
EPS = 1e-6
CHUNK = 64
CONV_W = 4
DN_HEADS, DN_DK, DN_DV = 4, 128, 128
DN_QK = DN_HEADS * DN_DK
DN_V = DN_HEADS * DN_DV
CONV_CH = 2 * DN_QK + DN_V
GLA_HEADS, GLA_DK, GLA_DV = 4, 64, 128
GLA_K = GLA_HEADS * GLA_DK
GLA_V = GLA_HEADS * GLA_DV
GLA_RANK = 16
GLA_NORMALIZER = 16.0
GLA_PACK = 128 // GLA_DK
AB_SPLITS = (CONV_CH, DN_HEADS, DN_HEADS, DN_V, GLA_K, GLA_K, GLA_V, GLA_RANK, GLA_V)
HG_HEADS, HG_DK, HG_DV = 8, 128, 128
HG_F = HG_HEADS * HG_DK
N_GROUPS, EXPERTS_PER_GROUP = 4, 8
N_EXPERTS = N_GROUPS * EXPERTS_PER_GROUP
TOP_K = 2
D_EXPERT = 512

LANES = 128
SUBLANES = 8
VMEM_LIMIT_BYTES = 56 * 1024 * 1024

AB_U, AB_Z, AB_GQ, AB_GK, AB_GV, AB_GG, AB_MISC = 0, 1536, 2048, 2304, 2560, 3072, 3584
AB_COLS = AB_MISC + LANES
MISC_A, MISC_B, MISC_LR = 0, DN_HEADS, 2 * DN_HEADS

INFO_E, INFO_W, INFO_R = 0, TOP_K, 2 * TOP_K

TOKEN_TILE = 256
MOE_TILE = 256
NEG = -1e30
LOG2E = 1.4426950408889634
GLA_SUB_BLOCK = SUBLANES


def _params(sem):
    return pltpu.CompilerParams(dimension_semantics=sem, vmem_limit_bytes=VMEM_LIMIT_BYTES)


def _mm(a, b):
    return jnp.dot(a.astype(BF16), b.astype(BF16), preferred_element_type=F32)


def _mm_nt(a, b):
    return lax.dot_general(a.astype(BF16), b.astype(BF16), (((1,), (1,)), ((), ())), preferred_element_type=F32)


def _mm_tn(a, b):
    return lax.dot_general(a.astype(BF16), b.astype(BF16), (((0,), (0,)), ((), ())), preferred_element_type=F32)


def _mm_hi(a, b):
    return jnp.dot(a, b, preferred_element_type=F32, precision=lax.Precision.HIGHEST)


def _sigmoid(x):
    return 1.0 / (1.0 + jnp.exp(-x))


def _softplus(x):
    return jnp.maximum(x, 0.0) + jnp.log(1.0 + jnp.exp(-jnp.abs(x)))


def _rms(x, gain):
    return x * lax.rsqrt(jnp.mean(x * x, axis=-1, keepdims=True) + EPS) * gain


def _gated_rms(o, gate, gain):
    return _rms(o, gain) * (gate * _sigmoid(gate))


def _row_to_col(row):
    n = row.shape[1]
    r = lax.broadcasted_iota(I32, (n, n), 0)
    c = lax.broadcasted_iota(I32, (n, n), 1)
    return jnp.sum(jnp.where(r == c, jnp.broadcast_to(row, (n, n)), 0.0), axis=1, keepdims=True)


def _split2(x):
    hi = x.astype(BF16)
    return hi, (x - hi.astype(F32)).astype(BF16)


def _split3(x):
    hi = x.astype(BF16)
    r = x - hi.astype(F32)
    mid = r.astype(BF16)
    return hi, mid, (r - mid.astype(F32)).astype(BF16)


def _mm_cum(tri, x):
    hi, mid, lo = _split3(x)
    dot = lambda p: jnp.dot(tri, p, preferred_element_type=F32)
    return dot(hi) + (dot(mid) + dot(lo))


def _mm_split(a, b):
    ah, al = _split2(a)
    bh, bl = _split2(b)
    dot = lambda x, y: jnp.dot(x, y, preferred_element_type=F32)
    return dot(ah, bh) + (dot(ah, bl) + dot(al, bh))


def _delta_chunks(qs_, ks_, vs_, gcs_, grs_, betas_, ss_):
    heads = range(len(qs_))
    c = qs_[0].shape[0]
    ti = lax.broadcasted_iota(I32, (c, c), 0)
    tj = lax.broadcasted_iota(I32, (c, c), 1)
    eye = jnp.where(ti == tj, 1.0, 0.0)
    decay = [jnp.exp(jnp.where(ti >= tj, gcs_[h] - grs_[h], NEG)) for h in heads]
    qs = [qs_[h] * (DN_DK ** -0.5) for h in heads]
    kk = [_mm_nt(ks_[h], ks_[h]) for h in heads]
    nm = [jnp.where(ti > tj, betas_[h] * kk[h] * decay[h], 0.0) for h in heads]
    t = [eye - nm[h] for h in heads]
    p = nm
    step = 2
    while step < c:
        p = [_mm(p[h], p[h]) for h in heads]
        t = [t[h] + _mm(t[h], p[h]) for h in heads]
        step *= 2
    eg = [jnp.exp(gcs_[h]) for h in heads]
    rhs = [jnp.concatenate([betas_[h] * vs_[h], (betas_[h] * eg[h]) * ks_[h]], axis=1) for h in heads]
    sol = [_mm(t[h], rhs[h]) for h in heads]
    resid = [rhs[h] - (sol[h] + _mm_split(nm[h], sol[h])) for h in heads]
    sol = [sol[h] + _mm(t[h], resid[h]) for h in heads]
    qk = [_mm_nt(qs[h], ks_[h]) * decay[h] for h in heads]
    u = [sol[h][:, :DN_DV] - _mm(sol[h][:, DN_DV:], ss_[h]) for h in heads]
    o = [_mm(qs[h] * eg[h], ss_[h]) + _mm(qk[h], u[h]) for h in heads]
    gl = [gcs_[h][c - 1:c] for h in heads]
    s_new = [ss_[h] * jnp.exp(gl[h]) + _mm_tn(ks_[h] * jnp.exp(gl[h] - gcs_[h]), u[h]) for h in heads]
    return o, s_new


def _gla_chunks(qs, ks, gs, vss, ss, m):
    probs = range(len(qs))
    c = qs[0].shape[0]
    nh = len(vss[0])
    heads = range(nh)
    dk = LANES // nh
    dlane = lax.broadcasted_iota(I32, (1, LANES), 1)
    if nh == 1:
        mask = lambda x, h: x
    else:
        hm = [jnp.where(jnp.logical_and(dlane >= h * dk, dlane < (h + 1) * dk), 1.0, 0.0) for h in heads]
        mask = lambda x, h: x * hm[h]
    gs = [gs[i] * LOG2E for i in probs]
    qeg = [qs[i] * jnp.exp2(gs[i]) for i in probs]
    o = [[_mm(mask(qeg[i], h), ss[i]) for h in heads] for i in probs]
    lane = lax.broadcasted_iota(I32, (m, c), 1)
    sub = lax.broadcasted_iota(I32, (m, c), 0)
    blocks = [[[] for _ in heads] for _ in probs]
    for blk in range(c // m):
        r0 = blk * m
        qb = [qs[i][r0:r0 + m] for i in probs]
        gb = [gs[i][r0:r0 + m] for i in probs]
        kb = [ks[i][r0:r0 + m] for i in probs]
        if blk > 0:
            base = [gs[i][r0 - 1:r0] for i in probs]
            kt = [jnp.concatenate([ks[i][:r0] * jnp.exp2(base[i] - gs[i][:r0]), jnp.zeros((c - r0, LANES), F32)], axis=0)
                  for i in probs]
            qt = [qb[i] * jnp.exp2(gb[i] - base[i]) for i in probs]
            att = [[_mm_nt(mask(qt[i], h), kt[i]) for h in heads] for i in probs]
        else:
            att = [[jnp.zeros((m, c), F32) for _ in heads] for _ in probs]
        for j in range(m):
            keep = jnp.logical_and(lane == r0 + j, sub >= j)
            for i in probs:
                prod = qb[i] * (kb[i][j:j + 1] * jnp.exp2(gb[i] - gb[i][j:j + 1]))
                for h in heads:
                    att[i][h] = jnp.where(keep, jnp.sum(mask(prod, h), axis=1, keepdims=True), att[i][h])
        for i in probs:
            for h in heads:
                blocks[i][h].append(att[i][h])
    s_new = []
    for i in probs:
        gl = gs[i][c - 1:c]
        kg = ks[i] * jnp.exp2(gl - gs[i])
        s_i = ss[i] * _row_to_col(jnp.exp2(gl))
        for h in heads:
            att = blocks[i][h][0] if len(blocks[i][h]) == 1 else jnp.concatenate(blocks[i][h], axis=0)
            o[i][h] = o[i][h] + _mm(att, vss[i][h])
            s_i = s_i + _mm_tn(mask(kg, h), vss[i][h])
        s_new.append(s_i)
    return o, s_new


def _tri_incl(c):
    r = lax.broadcasted_iota(I32, (c, c), 0)
    cc = lax.broadcasted_iota(I32, (c, c), 1)
    return jnp.where(r >= cc, 1.0, 0.0).astype(BF16)


def _norm_proj_kernel(x_ref, gain_ref, w_ref, p_ref):
    h = _rms(x_ref[...], gain_ref[...])
    p_ref[...] = jnp.dot(h.astype(BF16), w_ref[...], preferred_element_type=F32)


def _norm_proj(x, gain, w):
    n, d = x.shape
    m = w.shape[1]
    tm = TOKEN_TILE
    return pl.pallas_call(
        _norm_proj_kernel,
        grid=(n // tm,),
        in_specs=[pl.BlockSpec((tm, d), lambda i: (i, 0)), pl.BlockSpec((1, d), lambda i: (0, 0)),
                  pl.BlockSpec((d, m), lambda i: (0, 0))],
        out_specs=pl.BlockSpec((tm, m), lambda i: (i, 0)),
        out_shape=jax.ShapeDtypeStruct((n, m), F32),
        compiler_params=_params(("parallel",)),
        name="norm_proj",
    )(x, gain, w)


def _combine_norm_kernel(pos_ref, *refs, project, tiles_a):
    if project:
        x_ref, info_ref, gain_ref, w_ref, ys_hbm, xs_ref, out_ref, ybuf, sem = refs
    else:
        x_ref, info_ref, gain_ref, ys_hbm, outa_ref, outb_ref, ybuf, sem = refs
    i = pl.program_id(0)
    n = pl.num_programs(0)
    tm = x_ref.shape[0]
    slot = i % 2

    def gather_start(tile, s):
        def body(r, carry):
            for k in range(TOP_K):
                row = pos_ref[TOP_K * (tile * tm + r) + k]
                pltpu.make_async_copy(ys_hbm.at[pl.ds(row, 1)], ybuf.at[s, k, pl.ds(r, 1)], sem.at[s]).start(priority=k % 2)
            return carry
        lax.fori_loop(0, tm, body, 0, unroll=8)

    @pl.when(i == 0)
    def _():
        gather_start(0, 0)

    @pl.when(i + 1 < n)
    def _():
        gather_start(i + 1, 1 - slot)

    for k in range(TOP_K):
        pltpu.make_async_copy(ys_hbm.at[pl.ds(0, tm)], ybuf.at[slot, k], sem.at[slot]).wait()
    info = info_ref[...]
    x = x_ref[...] + (info[:, INFO_W:INFO_W + 1] * ybuf[slot, 0] + info[:, INFO_W + 1:INFO_W + 2] * ybuf[slot, 1])
    h = _rms(x, gain_ref[...])
    if project:
        xs_ref[...] = x
        out_ref[...] = jnp.dot(h.astype(BF16), w_ref[...], preferred_element_type=F32)
    else:
        @pl.when(i < tiles_a)
        def _():
            outa_ref[...] = h

        @pl.when(i >= tiles_a)
        def _():
            outb_ref[...] = h


def _combine_norm(x, info, pos, ys, gain, w, n_a=None):
    n, d = x.shape
    tm = TOKEN_TILE
    project = w is not None
    tiles_a = None if project else n_a // tm
    row = lambda width: pl.BlockSpec((tm, width), lambda i, pos: (i, 0))
    in_specs = [row(d), row(LANES), pl.BlockSpec((1, d), lambda i, pos: (0, 0))]
    args = [x, info, gain]
    if project:
        m = w.shape[1]
        in_specs.append(pl.BlockSpec((d, m), lambda i, pos: (0, 0)))
        args.append(w)
        out_specs = [row(d), row(m)]
        out_shape = [jax.ShapeDtypeStruct((n, d), F32), jax.ShapeDtypeStruct((n, m), F32)]
    else:
        out_specs = [pl.BlockSpec((tm, d), lambda i, pos: (jnp.minimum(i, tiles_a - 1), 0)),
                     pl.BlockSpec((tm, d), lambda i, pos: (jnp.maximum(i - tiles_a, 0), 0))]
        out_shape = [jax.ShapeDtypeStruct((n_a, d), F32), jax.ShapeDtypeStruct((n - n_a, d), F32)]
    in_specs.append(pl.BlockSpec(memory_space=pl.ANY))
    args.append(ys)
    grid_spec = pltpu.PrefetchScalarGridSpec(
        num_scalar_prefetch=1, grid=(n // tm,), in_specs=in_specs, out_specs=out_specs,
        scratch_shapes=[pltpu.VMEM((2, TOP_K, tm, d), F32), pltpu.SemaphoreType.DMA((2,))])
    return pl.pallas_call(
        functools.partial(_combine_norm_kernel, project=project, tiles_a=tiles_a),
        grid_spec=grid_spec,
        out_shape=out_shape,
        compiler_params=_params(("arbitrary",)),
        name="combine_norm",
    )(pos, *args)


def _ab_mixer_kernel(*refs, c, nseq, zero_init):
    p_refs, refs = refs[:nseq], refs[nseq:]
    if zero_init:
        cw_ref, prm_ref, gkw_ref, gkb_ref, o_ref, conv_ref, sdn_ref, sgla_ref = refs
    else:
        (cw_ref, prm_ref, gkw_ref, gkb_ref, conv_in, sdn_in, sgla_in,
         o_ref, conv_ref, sdn_ref, sgla_ref) = refs

    @pl.when(pl.program_id(1) == 0)
    def _():
        if zero_init:
            conv_ref[...] = jnp.zeros_like(conv_ref)
            sdn_ref[...] = jnp.zeros_like(sdn_ref)
            sgla_ref[...] = jnp.zeros_like(sgla_ref)
        else:
            conv_ref[...] = conv_in[...]
            sdn_ref[...] = sdn_in[...]
            sgla_ref[...] = sgla_in[...]

    m = min(GLA_SUB_BLOCK, c)
    tri = _tri_incl(c)
    cw = cw_ref[...]
    a_log, dt_bias = prm_ref[0:1], prm_ref[1:2]
    dn_gain, gla_gain = prm_ref[2:3], prm_ref[3:4]
    n_pairs = GLA_HEADS // GLA_PACK
    conv_old = [conv_ref[b] for b in range(nseq)]
    sdn_old = [[sdn_ref[b, h] for h in range(DN_HEADS)] for b in range(nseq)]
    sgla_old = [[sgla_ref[b, pr] for pr in range(n_pairs)] for b in range(nseq)]
    dn = {key: [] for key in ("q", "k", "v", "gc", "gr", "beta", "s")}
    miscs = []
    for b in range(nseq):
        p_ref = p_refs[b]
        u = p_ref[:, AB_U:AB_U + CONV_CH]
        ucat = jnp.concatenate([conv_old[b], u], axis=0)
        acc = u * cw[CONV_W - 1:CONV_W]
        for j in range(1, CONV_W):
            acc = acc + pltpu.roll(ucat, j, 0)[SUBLANES:SUBLANES + c] * cw[CONV_W - 1 - j:CONV_W - j]
        conv_ref[b] = ucat[c:c + SUBLANES]
        qkv = acc * _sigmoid(acc)

        misc = p_ref[:, AB_MISC:AB_MISC + LANES]
        miscs.append(misc)
        g_all = -jnp.exp(a_log) * _softplus(misc + dt_bias)
        beta_all = _sigmoid(misc)
        gcum = _mm_cum(tri, g_all)
        gcum_t = jnp.concatenate([gcum, jnp.zeros((LANES - c, LANES), F32)], axis=0).T
        for h in range(DN_HEADS):
            lo = h * DN_DK
            q = qkv[:, lo:lo + DN_DK]
            k = qkv[:, DN_QK + lo:DN_QK + lo + DN_DK]
            dn["q"].append(q * lax.rsqrt(jnp.sum(q * q, axis=-1, keepdims=True) + 1e-6))
            dn["k"].append(k * lax.rsqrt(jnp.sum(k * k, axis=-1, keepdims=True) + 1e-6))
            dn["v"].append(qkv[:, 2 * DN_QK + h * DN_DV:2 * DN_QK + (h + 1) * DN_DV])
            dn["gc"].append(gcum[:, MISC_A + h:MISC_A + h + 1])
            dn["gr"].append(gcum_t[MISC_A + h:MISC_A + h + 1, :c])
            dn["beta"].append(beta_all[:, MISC_B + h:MISC_B + h + 1])
            dn["s"].append(sdn_old[b][h])
    os_, ss_ = _delta_chunks(dn["q"], dn["k"], dn["v"], dn["gc"], dn["gr"], dn["beta"], dn["s"])
    for b in range(nseq):
        for h in range(DN_HEADS):
            i = b * DN_HEADS + h
            sdn_ref[b, h] = ss_[i]
            z = p_refs[b][:, AB_Z + h * DN_DV:AB_Z + (h + 1) * DN_DV]
            o_ref[b, :, h * DN_DV:(h + 1) * DN_DV] = _gated_rms(os_[i], z, dn_gain).astype(o_ref.dtype)

    gl = {key: [] for key in ("q", "k", "g", "v", "s")}
    for b in range(nseq):
        p_ref = p_refs[b]
        logits = _mm_split(miscs[b], gkw_ref[...]) + gkb_ref[...]
        logf = -_softplus(-logits) * (1.0 / GLA_NORMALIZER)
        gall = _mm_cum(tri, logf)
        for pair in range(n_pairs):
            lo = pair * LANES
            gl["q"].append(p_ref[:, AB_GQ + lo:AB_GQ + lo + LANES] * (GLA_DK ** -0.5))
            gl["k"].append(p_ref[:, AB_GK + lo:AB_GK + lo + LANES])
            gl["g"].append(gall[:, lo:lo + LANES])
            gl["v"].append([p_ref[:, AB_GV + h * GLA_DV:AB_GV + (h + 1) * GLA_DV]
                            for h in range(pair * GLA_PACK, (pair + 1) * GLA_PACK)])
            gl["s"].append(sgla_old[b][pair])
    os_, ss_ = _gla_chunks(gl["q"], gl["k"], gl["g"], gl["v"], gl["s"], m)
    for b in range(nseq):
        for pair in range(n_pairs):
            i = b * n_pairs + pair
            sgla_ref[b, pair] = ss_[i]
            for hh in range(GLA_PACK):
                h = pair * GLA_PACK + hh
                gate = p_refs[b][:, AB_GG + h * GLA_DV:AB_GG + (h + 1) * GLA_DV]
                o_ref[b, :, DN_V + h * GLA_DV:DN_V + (h + 1) * GLA_DV] = \
                    _gated_rms(os_[i][hh], gate, gla_gain).astype(o_ref.dtype)


def _seq_specs(cols, row0, c, nchunk, nseq):
    blk0 = row0 // c
    return [pl.BlockSpec((c, cols), lambda g, i, s=s: (blk0 + (g * nseq + s) * nchunk + i, 0)) for s in range(nseq)]


def _ab_mixer(p, row0, nb, t, nseq, cw, prm, gkw, gkb, states):
    c = min(CHUNK, t)
    nchunk = t // c
    n_groups = nb // nseq
    zero_init = states is None
    st_map3 = lambda g, i: (g, 0, 0)
    st_map4 = lambda g, i: (g, 0, 0, 0)
    const = lambda shape: pl.BlockSpec(shape, lambda g, i: (0,) * len(shape))
    in_specs = _seq_specs(AB_COLS, row0, c, nchunk, nseq) + [const(cw.shape), const(prm.shape), const(gkw.shape),
                                                             const(gkb.shape)]
    args = [p] * nseq + [cw, prm, gkw, gkb]
    gla_packed = (GLA_HEADS // GLA_PACK, GLA_PACK * GLA_DK, GLA_DV)
    st_specs = [pl.BlockSpec((nseq, SUBLANES, CONV_CH), st_map3),
                pl.BlockSpec((nseq, DN_HEADS, DN_DK, DN_DV), st_map4),
                pl.BlockSpec((nseq,) + gla_packed, st_map4)]
    if not zero_init:
        conv8, sdn, sgla = states
        in_specs += st_specs
        args += [conv8, sdn, sgla.reshape((nb,) + gla_packed)]
    out_shape = [jax.ShapeDtypeStruct((nb, t, D_MODEL), BF16),
                 jax.ShapeDtypeStruct((nb, SUBLANES, CONV_CH), F32),
                 jax.ShapeDtypeStruct((nb, DN_HEADS, DN_DK, DN_DV), F32),
                 jax.ShapeDtypeStruct((nb,) + gla_packed, F32)]
    out_specs = [pl.BlockSpec((nseq, c, D_MODEL), lambda g, i: (g, i, 0))] + st_specs
    o, conv_out, sdn_out, sgla_out = pl.pallas_call(
        functools.partial(_ab_mixer_kernel, c=c, nseq=nseq, zero_init=zero_init),
        grid=(n_groups, nchunk),
        in_specs=in_specs,
        out_specs=out_specs,
        out_shape=out_shape,
        compiler_params=_params(("parallel", "arbitrary")),
        name="ab_mixer",
    )(*args)
    return o.reshape(nb * t, D_MODEL), conv_out, sdn_out, sgla_out.reshape(nb, GLA_HEADS, GLA_DK, GLA_DV)


def _hgrn_mixer_kernel(*refs, c, nseq, zero_init, layer):
    p_refs, refs = refs[:nseq], refs[nseq:]
    if zero_init:
        lb_ref, gain_ref, o_ref, s_ref = refs
    else:
        lb_ref, gain_ref, s_in, o_ref, s_ref = refs

    @pl.when(pl.program_id(1) == 0)
    def _():
        if zero_init:
            s_ref[...] = jnp.zeros_like(s_ref)
        else:
            s_ref[...] = s_in[...]

    raw = lb_ref[...]
    e = jnp.exp(raw - jnp.max(raw, axis=0, keepdims=True))
    sm = e / jnp.sum(e, axis=0, keepdims=True)
    cum = sm[0:1]
    for l in range(1, layer + 1):
        cum = cum + sm[l:l + 1]
    lb = cum - sm[0:1]

    m = min(GLA_SUB_BLOCK, c)
    tri = _tri_incl(c)
    gain = gain_ref[...]
    pr = {key: [] for key in ("q", "k", "g", "v", "s")}
    for b in range(nseq):
        p_ref = p_refs[b]
        fr = p_ref[:, HG_F:2 * HG_F]
        logf = jnp.log(lb + (1.0 - lb) * _sigmoid(fr))
        kall = (1.0 - lb) * _sigmoid(-fr)
        gall = _mm_cum(tri, logf)
        for h in range(HG_HEADS):
            lo = h * HG_DK
            pr["q"].append(p_ref[:, lo:lo + HG_DK] * (HG_DK ** -0.5))
            pr["k"].append(kall[:, lo:lo + HG_DK])
            pr["g"].append(gall[:, lo:lo + HG_DK])
            pr["v"].append([p_ref[:, 2 * HG_F + h * HG_DV:2 * HG_F + (h + 1) * HG_DV]])
            pr["s"].append(s_ref[b, h])
    os_, ss_ = _gla_chunks(pr["q"], pr["k"], pr["g"], pr["v"], pr["s"], m)
    for b in range(nseq):
        for h in range(HG_HEADS):
            i = b * HG_HEADS + h
            s_ref[b, h] = ss_[i]
            gate = p_refs[b][:, 2 * HG_F + D_MODEL + h * HG_DV:2 * HG_F + D_MODEL + (h + 1) * HG_DV]
            o_ref[b, :, h * HG_DV:(h + 1) * HG_DV] = _gated_rms(os_[i][0], gate, gain).astype(o_ref.dtype)


def _hgrn_mixer(p, row0, nb, t, nseq, lb, gain, state, layer):
    c = min(CHUNK, t)
    nchunk = t // c
    zero_init = state is None
    const = lambda shape: pl.BlockSpec(shape, lambda g, i: (0,) * len(shape))
    st_spec = pl.BlockSpec((nseq, HG_HEADS, HG_DK, HG_DV), lambda g, i: (g, 0, 0, 0))
    in_specs = _seq_specs(p.shape[1], row0, c, nchunk, nseq) + [const(lb.shape), const(gain.shape)]
    args = [p] * nseq + [lb, gain]
    if not zero_init:
        in_specs.append(st_spec)
        args.append(state)
    o, s_out = pl.pallas_call(
        functools.partial(_hgrn_mixer_kernel, c=c, nseq=nseq, zero_init=zero_init, layer=layer),
        grid=(nb // nseq, nchunk),
        in_specs=in_specs,
        out_specs=[pl.BlockSpec((nseq, c, D_MODEL), lambda g, i: (g, i, 0)), st_spec],
        out_shape=[jax.ShapeDtypeStruct((nb, t, D_MODEL), BF16),
                   jax.ShapeDtypeStruct((nb, HG_HEADS, HG_DK, HG_DV), F32)],
        compiler_params=_params(("parallel", "arbitrary")),
        name="hgrn_mixer",
    )(*args)
    return o.reshape(nb * t, D_MODEL), s_out


def _outproj_router_kernel(*refs, tiles_a, split_x):
    if split_x:
        oa_ref, ob_ref, xa_ref, xb_ref = refs[:4]
        refs = refs[4:]
    else:
        oa_ref, ob_ref, x_ref = refs[:3]
        refs = refs[3:]
    wout_ref, gain_ref, wr_ref, br_ref, x1_ref, h_ref, info_ref, cnt_ref, run_ref = refs
    first = pl.program_id(0) < tiles_a

    @pl.when(pl.program_id(0) == 0)
    def _():
        run_ref[...] = jnp.zeros_like(run_ref)

    o = jnp.where(first, oa_ref[...], ob_ref[...])
    x = jnp.where(first, xa_ref[...], xb_ref[...]) if split_x else x_ref[...]
    x1 = x + jnp.dot(o, wout_ref[...], preferred_element_type=F32)
    x1_ref[...] = x1
    h = _rms(x1, gain_ref[...])
    h_ref[...] = h
    logit = _mm_hi(h, wr_ref[...]) + br_ref[...]
    lane = lax.broadcasted_iota(I32, logit.shape, 1)
    far = jnp.int32(LANES)
    lg = jnp.where(lane < N_GROUPS, logit, NEG)
    mg = jnp.max(lg, axis=-1, keepdims=True)
    p_g = 1.0 / jnp.sum(jnp.exp(lg - mg), axis=-1, keepdims=True)
    g_top = jnp.min(jnp.where(lg == mg, lane, far), axis=-1, keepdims=True)
    lo = N_GROUPS + g_top * EXPERTS_PER_GROUP
    sel = jnp.logical_and(lane >= lo, lane < lo + EXPERTS_PER_GROUP)
    le = jnp.where(sel, logit, NEG)
    ee = jnp.exp(le - jnp.max(le, axis=-1, keepdims=True))
    pe = jnp.where(sel, ee / jnp.sum(ee, axis=-1, keepdims=True), -1.0)
    p1 = jnp.max(pe, axis=-1, keepdims=True)
    i1 = jnp.min(jnp.where(pe == p1, lane, far), axis=-1, keepdims=True)
    pe2 = jnp.where(lane == i1, -1.0, pe)
    p2 = jnp.max(pe2, axis=-1, keepdims=True)
    i2 = jnp.min(jnp.where(pe2 == p2, lane, far), axis=-1, keepdims=True)
    denom = p1 + p2
    w1 = p_g * (p1 / denom)
    w2 = p_g * (p2 / denom)
    tm = logit.shape[0]
    oh1 = jnp.where(lane == i1, 1.0, 0.0)
    oh2 = jnp.where(lane == i2, 1.0, 0.0)
    both = oh1 + oh2
    tr = lax.broadcasted_iota(I32, (tm, tm), 0)
    tc = lax.broadcasted_iota(I32, (tm, tm), 1)
    before = run_ref[...] + _mm(jnp.where(tr > tc, 1.0, 0.0), both)
    r1 = jnp.sum(oh1 * before, axis=-1, keepdims=True)
    r2 = jnp.sum(oh2 * before, axis=-1, keepdims=True)
    run = run_ref[...] + jnp.sum(both, axis=0, keepdims=True)
    run_ref[...] = run
    cnt_ref[...] = jnp.broadcast_to(run, cnt_ref.shape)
    vals = ((i1 - N_GROUPS).astype(F32), (i2 - N_GROUPS).astype(F32), w1, w2, r1, r2)
    info = jnp.zeros_like(logit)
    for idx, val in enumerate(vals):
        info = jnp.where(lane == idx, val, info)
    info_ref[...] = info


def _outproj_router(o_a, o_b, xs, wout, gain, wr, br):
    n_a, d = o_a.shape
    n = n_a + o_b.shape[0]
    tm = TOKEN_TILE
    tiles_a = n_a // tm
    row = lambda w: pl.BlockSpec((tm, w), lambda i: (i, 0))
    seg_a = pl.BlockSpec((tm, d), lambda i: (jnp.minimum(i, tiles_a - 1), 0))
    seg_b = pl.BlockSpec((tm, d), lambda i: (jnp.maximum(i - tiles_a, 0), 0))
    const = lambda shape: pl.BlockSpec(shape, lambda i: (0,) * len(shape))
    split_x = len(xs) == 2
    return pl.pallas_call(
        functools.partial(_outproj_router_kernel, tiles_a=tiles_a, split_x=split_x),
        grid=(n // tm,),
        in_specs=[seg_a, seg_b] + ([seg_a, seg_b] if split_x else [row(d)])
        + [const((d, d)), const((1, d)), const((d, LANES)), const((1, LANES))],
        out_specs=[row(d), row(d), row(LANES), const((SUBLANES, LANES))],
        out_shape=[jax.ShapeDtypeStruct((n, d), F32), jax.ShapeDtypeStruct((n, d), F32),
                   jax.ShapeDtypeStruct((n, LANES), F32), jax.ShapeDtypeStruct((SUBLANES, LANES), F32)],
        scratch_shapes=[pltpu.VMEM((1, LANES), F32)],
        compiler_params=_params(("arbitrary",)),
        name="outproj_router",
    )(o_a, o_b, *xs, wout, gain, wr, br)


def _dispatch_kernel(pos_ref, lo_ref, hi_ref, h_ref, xs_hbm, buf, zrow, sem, zsem):
    i = pl.program_id(0)
    n = pl.num_programs(0)
    tm = h_ref.shape[0]
    slot = i % 2

    def tile_wait(s):
        for _ in range(TOP_K):
            pltpu.make_async_copy(buf.at[s], xs_hbm.at[pl.ds(0, tm)], sem.at[s]).wait()

    def pad_rows(fn, tile_fn):
        for e in range(N_EXPERTS):
            lax.fori_loop(lo_ref[e], hi_ref[e], fn, 0)
        lax.fori_loop(hi_ref[N_EXPERTS - 1] // tm, xs_hbm.shape[0] // tm, tile_fn, 0)

    def pad_start(r, carry):
        pltpu.make_async_copy(zrow.at[pl.ds(0, 1)], xs_hbm.at[pl.ds(r, 1)], zsem.at[0]).start()
        return carry

    def pad_wait(r, carry):
        pltpu.make_async_copy(zrow.at[pl.ds(0, 1)], xs_hbm.at[pl.ds(0, 1)], zsem.at[0]).wait()
        return carry

    def pad_tile_start(t, carry):
        pltpu.make_async_copy(zrow, xs_hbm.at[pl.ds(pl.multiple_of(t * tm, tm), tm)], zsem.at[0]).start()
        return carry

    def pad_tile_wait(t, carry):
        pltpu.make_async_copy(zrow, xs_hbm.at[pl.ds(0, tm)], zsem.at[0]).wait()
        return carry

    @pl.when(i == 0)
    def _():
        zrow[...] = jnp.zeros_like(zrow)
        pad_rows(pad_start, pad_tile_start)

    @pl.when(i >= 2)
    def _():
        tile_wait(slot)
    buf[slot] = h_ref[...]

    def body(r, carry):
        for k in range(TOP_K):
            row = pos_ref[TOP_K * (i * tm + r) + k]
            pltpu.make_async_copy(buf.at[slot, pl.ds(r, 1)], xs_hbm.at[pl.ds(row, 1)], sem.at[slot]).start(priority=k % 2)
        return carry
    lax.fori_loop(0, tm, body, 0, unroll=8)

    @pl.when(i == n - 1)
    def _():
        tile_wait(slot)

        @pl.when(n > 1)
        def _():
            tile_wait(1 - slot)
        pad_rows(pad_wait, pad_tile_wait)


def _dispatch(h, pos, pad_lo, pad_hi, n_rows):
    n, d = h.shape
    tm = TOKEN_TILE
    grid_spec = pltpu.PrefetchScalarGridSpec(
        num_scalar_prefetch=3,
        grid=(n // tm,),
        in_specs=[pl.BlockSpec((tm, d), lambda i, pos, lo, hi: (i, 0))],
        out_specs=pl.BlockSpec(memory_space=pl.ANY),
        scratch_shapes=[pltpu.VMEM((2, tm, d), F32), pltpu.VMEM((tm, d), F32),
                        pltpu.SemaphoreType.DMA((2,)), pltpu.SemaphoreType.DMA((1,))],
    )
    return pl.pallas_call(
        _dispatch_kernel,
        grid_spec=grid_spec,
        out_shape=jax.ShapeDtypeStruct((n_rows, d), F32),
        compiler_params=_params(("arbitrary",)),
        name="moe_dispatch",
    )(pos, pad_lo, pad_hi, h)


def _moe_kernel(te_ref, nact_ref, x_ref, w1_ref, w3_ref, w2_ref, y_ref, w1b, w3b, w2b):
    i = pl.program_id(0)

    @pl.when(i < nact_ref[0])
    def _():
        @pl.when(jnp.logical_or(i == 0, te_ref[i] != te_ref[jnp.maximum(i - 1, 0)]))
        def _():
            w1b[...] = w1_ref[0, 0].astype(BF16)
            w3b[...] = w3_ref[0, 0].astype(BF16)
            w2b[...] = w2_ref[0, 0].astype(BF16)

        xb = x_ref[...].astype(BF16)
        h1 = jnp.dot(xb, w1b[...], preferred_element_type=F32)
        h3 = jnp.dot(xb, w3b[...], preferred_element_type=F32)
        act = (h1 * _sigmoid(h1)) * h3
        y_ref[...] = jnp.dot(act.astype(BF16), w2b[...], preferred_element_type=F32)

    @pl.when(i >= nact_ref[0])
    def _():
        y_ref[...] = jnp.zeros_like(y_ref)


def _moe(xs, te, nact, w1, w3, w2, layer):
    p_rows, d = xs.shape
    f = w1.shape[3]
    tm = MOE_TILE
    last = lambda i, nact: jnp.maximum(jnp.minimum(i, nact[0] - 1), 0)
    grid_spec = pltpu.PrefetchScalarGridSpec(
        num_scalar_prefetch=2,
        grid=(p_rows // tm,),
        in_specs=[
            pl.BlockSpec((tm, d), lambda i, te, nact: (last(i, nact), 0)),
            pl.BlockSpec((1, 1, d, f), lambda i, te, nact: (layer, te[i], 0, 0)),
            pl.BlockSpec((1, 1, d, f), lambda i, te, nact: (layer, te[i], 0, 0)),
            pl.BlockSpec((1, 1, f, d), lambda i, te, nact: (layer, te[i], 0, 0)),
        ],
        out_specs=pl.BlockSpec((tm, d), lambda i, te, nact: (i, 0)),
        scratch_shapes=[pltpu.VMEM((d, f), BF16), pltpu.VMEM((d, f), BF16), pltpu.VMEM((f, d), BF16)],
    )
    return pl.pallas_call(
        _moe_kernel,
        grid_spec=grid_spec,
        out_shape=jax.ShapeDtypeStruct((p_rows, d), F32),
        compiler_params=_params(("arbitrary",)),
        name="moe_experts",
    )(te, nact, xs, w1, w3, w2)


def _moe_rows(n):
    return ((TOP_K * n + N_EXPERTS * (MOE_TILE - 1)) // MOE_TILE) * MOE_TILE


def _route_tables(info, counts_row, n):
    tm = MOE_TILE
    n_tiles = _moe_rows(n) // tm
    ids = jnp.arange(N_EXPERTS, dtype=I32)
    counts = counts_row[0, N_GROUPS:N_GROUPS + N_EXPERTS].astype(I32)
    padded = ((counts + tm - 1) // tm) * tm
    cum = jnp.cumsum(padded)
    off = cum - padded
    e = info[:, INFO_E:INFO_E + TOP_K].astype(I32)
    rank = info[:, INFO_R:INFO_R + TOP_K].astype(I32)
    pos = (jnp.sum(jnp.where(e[:, :, None] == ids, off, 0), axis=-1) + rank).reshape(TOP_K * n)
    tile_start = jnp.arange(n_tiles, dtype=I32) * tm
    nact = cum[-1] // tm
    te_raw = jnp.sum((cum[None, :] <= tile_start[:, None]).astype(I32), axis=1)
    te_last = jnp.sum((cum <= (nact - 1) * tm).astype(I32))
    te = jnp.where(tile_start < cum[-1], jnp.minimum(te_raw, N_EXPERTS - 1), te_last)
    return pos, off + counts, cum, te, nact.reshape(1)


def _pad_lanes(a, width=LANES):
    return jnp.pad(a, [(0, 0)] * (a.ndim - 1) + [(0, width - a.shape[-1])])


def _seq_group(nb, want):
    g = want
    while nb % g:
        g //= 2
    return g


PROMPT_SEQS_PER_STEP = 4
SAMPLE_SEQS_PER_STEP = 8


def kernel(x_prompt, x_sample, state_conv_ab, state_delta_ab, state_gla_ab, state_hgrn_c, mix_norm, ab_w_in, ab_conv_w, ab_a_log, ab_dt_bias, ab_dn_norm, ab_gk_w2, ab_gk_b, ab_gla_norm, ab_w_out, c_w_in, c_lower_bounds, c_norm, c_w_out, ffn_norm, moe_w_group, moe_b_group, moe_w_expert, moe_b_expert, moe_w1, moe_w3, moe_w2, final_norm):
    bp, tp, d = x_prompt.shape
    bs, ts, _ = x_sample.shape
    n_p, n_s = bp * tp, bs * ts
    n = n_p + n_s
    depth = mix_norm.shape[0]
    xs_res = (x_prompt.reshape(n_p, d), x_sample.reshape(n_s, d))
    pg, sg = _seq_group(bp, PROMPT_SEQS_PER_STEP), _seq_group(bs, SAMPLE_SEQS_PER_STEP)

    info = pos = ys = None
    conv_p, delta_p, gla_p, hgrn_p = [], [], [], []
    conv_s, delta_s, gla_s, hgrn_s = [], [], [], []
    for layer in range(depth):
        j = layer // 2
        gain = mix_norm[layer].reshape(1, d)
        if layer % 2 == 0:
            u_w, a_w, b_w, z_w, gq_w, gk_w, gv_w, lr_w, gg_w = jnp.split(ab_w_in[j], _split_points(AB_SPLITS), axis=1)
            w_in = jnp.concatenate([u_w, z_w, gq_w, gk_w, gv_w, gg_w, _pad_lanes(jnp.concatenate([a_w, b_w, lr_w], axis=1))],
                                   axis=1).astype(BF16)
        else:
            w_in = c_w_in[j].astype(BF16)
        if ys is None:
            p_p, p_s = (_norm_proj(seg, gain, w_in) for seg in xs_res)
            off_s = 0
        else:
            x, p_p = _combine_norm(xs_res[0], info, pos, ys, gain, w_in)
            xs_res = (x,)
            p_s, off_s = p_p, n_p

        if layer % 2 == 0:
            prm = jnp.concatenate([_pad_lanes(ab_a_log[j].reshape(1, -1)), _pad_lanes(ab_dt_bias[j].reshape(1, -1)),
                                   ab_dn_norm[j].reshape(1, -1), ab_gla_norm[j].reshape(1, -1),
                                   jnp.zeros((4, LANES), F32)], axis=0)
            gkw = jnp.zeros((LANES, GLA_K), F32).at[MISC_LR:MISC_LR + GLA_RANK].set(ab_gk_w2[j])
            gkb = ab_gk_b[j].reshape(1, GLA_K)
            cw = ab_conv_w[j]
            o_p, c8, sd, sgl = _ab_mixer(p_p, 0, bp, tp, pg, cw, prm, gkw, gkb, None)
            conv_p.append(c8[:, SUBLANES - (CONV_W - 1):])
            delta_p.append(sd)
            gla_p.append(sgl)
            conv8 = jnp.pad(state_conv_ab[j], ((0, 0), (SUBLANES - (CONV_W - 1), 0), (0, 0)))
            o_s, c8, sd, sgl = _ab_mixer(p_s, off_s, bs, ts, sg, cw, prm, gkw, gkb,
                                         (conv8, state_delta_ab[j], state_gla_ab[j]))
            conv_s.append(c8[:, SUBLANES - (CONV_W - 1):])
            delta_s.append(sd)
            gla_s.append(sgl)
            w_out = ab_w_out[j]
        else:
            hg_gain = c_norm[j].reshape(1, -1)
            o_p, sh = _hgrn_mixer(p_p, 0, bp, tp, pg, c_lower_bounds, hg_gain, None, layer)
            hgrn_p.append(sh)
            o_s, sh = _hgrn_mixer(p_s, off_s, bs, ts, sg, c_lower_bounds, hg_gain, state_hgrn_c[j], layer)
            hgrn_s.append(sh)
            w_out = c_w_out[j]

        wr = _pad_lanes(jnp.concatenate([moe_w_group[layer], moe_w_expert[layer]], axis=1))
        br = _pad_lanes(jnp.concatenate([moe_b_group[layer], moe_b_expert[layer]]).reshape(1, -1))
        x, hn, info, counts = _outproj_router(o_p, o_s, xs_res, w_out.astype(BF16), ffn_norm[layer].reshape(1, d), wr, br)
        xs_res = (x,)
        pos, pad_lo, pad_hi, te, nact = _route_tables(info, counts, n)
        xsort = _dispatch(hn, pos, pad_lo, pad_hi, _moe_rows(n))
        ys = _moe(xsort, te, nact, moe_w1, moe_w3, moe_w2, layer)

    y_p, y_s = _combine_norm(xs_res[0], info, pos, ys, final_norm.reshape(1, d), None, n_a=n_p)
    y_prompt = y_p.reshape(bp, tp, d)
    y_sample = y_s.reshape(bs, ts, d)
    return (y_prompt, y_sample, jnp.stack(conv_p), jnp.stack(delta_p), jnp.stack(gla_p), jnp.stack(hgrn_p),
            jnp.stack(conv_s), jnp.stack(delta_s), jnp.stack(gla_s), jnp.stack(hgrn_s))


def _split_points(sizes):
    pts, acc = [], 0
    for s in sizes[:-1]:
        acc += s
        pts.append(acc)
    return pts
```

```python
import functools

import jax
import jax.numpy as jnp
from jax import lax
from jax.experimental import pallas as pl
from jax.experimental.pallas import tpu as pltpu

F32 = jnp.float32
BF16 = jnp.bfloat16
I32 = jnp.int32

D_MODEL = 1024
EPS = 1e-6
CHUNK = 64
CONV_W = 4
DN_HEADS, DN_DK, DN_DV = 4, 128, 128
DN_QK = DN_HEADS * DN_DK
DN_V = DN_HEADS * DN_DV
CONV_CH = 2 * DN_QK + DN_V
GLA_HEADS, GLA_DK, GLA_DV = 4, 64, 128
GLA_K = GLA_HEADS * GLA_DK
GLA_V = GLA_HEADS * GLA_DV
GLA_RANK = 16
GLA_NORMALIZER = 16.0
GLA_PACK = 128 // GLA_DK
AB_SPLITS = (CONV_CH, DN_HEADS, DN_HEADS, DN_V, GLA_K, GLA_K, GLA_V, GLA_RANK, GLA_V)
HG_HEADS, HG_DK, HG_DV = 8, 128, 128
HG_F = HG_HEADS * HG_DK
N_GROUPS, EXPERTS_PER_GROUP = 4, 8
N_EXPERTS = N_GROUPS * EXPERTS_PER_GROUP
TOP_K = 2
D_EXPERT = 512

LANES = 128
SUBLANES = 8
VMEM_LIMIT_BYTES = 56 * 1024 * 1024

AB_U, AB_Z, AB_GQ, AB_GK, AB_GV, AB_GG, AB_MISC = 0, 1536, 2048, 2304, 2560, 3072, 3584
AB_COLS = AB_MISC + LANES
MISC_A, MISC_B, MISC_LR = 0, DN_HEADS, 2 * DN_HEADS

INFO_E, INFO_W, INFO_R = 0, TOP_K, 2 * TOP_K

TOKEN_TILE = 256
MOE_TILE = 256
NEG = -1e30
LOG2E = 1.4426950408889634
GLA_SUB_BLOCK = SUBLANES


def _params(sem):
    return pltpu.CompilerParams(dimension_semantics=sem, vmem_limit_bytes=VMEM_LIMIT_BYTES)


def _mm(a, b):
    return jnp.dot(a.astype(BF16), b.astype(BF16), preferred_element_type=F32)


def _mm_nt(a, b):
    return lax.dot_general(a.astype(BF16), b.astype(BF16), (((1,), (1,)), ((), ())), preferred_element_type=F32)


def _mm_tn(a, b):
    return lax.dot_general(a.astype(BF16), b.astype(BF16), (((0,), (0,)), ((), ())), preferred_element_type=F32)


def _mm_hi(a, b):
    return jnp.dot(a, b, preferred_element_type=F32, precision=lax.Precision.HIGHEST)


def _sigmoid(x):
    return 1.0 / (1.0 + jnp.exp(-x))


def _softplus(x):
    return jnp.maximum(x, 0.0) + jnp.log(1.0 + jnp.exp(-jnp.abs(x)))


def _rms(x, gain):
    return x * lax.rsqrt(jnp.mean(x * x, axis=-1, keepdims=True) + EPS) * gain


def _gated_rms(o, gate, gain):
    return _rms(o, gain) * (gate * _sigmoid(gate))


ROW_TILE = D_MODEL // LANES
assert ROW_TILE == SUBLANES


def _store_token_tiles(ref, x):
    tm = x.shape[0]
    for j in range(ROW_TILE):
        ref[pl.ds(j, tm, stride=ROW_TILE), :] = x[:, j * LANES:(j + 1) * LANES]


def _load_token_tiles(ref):
    tm = ref.shape[0] // ROW_TILE
    return jnp.concatenate([ref[pl.ds(j, tm, stride=ROW_TILE), :] for j in range(ROW_TILE)], axis=1)


def _row_to_col(row):
    n = row.shape[1]
    r = lax.broadcasted_iota(I32, (n, n), 0)
    c = lax.broadcasted_iota(I32, (n, n), 1)
    return jnp.sum(jnp.where(r == c, jnp.broadcast_to(row, (n, n)), 0.0), axis=1, keepdims=True)


def _split2(x):
    hi = x.astype(BF16)
    return hi, (x - hi.astype(F32)).astype(BF16)


def _split3(x):
    hi = x.astype(BF16)
    r = x - hi.astype(F32)
    mid = r.astype(BF16)
    return hi, mid, (r - mid.astype(F32)).astype(BF16)


def _mm_cum(tri, x):
    hi, mid, lo = _split3(x)
    dot = lambda p: jnp.dot(tri, p, preferred_element_type=F32)
    return dot(hi) + (dot(mid) + dot(lo))


def _mm_split(a, b):
    ah, al = _split2(a)
    bh, bl = _split2(b)
    dot = lambda x, y: jnp.dot(x, y, preferred_element_type=F32)
    return dot(ah, bh) + (dot(ah, bl) + dot(al, bh))


def _delta_chunks(qs_, ks_, vs_, gcs_, grs_, betas_, ss_):
    heads = range(len(qs_))
    c = qs_[0].shape[0]
    ti = lax.broadcasted_iota(I32, (c, c), 0)
    tj = lax.broadcasted_iota(I32, (c, c), 1)
    eye = jnp.where(ti == tj, 1.0, 0.0)
    decay = [jnp.exp(jnp.where(ti >= tj, gcs_[h] - grs_[h], NEG)) for h in heads]
    qs = [qs_[h] * (DN_DK ** -0.5) for h in heads]
    kk = [_mm_nt(ks_[h], ks_[h]) for h in heads]
    nm = [jnp.where(ti > tj, betas_[h] * kk[h] * decay[h], 0.0) for h in heads]
    t = [eye - nm[h] for h in heads]
    p = nm
    step = 2
    while step < c:
        p = [_mm(p[h], p[h]) for h in heads]
        t = [t[h] + _mm(t[h], p[h]) for h in heads]
        step *= 2
    eg = [jnp.exp(gcs_[h]) for h in heads]
    rhs = [jnp.concatenate([betas_[h] * vs_[h], (betas_[h] * eg[h]) * ks_[h]], axis=1) for h in heads]
    sol = [_mm(t[h], rhs[h]) for h in heads]
    resid = [rhs[h] - (sol[h] + _mm_split(nm[h], sol[h])) for h in heads]
    sol = [sol[h] + _mm(t[h], resid[h]) for h in heads]
    qk = [_mm_nt(qs[h], ks_[h]) * decay[h] for h in heads]
    u = [sol[h][:, :DN_DV] - _mm(sol[h][:, DN_DV:], ss_[h]) for h in heads]
    o = [_mm(qs[h] * eg[h], ss_[h]) + _mm(qk[h], u[h]) for h in heads]
    gl = [gcs_[h][c - 1:c] for h in heads]
    s_new = [ss_[h] * jnp.exp(gl[h]) + _mm_tn(ks_[h] * jnp.exp(gl[h] - gcs_[h]), u[h]) for h in heads]
    return o, s_new


def _gla_chunks(qs, ks, gs, vss, ss, m):
    probs = range(len(qs))
    c = qs[0].shape[0]
    nh = len(vss[0])
    heads = range(nh)
    dk = LANES // nh
    dlane = lax.broadcasted_iota(I32, (1, LANES), 1)
    if nh == 1:
        mask = lambda x, h: x
    else:
        hm = [jnp.where(jnp.logical_and(dlane >= h * dk, dlane < (h + 1) * dk), 1.0, 0.0) for h in heads]
        mask = lambda x, h: x * hm[h]
    gs = [gs[i] * LOG2E for i in probs]
    qeg = [qs[i] * jnp.exp2(gs[i]) for i in probs]
    o = [[_mm(mask(qeg[i], h), ss[i]) for h in heads] for i in probs]
    lane = lax.broadcasted_iota(I32, (m, c), 1)
    sub = lax.broadcasted_iota(I32, (m, c), 0)
    blocks = [[[] for _ in heads] for _ in probs]
    for blk in range(c // m):
        r0 = blk * m
        qb = [qs[i][r0:r0 + m] for i in probs]
        gb = [gs[i][r0:r0 + m] for i in probs]
        kb = [ks[i][r0:r0 + m] for i in probs]
        if blk > 0:
            base = [gs[i][r0 - 1:r0] for i in probs]
            kt = [jnp.concatenate([ks[i][:r0] * jnp.exp2(base[i] - gs[i][:r0]), jnp.zeros((c - r0, LANES), F32)], axis=0)
                  for i in probs]
            qt = [qb[i] * jnp.exp2(gb[i] - base[i]) for i in probs]
            att = [[_mm_nt(mask(qt[i], h), kt[i]) for h in heads] for i in probs]
        else:
            att = [[jnp.zeros((m, c), F32) for _ in heads] for _ in probs]
        for j in range(m):
            keep = jnp.logical_and(lane == r0 + j, sub >= j)
            for i in probs:
                prod = qb[i] * (kb[i][j:j + 1] * jnp.exp2(gb[i] - gb[i][j:j + 1]))
                for h in heads:
                    att[i][h] = jnp.where(keep, jnp.sum(mask(prod, h), axis=1, keepdims=True), att[i][h])
        for i in probs:
            for h in heads:
                blocks[i][h].append(att[i][h])
    s_new = []
    for i in probs:
        gl = gs[i][c - 1:c]
        kg = ks[i] * jnp.exp2(gl - gs[i])
        s_i = ss[i] * _row_to_col(jnp.exp2(gl))
        for h in heads:
            att = blocks[i][h][0] if len(blocks[i][h]) == 1 else jnp.concatenate(blocks[i][h], axis=0)
            o[i][h] = o[i][h] + _mm(att, vss[i][h])
            s_i = s_i + _mm_tn(mask(kg, h), vss[i][h])
        s_new.append(s_i)
    return o, s_new


def _tri_incl(c):
    r = lax.broadcasted_iota(I32, (c, c), 0)
    cc = lax.broadcasted_iota(I32, (c, c), 1)
    return jnp.where(r >= cc, 1.0, 0.0).astype(BF16)


def _norm_proj_kernel(x_ref, gain_ref, w_ref, p_ref):
    h = _rms(x_ref[...], gain_ref[...])
    p_ref[...] = jnp.dot(h.astype(BF16), w_ref[...], preferred_element_type=F32)


def _norm_proj(x, gain, w):
    n, d = x.shape
    m = w.shape[1]
    tm = TOKEN_TILE
    return pl.pallas_call(
        _norm_proj_kernel,
        grid=(n // tm,),
        in_specs=[pl.BlockSpec((tm, d), lambda i: (i, 0)), pl.BlockSpec((1, d), lambda i: (0, 0)),
                  pl.BlockSpec((d, m), lambda i: (0, 0))],
        out_specs=pl.BlockSpec((tm, m), lambda i: (i, 0)),
        out_shape=jax.ShapeDtypeStruct((n, m), F32),
        compiler_params=_params(("parallel",)),
        name="norm_proj",
    )(x, gain, w)


def _combine_norm_kernel(pos_ref, *refs, project, tiles_a):
    if project:
        x_ref, info_ref, gain_ref, w_ref, ys_hbm, xs_ref, out_ref, ybuf, sem = refs
    else:
        x_ref, info_ref, gain_ref, ys_hbm, outa_ref, outb_ref, ybuf, sem = refs
    i = pl.program_id(0)
    n = pl.num_programs(0)
    tm = x_ref.shape[0]
    slot = i % 2

    def gather_start(tile, s):
        def body(r, carry):
            for k in range(TOP_K):
                src = pl.multiple_of(pos_ref[TOP_K * (tile * tm + r) + k] * ROW_TILE, ROW_TILE)
                dst = pl.multiple_of(r * ROW_TILE, ROW_TILE)
                pltpu.make_async_copy(ys_hbm.at[pl.ds(src, ROW_TILE)], ybuf.at[s, k, pl.ds(dst, ROW_TILE)],
                                      sem.at[s]).start(priority=k % 2)
            return carry
        lax.fori_loop(0, tm, body, 0, unroll=8)

    @pl.when(i == 0)
    def _():
        gather_start(0, 0)

    @pl.when(i + 1 < n)
    def _():
        gather_start(i + 1, 1 - slot)

    for k in range(TOP_K):
        pltpu.make_async_copy(ys_hbm.at[pl.ds(0, tm * ROW_TILE)], ybuf.at[slot, k], sem.at[slot]).wait()
    info = info_ref[...]
    y0, y1 = (_load_token_tiles(ybuf.at[slot, k]) for k in range(TOP_K))
    x = x_ref[...] + (info[:, INFO_W:INFO_W + 1] * y0 + info[:, INFO_W + 1:INFO_W + 2] * y1)
    h = _rms(x, gain_ref[...])
    if project:
        xs_ref[...] = x
        out_ref[...] = jnp.dot(h.astype(BF16), w_ref[...], preferred_element_type=F32)
    else:
        @pl.when(i < tiles_a)
        def _():
            outa_ref[...] = h

        @pl.when(i >= tiles_a)
        def _():
            outb_ref[...] = h


def _combine_norm(x, info, pos, ys, gain, w, n_a=None):
    n, d = x.shape
    tm = TOKEN_TILE
    project = w is not None
    tiles_a = None if project else n_a // tm
    row = lambda width: pl.BlockSpec((tm, width), lambda i, pos: (i, 0))
    in_specs = [row(d), row(LANES), pl.BlockSpec((1, d), lambda i, pos: (0, 0))]
    args = [x, info, gain]
    if project:
        m = w.shape[1]
        in_specs.append(pl.BlockSpec((d, m), lambda i, pos: (0, 0)))
        args.append(w)
        out_specs = [row(d), row(m)]
        out_shape = [jax.ShapeDtypeStruct((n, d), F32), jax.ShapeDtypeStruct((n, m), F32)]
    else:
        out_specs = [pl.BlockSpec((tm, d), lambda i, pos: (jnp.minimum(i, tiles_a - 1), 0)),
                     pl.BlockSpec((tm, d), lambda i, pos: (jnp.maximum(i - tiles_a, 0), 0))]
        out_shape = [jax.ShapeDtypeStruct((n_a, d), F32), jax.ShapeDtypeStruct((n - n_a, d), F32)]
    in_specs.append(pl.BlockSpec(memory_space=pl.ANY))
    args.append(ys)
    grid_spec = pltpu.PrefetchScalarGridSpec(
        num_scalar_prefetch=1, grid=(n // tm,), in_specs=in_specs, out_specs=out_specs,
        scratch_shapes=[pltpu.VMEM((2, TOP_K, tm * ROW_TILE, LANES), F32), pltpu.SemaphoreType.DMA((2,))])
    return pl.pallas_call(
        functools.partial(_combine_norm_kernel, project=project, tiles_a=tiles_a),
        grid_spec=grid_spec,
        out_shape=out_shape,
        compiler_params=_params(("arbitrary",)),
        name="combine_norm",
    )(pos, *args)


def _ab_mixer_kernel(*refs, c, nseq, zero_init):
    p_refs, refs = refs[:nseq], refs[nseq:]
    if zero_init:
        cw_ref, prm_ref, gkw_ref, gkb_ref, o_ref, conv_ref, sdn_ref, sgla_ref = refs
    else:
        (cw_ref, prm_ref, gkw_ref, gkb_ref, conv_in, sdn_in, sgla_in,
         o_ref, conv_ref, sdn_ref, sgla_ref) = refs

    @pl.when(pl.program_id(1) == 0)
    def _():
        if zero_init:
            conv_ref[...] = jnp.zeros_like(conv_ref)
            sdn_ref[...] = jnp.zeros_like(sdn_ref)
            sgla_ref[...] = jnp.zeros_like(sgla_ref)
        else:
            conv_ref[...] = conv_in[...]
            sdn_ref[...] = sdn_in[...]
            sgla_ref[...] = sgla_in[...]

    m = min(GLA_SUB_BLOCK, c)
    tri = _tri_incl(c)
    cw = cw_ref[...]
    a_log, dt_bias = prm_ref[0:1], prm_ref[1:2]
    dn_gain, gla_gain = prm_ref[2:3], prm_ref[3:4]
    n_pairs = GLA_HEADS // GLA_PACK
    conv_old = [conv_ref[b] for b in range(nseq)]
    sdn_old = [[sdn_ref[b, h] for h in range(DN_HEADS)] for b in range(nseq)]
    sgla_old = [[sgla_ref[b, pr] for pr in range(n_pairs)] for b in range(nseq)]
    dn = {key: [] for key in ("q", "k", "v", "gc", "gr", "beta", "s")}
    miscs = []
    for b in range(nseq):
        p_ref = p_refs[b]
        u = p_ref[:, AB_U:AB_U + CONV_CH]
        ucat = jnp.concatenate([conv_old[b], u], axis=0)
        acc = u * cw[CONV_W - 1:CONV_W]
        for j in range(1, CONV_W):
            acc = acc + pltpu.roll(ucat, j, 0)[SUBLANES:SUBLANES + c] * cw[CONV_W - 1 - j:CONV_W - j]
        conv_ref[b] = ucat[c:c + SUBLANES]
        qkv = acc * _sigmoid(acc)

        misc = p_ref[:, AB_MISC:AB_MISC + LANES]
        miscs.append(misc)
        g_all = -jnp.exp(a_log) * _softplus(misc + dt_bias)
        beta_all = _sigmoid(misc)
        gcum = _mm_cum(tri, g_all)
        gcum_t = jnp.concatenate([gcum, jnp.zeros((LANES - c, LANES), F32)], axis=0).T
        for h in range(DN_HEADS):
            lo = h * DN_DK
            q = qkv[:, lo:lo + DN_DK]
            k = qkv[:, DN_QK + lo:DN_QK + lo + DN_DK]
            dn["q"].append(q * lax.rsqrt(jnp.sum(q * q, axis=-1, keepdims=True) + 1e-6))
            dn["k"].append(k * lax.rsqrt(jnp.sum(k * k, axis=-1, keepdims=True) + 1e-6))
            dn["v"].append(qkv[:, 2 * DN_QK + h * DN_DV:2 * DN_QK + (h + 1) * DN_DV])
            dn["gc"].append(gcum[:, MISC_A + h:MISC_A + h + 1])
            dn["gr"].append(gcum_t[MISC_A + h:MISC_A + h + 1, :c])
            dn["beta"].append(beta_all[:, MISC_B + h:MISC_B + h + 1])
            dn["s"].append(sdn_old[b][h])
    os_, ss_ = _delta_chunks(dn["q"], dn["k"], dn["v"], dn["gc"], dn["gr"], dn["beta"], dn["s"])
    for b in range(nseq):
        for h in range(DN_HEADS):
            i = b * DN_HEADS + h
            sdn_ref[b, h] = ss_[i]
            z = p_refs[b][:, AB_Z + h * DN_DV:AB_Z + (h + 1) * DN_DV]
            o_ref[b, :, h * DN_DV:(h + 1) * DN_DV] = _gated_rms(os_[i], z, dn_gain).astype(o_ref.dtype)

    gl = {key: [] for key in ("q", "k", "g", "v", "s")}
    for b in range(nseq):
        p_ref = p_refs[b]
        logits = _mm_split(miscs[b], gkw_ref[...]) + gkb_ref[...]
        logf = -_softplus(-logits) * (1.0 / GLA_NORMALIZER)
        gall = _mm_cum(tri, logf)
        for pair in range(n_pairs):
            lo = pair * LANES
            gl["q"].append(p_ref[:, AB_GQ + lo:AB_GQ + lo + LANES] * (GLA_DK ** -0.5))
            gl["k"].append(p_ref[:, AB_GK + lo:AB_GK + lo + LANES])
            gl["g"].append(gall[:, lo:lo + LANES])
            gl["v"].append([p_ref[:, AB_GV + h * GLA_DV:AB_GV + (h + 1) * GLA_DV]
                            for h in range(pair * GLA_PACK, (pair + 1) * GLA_PACK)])
            gl["s"].append(sgla_old[b][pair])
    os_, ss_ = _gla_chunks(gl["q"], gl["k"], gl["g"], gl["v"], gl["s"], m)
    for b in range(nseq):
        for pair in range(n_pairs):
            i = b * n_pairs + pair
            sgla_ref[b, pair] = ss_[i]
            for hh in range(GLA_PACK):
                h = pair * GLA_PACK + hh
                gate = p_refs[b][:, AB_GG + h * GLA_DV:AB_GG + (h + 1) * GLA_DV]
                o_ref[b, :, DN_V + h * GLA_DV:DN_V + (h + 1) * GLA_DV] = \
                    _gated_rms(os_[i][hh], gate, gla_gain).astype(o_ref.dtype)


def _seq_specs(cols, row0, c, nchunk, nseq):
    blk0 = row0 // c
    return [pl.BlockSpec((c, cols), lambda g, i, s=s: (blk0 + (g * nseq + s) * nchunk + i, 0)) for s in range(nseq)]


def _ab_mixer(p, row0, nb, t, nseq, cw, prm, gkw, gkb, states):
    c = min(CHUNK, t)
    nchunk = t // c
    n_groups = nb // nseq
    zero_init = states is None
    st_map3 = lambda g, i: (g, 0, 0)
    st_map4 = lambda g, i: (g, 0, 0, 0)
    const = lambda shape: pl.BlockSpec(shape, lambda g, i: (0,) * len(shape))
    in_specs = _seq_specs(AB_COLS, row0, c, nchunk, nseq) + [const(cw.shape), const(prm.shape), const(gkw.shape),
                                                             const(gkb.shape)]
    args = [p] * nseq + [cw, prm, gkw, gkb]
    gla_packed = (GLA_HEADS // GLA_PACK, GLA_PACK * GLA_DK, GLA_DV)
    st_specs = [pl.BlockSpec((nseq, SUBLANES, CONV_CH), st_map3),
                pl.BlockSpec((nseq, DN_HEADS, DN_DK, DN_DV), st_map4),
                pl.BlockSpec((nseq,) + gla_packed, st_map4)]
    if not zero_init:
        conv8, sdn, sgla = states
        in_specs += st_specs
        args += [conv8, sdn, sgla.reshape((nb,) + gla_packed)]
    out_shape = [jax.ShapeDtypeStruct((nb, t, D_MODEL), BF16),
                 jax.ShapeDtypeStruct((nb, SUBLANES, CONV_CH), F32),
                 jax.ShapeDtypeStruct((nb, DN_HEADS, DN_DK, DN_DV), F32),
                 jax.ShapeDtypeStruct((nb,) + gla_packed, F32)]
    out_specs = [pl.BlockSpec((nseq, c, D_MODEL), lambda g, i: (g, i, 0))] + st_specs
    o, conv_out, sdn_out, sgla_out = pl.pallas_call(
        functools.partial(_ab_mixer_kernel, c=c, nseq=nseq, zero_init=zero_init),
        grid=(n_groups, nchunk),
        in_specs=in_specs,
        out_specs=out_specs,
        out_shape=out_shape,
        compiler_params=_params(("parallel", "arbitrary")),
        name="ab_mixer",
    )(*args)
    return o.reshape(nb * t, D_MODEL), conv_out, sdn_out, sgla_out.reshape(nb, GLA_HEADS, GLA_DK, GLA_DV)


def _hgrn_mixer_kernel(*refs, c, nseq, zero_init, layer):
    p_refs, refs = refs[:nseq], refs[nseq:]
    if zero_init:
        lb_ref, gain_ref, o_ref, s_ref = refs
    else:
        lb_ref, gain_ref, s_in, o_ref, s_ref = refs

    @pl.when(pl.program_id(1) == 0)
    def _():
        if zero_init:
            s_ref[...] = jnp.zeros_like(s_ref)
        else:
            s_ref[...] = s_in[...]

    raw = lb_ref[...]
    e = jnp.exp(raw - jnp.max(raw, axis=0, keepdims=True))
    sm = e / jnp.sum(e, axis=0, keepdims=True)
    cum = sm[0:1]
    for l in range(1, layer + 1):
        cum = cum + sm[l:l + 1]
    lb = cum - sm[0:1]

    m = min(GLA_SUB_BLOCK, c)
    tri = _tri_incl(c)
    gain = gain_ref[...]
    pr = {key: [] for key in ("q", "k", "g", "v", "s")}
    for b in range(nseq):
        p_ref = p_refs[b]
        fr = p_ref[:, HG_F:2 * HG_F]
        logf = jnp.log(lb + (1.0 - lb) * _sigmoid(fr))
        kall = (1.0 - lb) * _sigmoid(-fr)
        gall = _mm_cum(tri, logf)
        for h in range(HG_HEADS):
            lo = h * HG_DK
            pr["q"].append(p_ref[:, lo:lo + HG_DK] * (HG_DK ** -0.5))
            pr["k"].append(kall[:, lo:lo + HG_DK])
            pr["g"].append(gall[:, lo:lo + HG_DK])
            pr["v"].append([p_ref[:, 2 * HG_F + h * HG_DV:2 * HG_F + (h + 1) * HG_DV]])
            pr["s"].append(s_ref[b, h])
    os_, ss_ = _gla_chunks(pr["q"], pr["k"], pr["g"], pr["v"], pr["s"], m)
    for b in range(nseq):
        for h in range(HG_HEADS):
            i = b * HG_HEADS + h
            s_ref[b, h] = ss_[i]
            gate = p_refs[b][:, 2 * HG_F + D_MODEL + h * HG_DV:2 * HG_F + D_MODEL + (h + 1) * HG_DV]
            o_ref[b, :, h * HG_DV:(h + 1) * HG_DV] = _gated_rms(os_[i][0], gate, gain).astype(o_ref.dtype)


def _hgrn_mixer(p, row0, nb, t, nseq, lb, gain, state, layer):
    c = min(CHUNK, t)
    nchunk = t // c
    zero_init = state is None
    const = lambda shape: pl.BlockSpec(shape, lambda g, i: (0,) * len(shape))
    st_spec = pl.BlockSpec((nseq, HG_HEADS, HG_DK, HG_DV), lambda g, i: (g, 0, 0, 0))
    in_specs = _seq_specs(p.shape[1], row0, c, nchunk, nseq) + [const(lb.shape), const(gain.shape)]
    args = [p] * nseq + [lb, gain]
    if not zero_init:
        in_specs.append(st_spec)
        args.append(state)
    o, s_out = pl.pallas_call(
        functools.partial(_hgrn_mixer_kernel, c=c, nseq=nseq, zero_init=zero_init, layer=layer),
        grid=(nb // nseq, nchunk),
        in_specs=in_specs,
        out_specs=[pl.BlockSpec((nseq, c, D_MODEL), lambda g, i: (g, i, 0)), st_spec],
        out_shape=[jax.ShapeDtypeStruct((nb, t, D_MODEL), BF16),
                   jax.ShapeDtypeStruct((nb, HG_HEADS, HG_DK, HG_DV), F32)],
        compiler_params=_params(("parallel", "arbitrary")),
        name="hgrn_mixer",
    )(*args)
    return o.reshape(nb * t, D_MODEL), s_out


def _outproj_router_kernel(*refs, tiles_a, split_x):
    if split_x:
        oa_ref, ob_ref, xa_ref, xb_ref = refs[:4]
        refs = refs[4:]
    else:
        oa_ref, ob_ref, x_ref = refs[:3]
        refs = refs[3:]
    wout_ref, gain_ref, wr_ref, br_ref, x1_ref, h_ref, info_ref, cnt_ref, run_ref = refs
    first = pl.program_id(0) < tiles_a

    @pl.when(pl.program_id(0) == 0)
    def _():
        run_ref[...] = jnp.zeros_like(run_ref)

    o = jnp.where(first, oa_ref[...], ob_ref[...])
    x = jnp.where(first, xa_ref[...], xb_ref[...]) if split_x else x_ref[...]
    x1 = x + jnp.dot(o, wout_ref[...], preferred_element_type=F32)
    x1_ref[...] = x1
    h = _rms(x1, gain_ref[...])
    _store_token_tiles(h_ref, h)
    logit = _mm_hi(h, wr_ref[...]) + br_ref[...]
    lane = lax.broadcasted_iota(I32, logit.shape, 1)
    far = jnp.int32(LANES)
    lg = jnp.where(lane < N_GROUPS, logit, NEG)
    mg = jnp.max(lg, axis=-1, keepdims=True)
    p_g = 1.0 / jnp.sum(jnp.exp(lg - mg), axis=-1, keepdims=True)
    g_top = jnp.min(jnp.where(lg == mg, lane, far), axis=-1, keepdims=True)
    lo = N_GROUPS + g_top * EXPERTS_PER_GROUP
    sel = jnp.logical_and(lane >= lo, lane < lo + EXPERTS_PER_GROUP)
    le = jnp.where(sel, logit, NEG)
    ee = jnp.exp(le - jnp.max(le, axis=-1, keepdims=True))
    pe = jnp.where(sel, ee / jnp.sum(ee, axis=-1, keepdims=True), -1.0)
    p1 = jnp.max(pe, axis=-1, keepdims=True)
    i1 = jnp.min(jnp.where(pe == p1, lane, far), axis=-1, keepdims=True)
    pe2 = jnp.where(lane == i1, -1.0, pe)
    p2 = jnp.max(pe2, axis=-1, keepdims=True)
    i2 = jnp.min(jnp.where(pe2 == p2, lane, far), axis=-1, keepdims=True)
    denom = p1 + p2
    w1 = p_g * (p1 / denom)
    w2 = p_g * (p2 / denom)
    tm = logit.shape[0]
    oh1 = jnp.where(lane == i1, 1.0, 0.0)
    oh2 = jnp.where(lane == i2, 1.0, 0.0)
    both = oh1 + oh2
    tr = lax.broadcasted_iota(I32, (tm, tm), 0)
    tc = lax.broadcasted_iota(I32, (tm, tm), 1)
    before = run_ref[...] + _mm(jnp.where(tr > tc, 1.0, 0.0), both)
    r1 = jnp.sum(oh1 * before, axis=-1, keepdims=True)
    r2 = jnp.sum(oh2 * before, axis=-1, keepdims=True)
    run = run_ref[...] + jnp.sum(both, axis=0, keepdims=True)
    run_ref[...] = run
    cnt_ref[...] = jnp.broadcast_to(run, cnt_ref.shape)
    vals = ((i1 - N_GROUPS).astype(F32), (i2 - N_GROUPS).astype(F32), w1, w2, r1, r2)
    info = jnp.zeros_like(logit)
    for idx, val in enumerate(vals):
        info = jnp.where(lane == idx, val, info)
    info_ref[...] = info


def _outproj_router(o_a, o_b, xs, wout, gain, wr, br):
    n_a, d = o_a.shape
    n = n_a + o_b.shape[0]
    tm = TOKEN_TILE
    tiles_a = n_a // tm
    row = lambda w: pl.BlockSpec((tm, w), lambda i: (i, 0))
    seg_a = pl.BlockSpec((tm, d), lambda i: (jnp.minimum(i, tiles_a - 1), 0))
    seg_b = pl.BlockSpec((tm, d), lambda i: (jnp.maximum(i - tiles_a, 0), 0))
    const = lambda shape: pl.BlockSpec(shape, lambda i: (0,) * len(shape))
    split_x = len(xs) == 2
    return pl.pallas_call(
        functools.partial(_outproj_router_kernel, tiles_a=tiles_a, split_x=split_x),
        grid=(n // tm,),
        in_specs=[seg_a, seg_b] + ([seg_a, seg_b] if split_x else [row(d)])
        + [const((d, d)), const((1, d)), const((d, LANES)), const((1, LANES))],
        out_specs=[row(d), pl.BlockSpec((tm * ROW_TILE, LANES), lambda i: (i, 0)), row(LANES), const((SUBLANES, LANES))],
        out_shape=[jax.ShapeDtypeStruct((n, d), F32), jax.ShapeDtypeStruct((n * ROW_TILE, LANES), F32),
                   jax.ShapeDtypeStruct((n, LANES), F32), jax.ShapeDtypeStruct((SUBLANES, LANES), F32)],
        scratch_shapes=[pltpu.VMEM((1, LANES), F32)],
        compiler_params=_params(("arbitrary",)),
        name="outproj_router",
    )(o_a, o_b, *xs, wout, gain, wr, br)


def _dispatch_kernel(pos_ref, lo_ref, hi_ref, h_ref, xs_hbm, buf, zrow, sem, zsem):
    i = pl.program_id(0)
    n = pl.num_programs(0)
    tm = h_ref.shape[0] // ROW_TILE
    blk = tm * ROW_TILE
    slot = i % 2
    token = lambda r: pl.ds(pl.multiple_of(r * ROW_TILE, ROW_TILE), ROW_TILE)

    def tile_wait(s):
        for _ in range(TOP_K):
            pltpu.make_async_copy(buf.at[s], xs_hbm.at[pl.ds(0, blk)], sem.at[s]).wait()

    def pad_rows(fn, tile_fn):
        for e in range(N_EXPERTS):
            lax.fori_loop(lo_ref[e], hi_ref[e], fn, 0)
        lax.fori_loop(hi_ref[N_EXPERTS - 1] // tm, xs_hbm.shape[0] // blk, tile_fn, 0)

    def pad_start(r, carry):
        pltpu.make_async_copy(zrow.at[token(0)], xs_hbm.at[token(r)], zsem.at[0]).start()
        return carry

    def pad_wait(r, carry):
        pltpu.make_async_copy(zrow.at[token(0)], xs_hbm.at[token(0)], zsem.at[0]).wait()
        return carry

    def pad_tile_start(t, carry):
        pltpu.make_async_copy(zrow, xs_hbm.at[pl.ds(pl.multiple_of(t * blk, blk), blk)], zsem.at[0]).start()
        return carry

    def pad_tile_wait(t, carry):
        pltpu.make_async_copy(zrow, xs_hbm.at[pl.ds(0, blk)], zsem.at[0]).wait()
        return carry

    @pl.when(i == 0)
    def _():
        zrow[...] = jnp.zeros_like(zrow)
        pad_rows(pad_start, pad_tile_start)

    @pl.when(i >= 2)
    def _():
        tile_wait(slot)
    buf[slot] = h_ref[...]

    def body(r, carry):
        for k in range(TOP_K):
            row = pos_ref[TOP_K * (i * tm + r) + k]
            pltpu.make_async_copy(buf.at[slot, token(r)], xs_hbm.at[token(row)], sem.at[slot]).start(priority=k % 2)
        return carry
    lax.fori_loop(0, tm, body, 0, unroll=8)

    @pl.when(i == n - 1)
    def _():
        tile_wait(slot)

        @pl.when(n > 1)
        def _():
            tile_wait(1 - slot)
        pad_rows(pad_wait, pad_tile_wait)


def _dispatch(h, pos, pad_lo, pad_hi, n_rows):
    blk = TOKEN_TILE * ROW_TILE
    n = h.shape[0] // ROW_TILE
    grid_spec = pltpu.PrefetchScalarGridSpec(
        num_scalar_prefetch=3,
        grid=(n // TOKEN_TILE,),
        in_specs=[pl.BlockSpec((blk, LANES), lambda i, pos, lo, hi: (i, 0))],
        out_specs=pl.BlockSpec(memory_space=pl.ANY),
        scratch_shapes=[pltpu.VMEM((2, blk, LANES), F32), pltpu.VMEM((blk, LANES), F32),
                        pltpu.SemaphoreType.DMA((2,)), pltpu.SemaphoreType.DMA((1,))],
    )
    return pl.pallas_call(
        _dispatch_kernel,
        grid_spec=grid_spec,
        out_shape=jax.ShapeDtypeStruct((n_rows * ROW_TILE, LANES), F32),
        compiler_params=_params(("arbitrary",)),
        name="moe_dispatch",
    )(pos, pad_lo, pad_hi, h)


def _moe_kernel(te_ref, nact_ref, x_ref, w1_ref, w3_ref, w2_ref, y_ref, w1b, w3b, w2b):
    i = pl.program_id(0)

    @pl.when(i < nact_ref[0])
    def _():
        @pl.when(jnp.logical_or(i == 0, te_ref[i] != te_ref[jnp.maximum(i - 1, 0)]))
        def _():
            w1b[...] = w1_ref[0, 0].astype(BF16)
            w3b[...] = w3_ref[0, 0].astype(BF16)
            w2b[...] = w2_ref[0, 0].astype(BF16)

        xb = _load_token_tiles(x_ref).astype(BF16)
        h1 = jnp.dot(xb, w1b[...], preferred_element_type=F32)
        h3 = jnp.dot(xb, w3b[...], preferred_element_type=F32)
        act = (h1 * _sigmoid(h1)) * h3
        _store_token_tiles(y_ref, jnp.dot(act.astype(BF16), w2b[...], preferred_element_type=F32))

    @pl.when(i >= nact_ref[0])
    def _():
        y_ref[...] = jnp.zeros_like(y_ref)


def _moe(xs, te, nact, w1, w3, w2, layer):
    d, f = w1.shape[2], w1.shape[3]
    tm = MOE_TILE
    blk = tm * ROW_TILE
    n_tiles = xs.shape[0] // blk
    last = lambda i, nact: jnp.maximum(jnp.minimum(i, nact[0] - 1), 0)
    grid_spec = pltpu.PrefetchScalarGridSpec(
        num_scalar_prefetch=2,
        grid=(n_tiles,),
        in_specs=[
            pl.BlockSpec((blk, LANES), lambda i, te, nact: (last(i, nact), 0)),
            pl.BlockSpec((1, 1, d, f), lambda i, te, nact: (layer, te[i], 0, 0)),
            pl.BlockSpec((1, 1, d, f), lambda i, te, nact: (layer, te[i], 0, 0)),
            pl.BlockSpec((1, 1, f, d), lambda i, te, nact: (layer, te[i], 0, 0)),
        ],
        out_specs=pl.BlockSpec((blk, LANES), lambda i, te, nact: (i, 0)),
        scratch_shapes=[pltpu.VMEM((d, f), BF16), pltpu.VMEM((d, f), BF16), pltpu.VMEM((f, d), BF16)],
    )
    return pl.pallas_call(
        _moe_kernel,
        grid_spec=grid_spec,
        out_shape=jax.ShapeDtypeStruct(xs.shape, F32),
        compiler_params=_params(("arbitrary",)),
        name="moe_experts",
    )(te, nact, xs, w1, w3, w2)


def _moe_rows(n):
    return ((TOP_K * n + N_EXPERTS * (MOE_TILE - 1)) // MOE_TILE) * MOE_TILE


def _route_tables(info, counts_row, n):
    tm = MOE_TILE
    n_tiles = _moe_rows(n) // tm
    ids = jnp.arange(N_EXPERTS, dtype=I32)
    counts = counts_row[0, N_GROUPS:N_GROUPS + N_EXPERTS].astype(I32)
    padded = ((counts + tm - 1) // tm) * tm
    cum = jnp.cumsum(padded)
    off = cum - padded
    e = info[:, INFO_E:INFO_E + TOP_K].astype(I32)
    rank = info[:, INFO_R:INFO_R + TOP_K].astype(I32)
    pos = (jnp.sum(jnp.where(e[:, :, None] == ids, off, 0), axis=-1) + rank).reshape(TOP_K * n)
    tile_start = jnp.arange(n_tiles, dtype=I32) * tm
    nact = cum[-1] // tm
    te_raw = jnp.sum((cum[None, :] <= tile_start[:, None]).astype(I32), axis=1)
    te_last = jnp.sum((cum <= (nact - 1) * tm).astype(I32))
    te = jnp.where(tile_start < cum[-1], jnp.minimum(te_raw, N_EXPERTS - 1), te_last)
    return pos, off + counts, cum, te, nact.reshape(1)


def _pad_lanes(a, width=LANES):
    return jnp.pad(a, [(0, 0)] * (a.ndim - 1) + [(0, width - a.shape[-1])])


def _seq_group(nb, want):
    g = want
    while nb % g:
        g //= 2
    return g


PROMPT_SEQS_PER_STEP = 4
SAMPLE_SEQS_PER_STEP = 8


def kernel(x_prompt, x_sample, state_conv_ab, state_delta_ab, state_gla_ab, state_hgrn_c, mix_norm, ab_w_in, ab_conv_w, ab_a_log, ab_dt_bias, ab_dn_norm, ab_gk_w2, ab_gk_b, ab_gla_norm, ab_w_out, c_w_in, c_lower_bounds, c_norm, c_w_out, ffn_norm, moe_w_group, moe_b_group, moe_w_expert, moe_b_expert, moe_w1, moe_w3, moe_w2, final_norm):
    bp, tp, d = x_prompt.shape
    bs, ts, _ = x_sample.shape
    n_p, n_s = bp * tp, bs * ts
    n = n_p + n_s
    depth = mix_norm.shape[0]
    xs_res = (x_prompt.reshape(n_p, d), x_sample.reshape(n_s, d))
    pg, sg = _seq_group(bp, PROMPT_SEQS_PER_STEP), _seq_group(bs, SAMPLE_SEQS_PER_STEP)

    info = pos = ys = None
    conv_p, delta_p, gla_p, hgrn_p = [], [], [], []
    conv_s, delta_s, gla_s, hgrn_s = [], [], [], []
    for layer in range(depth):
        j = layer // 2
        gain = mix_norm[layer].reshape(1, d)
        if layer % 2 == 0:
            u_w, a_w, b_w, z_w, gq_w, gk_w, gv_w, lr_w, gg_w = jnp.split(ab_w_in[j], _split_points(AB_SPLITS), axis=1)
            w_in = jnp.concatenate([u_w, z_w, gq_w, gk_w, gv_w, gg_w, _pad_lanes(jnp.concatenate([a_w, b_w, lr_w], axis=1))],
                                   axis=1).astype(BF16)
        else:
            w_in = c_w_in[j].astype(BF16)
        if ys is None:
            p_p, p_s = (_norm_proj(seg, gain, w_in) for seg in xs_res)
            off_s = 0
        else:
            x, p_p = _combine_norm(xs_res[0], info, pos, ys, gain, w_in)
            xs_res = (x,)
            p_s, off_s = p_p, n_p

        if layer % 2 == 0:
            prm = jnp.concatenate([_pad_lanes(ab_a_log[j].reshape(1, -1)), _pad_lanes(ab_dt_bias[j].reshape(1, -1)),
                                   ab_dn_norm[j].reshape(1, -1), ab_gla_norm[j].reshape(1, -1),
                                   jnp.zeros((4, LANES), F32)], axis=0)
            gkw = jnp.zeros((LANES, GLA_K), F32).at[MISC_LR:MISC_LR + GLA_RANK].set(ab_gk_w2[j])
            gkb = ab_gk_b[j].reshape(1, GLA_K)
            cw = ab_conv_w[j]
            o_p, c8, sd, sgl = _ab_mixer(p_p, 0, bp, tp, pg, cw, prm, gkw, gkb, None)
            conv_p.append(c8[:, SUBLANES - (CONV_W - 1):])
            delta_p.append(sd)
            gla_p.append(sgl)
            conv8 = jnp.pad(state_conv_ab[j], ((0, 0), (SUBLANES - (CONV_W - 1), 0), (0, 0)))
            o_s, c8, sd, sgl = _ab_mixer(p_s, off_s, bs, ts, sg, cw, prm, gkw, gkb,
                                         (conv8, state_delta_ab[j], state_gla_ab[j]))
            conv_s.append(c8[:, SUBLANES - (CONV_W - 1):])
            delta_s.append(sd)
            gla_s.append(sgl)
            w_out = ab_w_out[j]
        else:
            hg_gain = c_norm[j].reshape(1, -1)
            o_p, sh = _hgrn_mixer(p_p, 0, bp, tp, pg, c_lower_bounds, hg_gain, None, layer)
            hgrn_p.append(sh)
            o_s, sh = _hgrn_mixer(p_s, off_s, bs, ts, sg, c_lower_bounds, hg_gain, state_hgrn_c[j], layer)
            hgrn_s.append(sh)
            w_out = c_w_out[j]

        wr = _pad_lanes(jnp.concatenate([moe_w_group[layer], moe_w_expert[layer]], axis=1))
        br = _pad_lanes(jnp.concatenate([moe_b_group[layer], moe_b_expert[layer]]).reshape(1, -1))
        x, hn, info, counts = _outproj_router(o_p, o_s, xs_res, w_out.astype(BF16), ffn_norm[layer].reshape(1, d), wr, br)
        xs_res = (x,)
        pos, pad_lo, pad_hi, te, nact = _route_tables(info, counts, n)
        xsort = _dispatch(hn, pos, pad_lo, pad_hi, _moe_rows(n))
        ys = _moe(xsort, te, nact, moe_w1, moe_w3, moe_w2, layer)

    y_p, y_s = _combine_norm(xs_res[0], info, pos, ys, final_norm.reshape(1, d), None, n_a=n_p)
    y_prompt = y_p.reshape(bp, tp, d)
    y_sample = y_s.reshape(bs, ts, d)
    return (y_prompt, y_sample, jnp.stack(conv_p), jnp.stack(delta_p), jnp.stack(gla_p), jnp.stack(hgrn_p),
            jnp.stack(conv_s), jnp.stack(delta_s), jnp.stack(gla_s), jnp.stack(hgrn_s))


def _split_points(sizes):
    pts, acc = [], 0
    for s in sizes[:-1]:
        acc += s
        pts.append(acc)
    return pts
```

```python
import functools

import jax
import jax.numpy as jnp
from jax import lax
from jax.experimental import pallas as pl
from jax.experimental.pallas import tpu as pltpu

F32 = jnp.float32
BF16 = jnp.bfloat16
I32 = jnp.int32

D_MODEL = 1024
EPS = 1e-6
CHUNK = 64
CONV_W = 4
DN_HEADS, DN_DK, DN_DV = 4, 128, 128
DN_QK = DN_HEADS * DN_DK
DN_V = DN_HEADS * DN_DV
CONV_CH = 2 * DN_QK + DN_V
GLA_HEADS, GLA_DK, GLA_DV = 4, 64, 128
GLA_K = GLA_HEADS * GLA_DK
GLA_V = GLA_HEADS * GLA_DV
GLA_RANK = 16
GLA_NORMALIZER = 16.0
GLA_PACK = 128 // GLA_DK
AB_SPLITS = (CONV_CH, DN_HEADS, DN_HEADS, DN_V, GLA_K, GLA_K, GLA_V, GLA_RANK, GLA_V)
HG_HEADS, HG_DK, HG_DV = 8, 128, 128
HG_F = HG_HEADS * HG_DK
N_GROUPS, EXPERTS_PER_GROUP = 4, 8
N_EXPERTS = N_GROUPS * EXPERTS_PER_GROUP
TOP_K = 2
D_EXPERT = 512

LANES = 128
SUBLANES = 8
VMEM_LIMIT_BYTES = 56 * 1024 * 1024

AB_U, AB_Z, AB_GQ, AB_GK, AB_GV, AB_GG, AB_MISC = 0, 1536, 2048, 2304, 2560, 3072, 3584
AB_COLS = AB_MISC + LANES
MISC_A, MISC_B, MISC_LR = 0, DN_HEADS, 2 * DN_HEADS

INFO_E, INFO_W, INFO_R = 0, TOP_K, 2 * TOP_K

TOKEN_TILE = 512
MOE_TILE = 512
NEG = -1e30
LOG2E = 1.4426950408889634
GLA_SUB_BLOCK = SUBLANES


def _params(sem):
    return pltpu.CompilerParams(dimension_semantics=sem, vmem_limit_bytes=VMEM_LIMIT_BYTES)


def _mm(a, b):
    return jnp.dot(a.astype(BF16), b.astype(BF16), preferred_element_type=F32)


def _mm_nt(a, b):
    return lax.dot_general(a.astype(BF16), b.astype(BF16), (((1,), (1,)), ((), ())), preferred_element_type=F32)


def _mm_tn(a, b):
    return lax.dot_general(a.astype(BF16), b.astype(BF16), (((0,), (0,)), ((), ())), preferred_element_type=F32)


def _mm_hi(a, b):
    return jnp.dot(a, b, preferred_element_type=F32, precision=lax.Precision.HIGHEST)


def _sigmoid(x):
    return 1.0 / (1.0 + jnp.exp(-x))


def _softplus(x):
    return jnp.maximum(x, 0.0) + jnp.log(1.0 + jnp.exp(-jnp.abs(x)))


def _rms(x, gain):
    return x * lax.rsqrt(jnp.mean(x * x, axis=-1, keepdims=True) + EPS) * gain


def _gated_rms(o, gate, gain):
    return _rms(o, gain) * (gate * _sigmoid(gate))


ROW_TILE = D_MODEL // LANES
assert ROW_TILE == SUBLANES


def _store_token_tiles(ref, x):
    tm = x.shape[0]
    for j in range(ROW_TILE):
        ref[pl.ds(j, tm, stride=ROW_TILE), :] = x[:, j * LANES:(j + 1) * LANES]


def _load_token_tiles(ref):
    tm = ref.shape[0] // ROW_TILE
    return jnp.concatenate([ref[pl.ds(j, tm, stride=ROW_TILE), :] for j in range(ROW_TILE)], axis=1)


def _row_to_col(row):
    n = row.shape[1]
    r = lax.broadcasted_iota(I32, (n, n), 0)
    c = lax.broadcasted_iota(I32, (n, n), 1)
    return jnp.sum(jnp.where(r == c, jnp.broadcast_to(row, (n, n)), 0.0), axis=1, keepdims=True)


def _split2(x):
    hi = x.astype(BF16)
    return hi, (x - hi.astype(F32)).astype(BF16)


def _split3(x):
    hi = x.astype(BF16)
    r = x - hi.astype(F32)
    mid = r.astype(BF16)
    return hi, mid, (r - mid.astype(F32)).astype(BF16)


def _mm_cum(tri, x):
    hi, mid, lo = _split3(x)
    dot = lambda p: jnp.dot(tri, p, preferred_element_type=F32)
    return dot(hi) + (dot(mid) + dot(lo))


def _mm_split(a, b):
    ah, al = _split2(a)
    bh, bl = _split2(b)
    dot = lambda x, y: jnp.dot(x, y, preferred_element_type=F32)
    return dot(ah, bh) + (dot(ah, bl) + dot(al, bh))


def _delta_chunks(qs_, ks_, vs_, gcs_, grs_, betas_, ss_):
    heads = range(len(qs_))
    c = qs_[0].shape[0]
    ti = lax.broadcasted_iota(I32, (c, c), 0)
    tj = lax.broadcasted_iota(I32, (c, c), 1)
    eye = jnp.where(ti == tj, 1.0, 0.0)
    decay = [jnp.exp(jnp.where(ti >= tj, gcs_[h] - grs_[h], NEG)) for h in heads]
    qs = [qs_[h] * (DN_DK ** -0.5) for h in heads]
    kk = [_mm_nt(ks_[h], ks_[h]) for h in heads]
    nm = [jnp.where(ti > tj, betas_[h] * kk[h] * decay[h], 0.0) for h in heads]
    t = [eye - nm[h] for h in heads]
    p = nm
    step = 2
    while step < c:
        p = [_mm(p[h], p[h]) for h in heads]
        t = [t[h] + _mm(t[h], p[h]) for h in heads]
        step *= 2
    eg = [jnp.exp(gcs_[h]) for h in heads]
    rhs = [jnp.concatenate([betas_[h] * vs_[h], (betas_[h] * eg[h]) * ks_[h]], axis=1) for h in heads]
    sol = [_mm(t[h], rhs[h]) for h in heads]
    resid = [rhs[h] - (sol[h] + _mm_split(nm[h], sol[h])) for h in heads]
    sol = [sol[h] + _mm(t[h], resid[h]) for h in heads]
    qk = [_mm_nt(qs[h], ks_[h]) * decay[h] for h in heads]
    u = [sol[h][:, :DN_DV] - _mm(sol[h][:, DN_DV:], ss_[h]) for h in heads]
    o = [_mm(qs[h] * eg[h], ss_[h]) + _mm(qk[h], u[h]) for h in heads]
    gl = [gcs_[h][c - 1:c] for h in heads]
    s_new = [ss_[h] * jnp.exp(gl[h]) + _mm_tn(ks_[h] * jnp.exp(gl[h] - gcs_[h]), u[h]) for h in heads]
    return o, s_new


def _gla_chunks(qs, ks, gs, vss, ss, m):
    probs = range(len(qs))
    c = qs[0].shape[0]
    nh = len(vss[0])
    heads = range(nh)
    dk = LANES // nh
    dlane = lax.broadcasted_iota(I32, (1, LANES), 1)
    if nh == 1:
        mask = lambda x, h: x
    else:
        hm = [jnp.where(jnp.logical_and(dlane >= h * dk, dlane < (h + 1) * dk), 1.0, 0.0) for h in heads]
        mask = lambda x, h: x * hm[h]
    gs = [gs[i] * LOG2E for i in probs]
    qeg = [qs[i] * jnp.exp2(gs[i]) for i in probs]
    o = [[_mm(mask(qeg[i], h), ss[i]) for h in heads] for i in probs]
    lane = lax.broadcasted_iota(I32, (m, c), 1)
    sub = lax.broadcasted_iota(I32, (m, c), 0)
    blocks = [[[] for _ in heads] for _ in probs]
    for blk in range(c // m):
        r0 = blk * m
        qb = [qs[i][r0:r0 + m] for i in probs]
        gb = [gs[i][r0:r0 + m] for i in probs]
        kb = [ks[i][r0:r0 + m] for i in probs]
        if blk > 0:
            base = [gs[i][r0 - 1:r0] for i in probs]
            kt = [jnp.concatenate([ks[i][:r0] * jnp.exp2(base[i] - gs[i][:r0]), jnp.zeros((c - r0, LANES), F32)], axis=0)
                  for i in probs]
            qt = [qb[i] * jnp.exp2(gb[i] - base[i]) for i in probs]
            att = [[_mm_nt(mask(qt[i], h), kt[i]) for h in heads] for i in probs]
        else:
            att = [[jnp.zeros((m, c), F32) for _ in heads] for _ in probs]
        for j in range(m):
            keep = jnp.logical_and(lane == r0 + j, sub >= j)
            for i in probs:
                prod = qb[i] * (kb[i][j:j + 1] * jnp.exp2(gb[i] - gb[i][j:j + 1]))
                for h in heads:
                    att[i][h] = jnp.where(keep, jnp.sum(mask(prod, h), axis=1, keepdims=True), att[i][h])
        for i in probs:
            for h in heads:
                blocks[i][h].append(att[i][h])
    s_new = []
    for i in probs:
        gl = gs[i][c - 1:c]
        kg = ks[i] * jnp.exp2(gl - gs[i])
        s_i = ss[i] * _row_to_col(jnp.exp2(gl))
        for h in heads:
            att = blocks[i][h][0] if len(blocks[i][h]) == 1 else jnp.concatenate(blocks[i][h], axis=0)
            o[i][h] = o[i][h] + _mm(att, vss[i][h])
            s_i = s_i + _mm_tn(mask(kg, h), vss[i][h])
        s_new.append(s_i)
    return o, s_new


def _tri_incl(c):
    r = lax.broadcasted_iota(I32, (c, c), 0)
    cc = lax.broadcasted_iota(I32, (c, c), 1)
    return jnp.where(r >= cc, 1.0, 0.0).astype(BF16)


def _norm_proj_kernel(x_ref, gain_ref, w_ref, p_ref):
    h = _rms(x_ref[...], gain_ref[...])
    p_ref[...] = jnp.dot(h.astype(BF16), w_ref[...], preferred_element_type=F32)


def _norm_proj(x, gain, w):
    n, d = x.shape
    m = w.shape[1]
    tm = TOKEN_TILE
    return pl.pallas_call(
        _norm_proj_kernel,
        grid=(n // tm,),
        in_specs=[pl.BlockSpec((tm, d), lambda i: (i, 0)), pl.BlockSpec((1, d), lambda i: (0, 0)),
                  pl.BlockSpec((d, m), lambda i: (0, 0))],
        out_specs=pl.BlockSpec((tm, m), lambda i: (i, 0)),
        out_shape=jax.ShapeDtypeStruct((n, m), F32),
        compiler_params=_params(("parallel",)),
        name="norm_proj",
    )(x, gain, w)


def _combine_norm_kernel(pos_ref, *refs, project, tiles_a):
    if project:
        x_ref, info_ref, gain_ref, w_ref, ys_hbm, xs_ref, out_ref, ybuf, sem = refs
    else:
        x_ref, info_ref, gain_ref, ys_hbm, outa_ref, outb_ref, ybuf, sem = refs
    i = pl.program_id(0)
    n = pl.num_programs(0)
    tm = x_ref.shape[0]
    slot = i % 2

    def gather_start(tile, s):
        def body(r, carry):
            for k in range(TOP_K):
                src = pl.multiple_of(pos_ref[TOP_K * (tile * tm + r) + k] * ROW_TILE, ROW_TILE)
                dst = pl.multiple_of(r * ROW_TILE, ROW_TILE)
                pltpu.make_async_copy(ys_hbm.at[pl.ds(src, ROW_TILE)], ybuf.at[s, k, pl.ds(dst, ROW_TILE)],
                                      sem.at[s]).start(priority=k % 2)
            return carry
        lax.fori_loop(0, tm, body, 0, unroll=8)

    @pl.when(i == 0)
    def _():
        gather_start(0, 0)

    @pl.when(i + 1 < n)
    def _():
        gather_start(i + 1, 1 - slot)

    for k in range(TOP_K):
        pltpu.make_async_copy(ys_hbm.at[pl.ds(0, tm * ROW_TILE)], ybuf.at[slot, k], sem.at[slot]).wait()
    info = info_ref[...]
    y0, y1 = (_load_token_tiles(ybuf.at[slot, k]) for k in range(TOP_K))
    x = x_ref[...] + (info[:, INFO_W:INFO_W + 1] * y0 + info[:, INFO_W + 1:INFO_W + 2] * y1)
    h = _rms(x, gain_ref[...])
    if project:
        xs_ref[...] = x
        out_ref[...] = jnp.dot(h.astype(BF16), w_ref[...], preferred_element_type=F32)
    else:
        @pl.when(i < tiles_a)
        def _():
            outa_ref[...] = h

        @pl.when(i >= tiles_a)
        def _():
            outb_ref[...] = h


def _combine_norm(x, info, pos, ys, gain, w, n_a=None):
    n, d = x.shape
    tm = TOKEN_TILE
    project = w is not None
    tiles_a = None if project else n_a // tm
    row = lambda width: pl.BlockSpec((tm, width), lambda i, pos: (i, 0))
    in_specs = [row(d), row(LANES), pl.BlockSpec((1, d), lambda i, pos: (0, 0))]
    args = [x, info, gain]
    if project:
        m = w.shape[1]
        in_specs.append(pl.BlockSpec((d, m), lambda i, pos: (0, 0)))
        args.append(w)
        out_specs = [row(d), row(m)]
        out_shape = [jax.ShapeDtypeStruct((n, d), F32), jax.ShapeDtypeStruct((n, m), F32)]
    else:
        out_specs = [pl.BlockSpec((tm, d), lambda i, pos: (jnp.minimum(i, tiles_a - 1), 0)),
                     pl.BlockSpec((tm, d), lambda i, pos: (jnp.maximum(i - tiles_a, 0), 0))]
        out_shape = [jax.ShapeDtypeStruct((n_a, d), F32), jax.ShapeDtypeStruct((n - n_a, d), F32)]
    in_specs.append(pl.BlockSpec(memory_space=pl.ANY))
    args.append(ys)
    grid_spec = pltpu.PrefetchScalarGridSpec(
        num_scalar_prefetch=1, grid=(n // tm,), in_specs=in_specs, out_specs=out_specs,
        scratch_shapes=[pltpu.VMEM((2, TOP_K, tm * ROW_TILE, LANES), F32), pltpu.SemaphoreType.DMA((2,))])
    return pl.pallas_call(
        functools.partial(_combine_norm_kernel, project=project, tiles_a=tiles_a),
        grid_spec=grid_spec,
        out_shape=out_shape,
        compiler_params=_params(("arbitrary",)),
        name="combine_norm",
    )(pos, *args)


def _ab_mixer_kernel(*refs, c, nseq, zero_init):
    p_refs, refs = refs[:nseq], refs[nseq:]
    if zero_init:
        cw_ref, prm_ref, gkw_ref, gkb_ref, o_ref, conv_ref, sdn_ref, sgla_ref = refs
    else:
        (cw_ref, prm_ref, gkw_ref, gkb_ref, conv_in, sdn_in, sgla_in,
         o_ref, conv_ref, sdn_ref, sgla_ref) = refs

    @pl.when(pl.program_id(1) == 0)
    def _():
        if zero_init:
            conv_ref[...] = jnp.zeros_like(conv_ref)
            sdn_ref[...] = jnp.zeros_like(sdn_ref)
            sgla_ref[...] = jnp.zeros_like(sgla_ref)
        else:
            conv_ref[...] = conv_in[...]
            sdn_ref[...] = sdn_in[...]
            sgla_ref[...] = sgla_in[...]

    m = min(GLA_SUB_BLOCK, c)
    tri = _tri_incl(c)
    cw = cw_ref[...]
    a_log, dt_bias = prm_ref[0:1], prm_ref[1:2]
    dn_gain, gla_gain = prm_ref[2:3], prm_ref[3:4]
    n_pairs = GLA_HEADS // GLA_PACK
    conv_old = [conv_ref[b] for b in range(nseq)]
    sdn_old = [[sdn_ref[b, h] for h in range(DN_HEADS)] for b in range(nseq)]
    sgla_old = [[sgla_ref[b, pr] for pr in range(n_pairs)] for b in range(nseq)]
    dn = {key: [] for key in ("q", "k", "v", "gc", "gr", "beta", "s")}
    miscs = []
    for b in range(nseq):
        p_ref = p_refs[b]
        u = p_ref[:, AB_U:AB_U + CONV_CH]
        ucat = jnp.concatenate([conv_old[b], u], axis=0)
        acc = u * cw[CONV_W - 1:CONV_W]
        for j in range(1, CONV_W):
            acc = acc + pltpu.roll(ucat, j, 0)[SUBLANES:SUBLANES + c] * cw[CONV_W - 1 - j:CONV_W - j]
        conv_ref[b] = ucat[c:c + SUBLANES]
        qkv = acc * _sigmoid(acc)

        misc = p_ref[:, AB_MISC:AB_MISC + LANES]
        miscs.append(misc)
        g_all = -jnp.exp(a_log) * _softplus(misc + dt_bias)
        beta_all = _sigmoid(misc)
        gcum = _mm_cum(tri, g_all)
        gcum_t = jnp.concatenate([gcum, jnp.zeros((LANES - c, LANES), F32)], axis=0).T
        for h in range(DN_HEADS):
            lo = h * DN_DK
            q = qkv[:, lo:lo + DN_DK]
            k = qkv[:, DN_QK + lo:DN_QK + lo + DN_DK]
            dn["q"].append(q * lax.rsqrt(jnp.sum(q * q, axis=-1, keepdims=True) + 1e-6))
            dn["k"].append(k * lax.rsqrt(jnp.sum(k * k, axis=-1, keepdims=True) + 1e-6))
            dn["v"].append(qkv[:, 2 * DN_QK + h * DN_DV:2 * DN_QK + (h + 1) * DN_DV])
            dn["gc"].append(gcum[:, MISC_A + h:MISC_A + h + 1])
            dn["gr"].append(gcum_t[MISC_A + h:MISC_A + h + 1, :c])
            dn["beta"].append(beta_all[:, MISC_B + h:MISC_B + h + 1])
            dn["s"].append(sdn_old[b][h])
    os_, ss_ = _delta_chunks(dn["q"], dn["k"], dn["v"], dn["gc"], dn["gr"], dn["beta"], dn["s"])
    for b in range(nseq):
        for h in range(DN_HEADS):
            i = b * DN_HEADS + h
            sdn_ref[b, h] = ss_[i]
            z = p_refs[b][:, AB_Z + h * DN_DV:AB_Z + (h + 1) * DN_DV]
            o_ref[b, :, h * DN_DV:(h + 1) * DN_DV] = _gated_rms(os_[i], z, dn_gain).astype(o_ref.dtype)

    gl = {key: [] for key in ("q", "k", "g", "v", "s")}
    for b in range(nseq):
        p_ref = p_refs[b]
        logits = _mm_split(miscs[b], gkw_ref[...]) + gkb_ref[...]
        logf = -_softplus(-logits) * (1.0 / GLA_NORMALIZER)
        gall = _mm_cum(tri, logf)
        for pair in range(n_pairs):
            lo = pair * LANES
            gl["q"].append(p_ref[:, AB_GQ + lo:AB_GQ + lo + LANES] * (GLA_DK ** -0.5))
            gl["k"].append(p_ref[:, AB_GK + lo:AB_GK + lo + LANES])
            gl["g"].append(gall[:, lo:lo + LANES])
            gl["v"].append([p_ref[:, AB_GV + h * GLA_DV:AB_GV + (h + 1) * GLA_DV]
                            for h in range(pair * GLA_PACK, (pair + 1) * GLA_PACK)])
            gl["s"].append(sgla_old[b][pair])
    os_, ss_ = _gla_chunks(gl["q"], gl["k"], gl["g"], gl["v"], gl["s"], m)
    for b in range(nseq):
        for pair in range(n_pairs):
            i = b * n_pairs + pair
            sgla_ref[b, pair] = ss_[i]
            for hh in range(GLA_PACK):
                h = pair * GLA_PACK + hh
                gate = p_refs[b][:, AB_GG + h * GLA_DV:AB_GG + (h + 1) * GLA_DV]
                o_ref[b, :, DN_V + h * GLA_DV:DN_V + (h + 1) * GLA_DV] = \
                    _gated_rms(os_[i][hh], gate, gla_gain).astype(o_ref.dtype)


def _seq_specs(cols, row0, c, nchunk, nseq):
    blk0 = row0 // c
    return [pl.BlockSpec((c, cols), lambda g, i, s=s: (blk0 + (g * nseq + s) * nchunk + i, 0)) for s in range(nseq)]


def _ab_mixer(p, row0, nb, t, nseq, cw, prm, gkw, gkb, states):
    c = min(CHUNK, t)
    nchunk = t // c
    n_groups = nb // nseq
    zero_init = states is None
    st_map3 = lambda g, i: (g, 0, 0)
    st_map4 = lambda g, i: (g, 0, 0, 0)
    const = lambda shape: pl.BlockSpec(shape, lambda g, i: (0,) * len(shape))
    in_specs = _seq_specs(AB_COLS, row0, c, nchunk, nseq) + [const(cw.shape), const(prm.shape), const(gkw.shape),
                                                             const(gkb.shape)]
    args = [p] * nseq + [cw, prm, gkw, gkb]
    gla_packed = (GLA_HEADS // GLA_PACK, GLA_PACK * GLA_DK, GLA_DV)
    st_specs = [pl.BlockSpec((nseq, SUBLANES, CONV_CH), st_map3),
                pl.BlockSpec((nseq, DN_HEADS, DN_DK, DN_DV), st_map4),
                pl.BlockSpec((nseq,) + gla_packed, st_map4)]
    if not zero_init:
        conv8, sdn, sgla = states
        in_specs += st_specs
        args += [conv8, sdn, sgla.reshape((nb,) + gla_packed)]
    out_shape = [jax.ShapeDtypeStruct((nb, t, D_MODEL), BF16),
                 jax.ShapeDtypeStruct((nb, SUBLANES, CONV_CH), F32),
                 jax.ShapeDtypeStruct((nb, DN_HEADS, DN_DK, DN_DV), F32),
                 jax.ShapeDtypeStruct((nb,) + gla_packed, F32)]
    out_specs = [pl.BlockSpec((nseq, c, D_MODEL), lambda g, i: (g, i, 0))] + st_specs
    o, conv_out, sdn_out, sgla_out = pl.pallas_call(
        functools.partial(_ab_mixer_kernel, c=c, nseq=nseq, zero_init=zero_init),
        grid=(n_groups, nchunk),
        in_specs=in_specs,
        out_specs=out_specs,
        out_shape=out_shape,
        compiler_params=_params(("parallel", "arbitrary")),
        name="ab_mixer",
    )(*args)
    return o.reshape(nb * t, D_MODEL), conv_out, sdn_out, sgla_out.reshape(nb, GLA_HEADS, GLA_DK, GLA_DV)


def _hgrn_mixer_kernel(*refs, c, nseq, zero_init, layer):
    p_refs, refs = refs[:nseq], refs[nseq:]
    if zero_init:
        lb_ref, gain_ref, o_ref, s_ref = refs
    else:
        lb_ref, gain_ref, s_in, o_ref, s_ref = refs

    @pl.when(pl.program_id(1) == 0)
    def _():
        if zero_init:
            s_ref[...] = jnp.zeros_like(s_ref)
        else:
            s_ref[...] = s_in[...]

    raw = lb_ref[...]
    e = jnp.exp(raw - jnp.max(raw, axis=0, keepdims=True))
    sm = e / jnp.sum(e, axis=0, keepdims=True)
    cum = sm[0:1]
    for l in range(1, layer + 1):
        cum = cum + sm[l:l + 1]
    lb = cum - sm[0:1]

    m = min(GLA_SUB_BLOCK, c)
    tri = _tri_incl(c)
    gain = gain_ref[...]
    pr = {key: [] for key in ("q", "k", "g", "v", "s")}
    for b in range(nseq):
        p_ref = p_refs[b]
        fr = p_ref[:, HG_F:2 * HG_F]
        logf = jnp.log(lb + (1.0 - lb) * _sigmoid(fr))
        kall = (1.0 - lb) * _sigmoid(-fr)
        gall = _mm_cum(tri, logf)
        for h in range(HG_HEADS):
            lo = h * HG_DK
            pr["q"].append(p_ref[:, lo:lo + HG_DK] * (HG_DK ** -0.5))
            pr["k"].append(kall[:, lo:lo + HG_DK])
            pr["g"].append(gall[:, lo:lo + HG_DK])
            pr["v"].append([p_ref[:, 2 * HG_F + h * HG_DV:2 * HG_F + (h + 1) * HG_DV]])
            pr["s"].append(s_ref[b, h])
    os_, ss_ = _gla_chunks(pr["q"], pr["k"], pr["g"], pr["v"], pr["s"], m)
    for b in range(nseq):
        for h in range(HG_HEADS):
            i = b * HG_HEADS + h
            s_ref[b, h] = ss_[i]
            gate = p_refs[b][:, 2 * HG_F + D_MODEL + h * HG_DV:2 * HG_F + D_MODEL + (h + 1) * HG_DV]
            o_ref[b, :, h * HG_DV:(h + 1) * HG_DV] = _gated_rms(os_[i][0], gate, gain).astype(o_ref.dtype)


def _hgrn_mixer(p, row0, nb, t, nseq, lb, gain, state, layer):
    c = min(CHUNK, t)
    nchunk = t // c
    zero_init = state is None
    const = lambda shape: pl.BlockSpec(shape, lambda g, i: (0,) * len(shape))
    st_spec = pl.BlockSpec((nseq, HG_HEADS, HG_DK, HG_DV), lambda g, i: (g, 0, 0, 0))
    in_specs = _seq_specs(p.shape[1], row0, c, nchunk, nseq) + [const(lb.shape), const(gain.shape)]
    args = [p] * nseq + [lb, gain]
    if not zero_init:
        in_specs.append(st_spec)
        args.append(state)
    o, s_out = pl.pallas_call(
        functools.partial(_hgrn_mixer_kernel, c=c, nseq=nseq, zero_init=zero_init, layer=layer),
        grid=(nb // nseq, nchunk),
        in_specs=in_specs,
        out_specs=[pl.BlockSpec((nseq, c, D_MODEL), lambda g, i: (g, i, 0)), st_spec],
        out_shape=[jax.ShapeDtypeStruct((nb, t, D_MODEL), BF16),
                   jax.ShapeDtypeStruct((nb, HG_HEADS, HG_DK, HG_DV), F32)],
        compiler_params=_params(("parallel", "arbitrary")),
        name="hgrn_mixer",
    )(*args)
    return o.reshape(nb * t, D_MODEL), s_out


def _outproj_router_kernel(*refs, tiles_a, split_x):
    if split_x:
        oa_ref, ob_ref, xa_ref, xb_ref = refs[:4]
        refs = refs[4:]
    else:
        oa_ref, ob_ref, x_ref = refs[:3]
        refs = refs[3:]
    wout_ref, gain_ref, wr_ref, br_ref, x1_ref, h_ref, info_ref, cnt_ref, run_ref = refs
    first = pl.program_id(0) < tiles_a

    @pl.when(pl.program_id(0) == 0)
    def _():
        run_ref[...] = jnp.zeros_like(run_ref)

    o = jnp.where(first, oa_ref[...], ob_ref[...])
    x = jnp.where(first, xa_ref[...], xb_ref[...]) if split_x else x_ref[...]
    x1 = x + jnp.dot(o, wout_ref[...], preferred_element_type=F32)
    x1_ref[...] = x1
    h = _rms(x1, gain_ref[...])
    _store_token_tiles(h_ref, h)
    logit = _mm_hi(h, wr_ref[...]) + br_ref[...]
    lane = lax.broadcasted_iota(I32, logit.shape, 1)
    far = jnp.int32(LANES)
    lg = jnp.where(lane < N_GROUPS, logit, NEG)
    mg = jnp.max(lg, axis=-1, keepdims=True)
    p_g = 1.0 / jnp.sum(jnp.exp(lg - mg), axis=-1, keepdims=True)
    g_top = jnp.min(jnp.where(lg == mg, lane, far), axis=-1, keepdims=True)
    lo = N_GROUPS + g_top * EXPERTS_PER_GROUP
    sel = jnp.logical_and(lane >= lo, lane < lo + EXPERTS_PER_GROUP)
    le = jnp.where(sel, logit, NEG)
    ee = jnp.exp(le - jnp.max(le, axis=-1, keepdims=True))
    pe = jnp.where(sel, ee / jnp.sum(ee, axis=-1, keepdims=True), -1.0)
    p1 = jnp.max(pe, axis=-1, keepdims=True)
    i1 = jnp.min(jnp.where(pe == p1, lane, far), axis=-1, keepdims=True)
    pe2 = jnp.where(lane == i1, -1.0, pe)
    p2 = jnp.max(pe2, axis=-1, keepdims=True)
    i2 = jnp.min(jnp.where(pe2 == p2, lane, far), axis=-1, keepdims=True)
    denom = p1 + p2
    w1 = p_g * (p1 / denom)
    w2 = p_g * (p2 / denom)
    tm = logit.shape[0]
    oh1 = jnp.where(lane == i1, 1.0, 0.0)
    oh2 = jnp.where(lane == i2, 1.0, 0.0)
    both = oh1 + oh2
    tr = lax.broadcasted_iota(I32, (tm, tm), 0)
    tc = lax.broadcasted_iota(I32, (tm, tm), 1)
    before = run_ref[...] + _mm(jnp.where(tr > tc, 1.0, 0.0), both)
    r1 = jnp.sum(oh1 * before, axis=-1, keepdims=True)
    r2 = jnp.sum(oh2 * before, axis=-1, keepdims=True)
    run = run_ref[...] + jnp.sum(both, axis=0, keepdims=True)
    run_ref[...] = run
    cnt_ref[...] = jnp.broadcast_to(run, cnt_ref.shape)
    vals = ((i1 - N_GROUPS).astype(F32), (i2 - N_GROUPS).astype(F32), w1, w2, r1, r2)
    info = jnp.zeros_like(logit)
    for idx, val in enumerate(vals):
        info = jnp.where(lane == idx, val, info)
    info_ref[...] = info


def _outproj_router(o_a, o_b, xs, wout, gain, wr, br):
    n_a, d = o_a.shape
    n = n_a + o_b.shape[0]
    tm = TOKEN_TILE
    tiles_a = n_a // tm
    row = lambda w: pl.BlockSpec((tm, w), lambda i: (i, 0))
    seg_a = pl.BlockSpec((tm, d), lambda i: (jnp.minimum(i, tiles_a - 1), 0))
    seg_b = pl.BlockSpec((tm, d), lambda i: (jnp.maximum(i - tiles_a, 0), 0))
    const = lambda shape: pl.BlockSpec(shape, lambda i: (0,) * len(shape))
    split_x = len(xs) == 2
    return pl.pallas_call(
        functools.partial(_outproj_router_kernel, tiles_a=tiles_a, split_x=split_x),
        grid=(n // tm,),
        in_specs=[seg_a, seg_b] + ([seg_a, seg_b] if split_x else [row(d)])
        + [const((d, d)), const((1, d)), const((d, LANES)), const((1, LANES))],
        out_specs=[row(d), pl.BlockSpec((tm * ROW_TILE, LANES), lambda i: (i, 0)), row(LANES), const((SUBLANES, LANES))],
        out_shape=[jax.ShapeDtypeStruct((n, d), F32), jax.ShapeDtypeStruct((n * ROW_TILE, LANES), F32),
                   jax.ShapeDtypeStruct((n, LANES), F32), jax.ShapeDtypeStruct((SUBLANES, LANES), F32)],
        scratch_shapes=[pltpu.VMEM((1, LANES), F32)],
        compiler_params=_params(("arbitrary",)),
        name="outproj_router",
    )(o_a, o_b, *xs, wout, gain, wr, br)


def _dispatch_kernel(pos_ref, lo_ref, hi_ref, h_ref, xs_hbm, buf, zrow, sem, zsem):
    i = pl.program_id(0)
    n = pl.num_programs(0)
    tm = h_ref.shape[0] // ROW_TILE
    blk = tm * ROW_TILE
    slot = i % 2
    token = lambda r: pl.ds(pl.multiple_of(r * ROW_TILE, ROW_TILE), ROW_TILE)

    def tile_wait(s):
        for _ in range(TOP_K):
            pltpu.make_async_copy(buf.at[s], xs_hbm.at[pl.ds(0, blk)], sem.at[s]).wait()

    zblk = zrow.shape[0]

    def pad_rows(fn, tile_fn):
        for e in range(N_EXPERTS):
            lax.fori_loop(lo_ref[e], hi_ref[e], fn, 0)
        lax.fori_loop(hi_ref[N_EXPERTS - 1] // (zblk // ROW_TILE), xs_hbm.shape[0] // zblk, tile_fn, 0)

    def pad_start(r, carry):
        pltpu.make_async_copy(zrow.at[token(0)], xs_hbm.at[token(r)], zsem.at[0]).start()
        return carry

    def pad_wait(r, carry):
        pltpu.make_async_copy(zrow.at[token(0)], xs_hbm.at[token(0)], zsem.at[0]).wait()
        return carry

    def pad_tile_start(t, carry):
        pltpu.make_async_copy(zrow, xs_hbm.at[pl.ds(pl.multiple_of(t * zblk, zblk), zblk)], zsem.at[0]).start()
        return carry

    def pad_tile_wait(t, carry):
        pltpu.make_async_copy(zrow, xs_hbm.at[pl.ds(0, zblk)], zsem.at[0]).wait()
        return carry

    @pl.when(i == 0)
    def _():
        zrow[...] = jnp.zeros_like(zrow)
        pad_rows(pad_start, pad_tile_start)

    @pl.when(i >= 2)
    def _():
        tile_wait(slot)
    buf[slot] = h_ref[...]

    def body(r, carry):
        for k in range(TOP_K):
            row = pos_ref[TOP_K * (i * tm + r) + k]
            pltpu.make_async_copy(buf.at[slot, token(r)], xs_hbm.at[token(row)], sem.at[slot]).start(priority=k % 2)
        return carry
    lax.fori_loop(0, tm, body, 0, unroll=8)

    @pl.when(i == n - 1)
    def _():
        tile_wait(slot)

        @pl.when(n > 1)
        def _():
            tile_wait(1 - slot)
        pad_rows(pad_wait, pad_tile_wait)


def _dispatch(h, pos, pad_lo, pad_hi, n_rows):
    blk = TOKEN_TILE * ROW_TILE
    n = h.shape[0] // ROW_TILE
    grid_spec = pltpu.PrefetchScalarGridSpec(
        num_scalar_prefetch=3,
        grid=(n // TOKEN_TILE,),
        in_specs=[pl.BlockSpec((blk, LANES), lambda i, pos, lo, hi: (i, 0))],
        out_specs=pl.BlockSpec(memory_space=pl.ANY),
        scratch_shapes=[pltpu.VMEM((2, blk, LANES), F32), pltpu.VMEM((MOE_TILE * ROW_TILE, LANES), F32),
                        pltpu.SemaphoreType.DMA((2,)), pltpu.SemaphoreType.DMA((1,))],
    )
    return pl.pallas_call(
        _dispatch_kernel,
        grid_spec=grid_spec,
        out_shape=jax.ShapeDtypeStruct((n_rows * ROW_TILE, LANES), F32),
        compiler_params=_params(("arbitrary",)),
        name="moe_dispatch",
    )(pos, pad_lo, pad_hi, h)


def _moe_kernel(te_ref, nact_ref, x_ref, w1_ref, w3_ref, w2_ref, y_ref, w1b, w3b, w2b):
    i = pl.program_id(0)

    @pl.when(i < nact_ref[0])
    def _():
        @pl.when(jnp.logical_or(i == 0, te_ref[i] != te_ref[jnp.maximum(i - 1, 0)]))
        def _():
            w1b[...] = w1_ref[0, 0].astype(BF16)
            w3b[...] = w3_ref[0, 0].astype(BF16)
            w2b[...] = w2_ref[0, 0].astype(BF16)

        xb = _load_token_tiles(x_ref).astype(BF16)
        h1 = jnp.dot(xb, w1b[...], preferred_element_type=F32)
        h3 = jnp.dot(xb, w3b[...], preferred_element_type=F32)
        act = (h1 * _sigmoid(h1)) * h3
        _store_token_tiles(y_ref, jnp.dot(act.astype(BF16), w2b[...], preferred_element_type=F32))

    @pl.when(i >= nact_ref[0])
    def _():
        y_ref[...] = jnp.zeros_like(y_ref)


def _moe(xs, te, nact, w1, w3, w2, layer):
    d, f = w1.shape[2], w1.shape[3]
    tm = MOE_TILE
    blk = tm * ROW_TILE
    n_tiles = xs.shape[0] // blk
    last = lambda i, nact: jnp.maximum(jnp.minimum(i, nact[0] - 1), 0)
    grid_spec = pltpu.PrefetchScalarGridSpec(
        num_scalar_prefetch=2,
        grid=(n_tiles,),
        in_specs=[
            pl.BlockSpec((blk, LANES), lambda i, te, nact: (last(i, nact), 0)),
            pl.BlockSpec((1, 1, d, f), lambda i, te, nact: (layer, te[i], 0, 0)),
            pl.BlockSpec((1, 1, d, f), lambda i, te, nact: (layer, te[i], 0, 0)),
            pl.BlockSpec((1, 1, f, d), lambda i, te, nact: (layer, te[i], 0, 0)),
        ],
        out_specs=pl.BlockSpec((blk, LANES), lambda i, te, nact: (i, 0)),
        scratch_shapes=[pltpu.VMEM((d, f), BF16), pltpu.VMEM((d, f), BF16), pltpu.VMEM((f, d), BF16)],
    )
    return pl.pallas_call(
        _moe_kernel,
        grid_spec=grid_spec,
        out_shape=jax.ShapeDtypeStruct(xs.shape, F32),
        compiler_params=_params(("arbitrary",)),
        name="moe_experts",
    )(te, nact, xs, w1, w3, w2)


def _moe_rows(n):
    return ((TOP_K * n + N_EXPERTS * (MOE_TILE - 1)) // MOE_TILE) * MOE_TILE


def _route_tables(info, counts_row, n):
    tm = MOE_TILE
    n_tiles = _moe_rows(n) // tm
    ids = jnp.arange(N_EXPERTS, dtype=I32)
    counts = counts_row[0, N_GROUPS:N_GROUPS + N_EXPERTS].astype(I32)
    padded = ((counts + tm - 1) // tm) * tm
    cum = jnp.cumsum(padded)
    off = cum - padded
    e = info[:, INFO_E:INFO_E + TOP_K].astype(I32)
    rank = info[:, INFO_R:INFO_R + TOP_K].astype(I32)
    pos = (jnp.sum(jnp.where(e[:, :, None] == ids, off, 0), axis=-1) + rank).reshape(TOP_K * n)
    tile_start = jnp.arange(n_tiles, dtype=I32) * tm
    nact = cum[-1] // tm
    te_raw = jnp.sum((cum[None, :] <= tile_start[:, None]).astype(I32), axis=1)
    te_last = jnp.sum((cum <= (nact - 1) * tm).astype(I32))
    te = jnp.where(tile_start < cum[-1], jnp.minimum(te_raw, N_EXPERTS - 1), te_last)
    return pos, off + counts, cum, te, nact.reshape(1)


def _pad_lanes(a, width=LANES):
    return jnp.pad(a, [(0, 0)] * (a.ndim - 1) + [(0, width - a.shape[-1])])


def _seq_group(nb, want):
    g = want
    while nb % g:
        g //= 2
    return g


PROMPT_SEQS_PER_STEP = 4
SAMPLE_SEQS_PER_STEP = 8


def kernel(x_prompt, x_sample, state_conv_ab, state_delta_ab, state_gla_ab, state_hgrn_c, mix_norm, ab_w_in, ab_conv_w, ab_a_log, ab_dt_bias, ab_dn_norm, ab_gk_w2, ab_gk_b, ab_gla_norm, ab_w_out, c_w_in, c_lower_bounds, c_norm, c_w_out, ffn_norm, moe_w_group, moe_b_group, moe_w_expert, moe_b_expert, moe_w1, moe_w3, moe_w2, final_norm):
    bp, tp, d = x_prompt.shape
    bs, ts, _ = x_sample.shape
    n_p, n_s = bp * tp, bs * ts
    n = n_p + n_s
    depth = mix_norm.shape[0]
    xs_res = (x_prompt.reshape(n_p, d), x_sample.reshape(n_s, d))
    pg, sg = _seq_group(bp, PROMPT_SEQS_PER_STEP), _seq_group(bs, SAMPLE_SEQS_PER_STEP)

    info = pos = ys = None
    conv_p, delta_p, gla_p, hgrn_p = [], [], [], []
    conv_s, delta_s, gla_s, hgrn_s = [], [], [], []
    for layer in range(depth):
        j = layer // 2
        gain = mix_norm[layer].reshape(1, d)
        if layer % 2 == 0:
            u_w, a_w, b_w, z_w, gq_w, gk_w, gv_w, lr_w, gg_w = jnp.split(ab_w_in[j], _split_points(AB_SPLITS), axis=1)
            w_in = jnp.concatenate([u_w, z_w, gq_w, gk_w, gv_w, gg_w, _pad_lanes(jnp.concatenate([a_w, b_w, lr_w], axis=1))],
                                   axis=1).astype(BF16)
        else:
            w_in = c_w_in[j].astype(BF16)
        if ys is None:
            p_p, p_s = (_norm_proj(seg, gain, w_in) for seg in xs_res)
            off_s = 0
        else:
            x, p_p = _combine_norm(xs_res[0], info, pos, ys, gain, w_in)
            xs_res = (x,)
            p_s, off_s = p_p, n_p

        if layer % 2 == 0:
            prm = jnp.concatenate([_pad_lanes(ab_a_log[j].reshape(1, -1)), _pad_lanes(ab_dt_bias[j].reshape(1, -1)),
                                   ab_dn_norm[j].reshape(1, -1), ab_gla_norm[j].reshape(1, -1),
                                   jnp.zeros((4, LANES), F32)], axis=0)
            gkw = jnp.zeros((LANES, GLA_K), F32).at[MISC_LR:MISC_LR + GLA_RANK].set(ab_gk_w2[j])
            gkb = ab_gk_b[j].reshape(1, GLA_K)
            cw = ab_conv_w[j]
            o_p, c8, sd, sgl = _ab_mixer(p_p, 0, bp, tp, pg, cw, prm, gkw, gkb, None)
            conv_p.append(c8[:, SUBLANES - (CONV_W - 1):])
            delta_p.append(sd)
            gla_p.append(sgl)
            conv8 = jnp.pad(state_conv_ab[j], ((0, 0), (SUBLANES - (CONV_W - 1), 0), (0, 0)))
            o_s, c8, sd, sgl = _ab_mixer(p_s, off_s, bs, ts, sg, cw, prm, gkw, gkb,
                                         (conv8, state_delta_ab[j], state_gla_ab[j]))
            conv_s.append(c8[:, SUBLANES - (CONV_W - 1):])
            delta_s.append(sd)
            gla_s.append(sgl)
            w_out = ab_w_out[j]
        else:
            hg_gain = c_norm[j].reshape(1, -1)
            o_p, sh = _hgrn_mixer(p_p, 0, bp, tp, pg, c_lower_bounds, hg_gain, None, layer)
            hgrn_p.append(sh)
            o_s, sh = _hgrn_mixer(p_s, off_s, bs, ts, sg, c_lower_bounds, hg_gain, state_hgrn_c[j], layer)
            hgrn_s.append(sh)
            w_out = c_w_out[j]

        wr = _pad_lanes(jnp.concatenate([moe_w_group[layer], moe_w_expert[layer]], axis=1))
        br = _pad_lanes(jnp.concatenate([moe_b_group[layer], moe_b_expert[layer]]).reshape(1, -1))
        x, hn, info, counts = _outproj_router(o_p, o_s, xs_res, w_out.astype(BF16), ffn_norm[layer].reshape(1, d), wr, br)
        xs_res = (x,)
        pos, pad_lo, pad_hi, te, nact = _route_tables(info, counts, n)
        xsort = _dispatch(hn, pos, pad_lo, pad_hi, _moe_rows(n))
        ys = _moe(xsort, te, nact, moe_w1, moe_w3, moe_w2, layer)

    y_p, y_s = _combine_norm(xs_res[0], info, pos, ys, final_norm.reshape(1, d), None, n_a=n_p)
    y_prompt = y_p.reshape(bp, tp, d)
    y_sample = y_s.reshape(bs, ts, d)
    return (y_prompt, y_sample, jnp.stack(conv_p), jnp.stack(delta_p), jnp.stack(gla_p), jnp.stack(hgrn_p),
            jnp.stack(conv_s), jnp.stack(delta_s), jnp.stack(gla_s), jnp.stack(hgrn_s))


def _split_points(sizes):
    pts, acc = [], 0
    for s in sizes[:-1]:
        acc += s
        pts.append(acc)
    return pts
```

```python
import functools

import jax
import jax.numpy as jnp
from jax import lax
from jax.experimental import pallas as pl
from jax.experimental.pallas import tpu as pltpu

F32 = jnp.float32
BF16 = jnp.bfloat16
I32 = jnp.int32

D_MODEL = 1024
EPS = 1e-6
CHUNK = 64
CONV_W = 4
DN_HEADS, DN_DK, DN_DV = 4, 128, 128
DN_QK = DN_HEADS * DN_DK
DN_V = DN_HEADS * DN_DV
CONV_CH = 2 * DN_QK + DN_V
GLA_HEADS, GLA_DK, GLA_DV = 4, 64, 128
GLA_K = GLA_HEADS * GLA_DK
GLA_V = GLA_HEADS * GLA_DV
GLA_RANK = 16
GLA_NORMALIZER = 16.0
GLA_PACK = 128 // GLA_DK
AB_SPLITS = (CONV_CH, DN_HEADS, DN_HEADS, DN_V, GLA_K, GLA_K, GLA_V, GLA_RANK, GLA_V)
HG_HEADS, HG_DK, HG_DV = 8, 128, 128
HG_F = HG_HEADS * HG_DK
N_GROUPS, EXPERTS_PER_GROUP = 4, 8
N_EXPERTS = N_GROUPS * EXPERTS_PER_GROUP
TOP_K = 2
D_EXPERT = 512

LANES = 128
SUBLANES = 8
VMEM_LIMIT_BYTES = 56 * 1024 * 1024

AB_U, AB_Z, AB_GQ, AB_GK, AB_GV, AB_GG, AB_MISC = 0, 1536, 2048, 2304, 2560, 3072, 3584
AB_COLS = AB_MISC + LANES
MISC_A, MISC_B, MISC_LR = 0, DN_HEADS, 2 * DN_HEADS

INFO_E, INFO_W, INFO_R = 0, TOP_K, 2 * TOP_K

TOKEN_TILE = 512
MOE_TILE = 256
NEG = -1e30
LOG2E = 1.4426950408889634
GLA_SUB_BLOCK = SUBLANES


def _params(sem):
    return pltpu.CompilerParams(dimension_semantics=sem, vmem_limit_bytes=VMEM_LIMIT_BYTES)


def _mm(a, b):
    return jnp.dot(a.astype(BF16), b.astype(BF16), preferred_element_type=F32)


def _mm_nt(a, b):
    return lax.dot_general(a.astype(BF16), b.astype(BF16), (((1,), (1,)), ((), ())), preferred_element_type=F32)


def _mm_tn(a, b):
    return lax.dot_general(a.astype(BF16), b.astype(BF16), (((0,), (0,)), ((), ())), preferred_element_type=F32)


def _mm_hi(a, b):
    return jnp.dot(a, b, preferred_element_type=F32, precision=lax.Precision.HIGHEST)


def _sigmoid(x):
    return 1.0 / (1.0 + jnp.exp(-x))


def _softplus(x):
    return jnp.maximum(x, 0.0) + jnp.log(1.0 + jnp.exp(-jnp.abs(x)))


def _rms(x, gain):
    return x * lax.rsqrt(jnp.mean(x * x, axis=-1, keepdims=True) + EPS) * gain


def _gated_rms(o, gate, gain):
    return _rms(o, gain) * (gate * _sigmoid(gate))


ROW_TILE = D_MODEL // LANES
assert ROW_TILE == SUBLANES


def _store_token_tiles(ref, x):
    tm = x.shape[0]
    for j in range(ROW_TILE):
        ref[pl.ds(j, tm, stride=ROW_TILE), :] = x[:, j * LANES:(j + 1) * LANES]


def _load_token_tiles(ref):
    tm = ref.shape[0] // ROW_TILE
    return jnp.concatenate([ref[pl.ds(j, tm, stride=ROW_TILE), :] for j in range(ROW_TILE)], axis=1)


def _row_to_col(row):
    n = row.shape[1]
    r = lax.broadcasted_iota(I32, (n, n), 0)
    c = lax.broadcasted_iota(I32, (n, n), 1)
    return jnp.sum(jnp.where(r == c, jnp.broadcast_to(row, (n, n)), 0.0), axis=1, keepdims=True)


def _split2(x):
    hi = x.astype(BF16)
    return hi, (x - hi.astype(F32)).astype(BF16)


def _split3(x):
    hi = x.astype(BF16)
    r = x - hi.astype(F32)
    mid = r.astype(BF16)
    return hi, mid, (r - mid.astype(F32)).astype(BF16)


def _mm_cum(tri, x):
    hi, mid, lo = _split3(x)
    dot = lambda p: jnp.dot(tri, p, preferred_element_type=F32)
    return dot(hi) + (dot(mid) + dot(lo))


def _mm_split(a, b):
    ah, al = _split2(a)
    bh, bl = _split2(b)
    dot = lambda x, y: jnp.dot(x, y, preferred_element_type=F32)
    return dot(ah, bh) + (dot(ah, bl) + dot(al, bh))


def _delta_chunks(qs_, ks_, vs_, gcs_, grs_, betas_, ss_):
    heads = range(len(qs_))
    c = qs_[0].shape[0]
    ti = lax.broadcasted_iota(I32, (c, c), 0)
    tj = lax.broadcasted_iota(I32, (c, c), 1)
    eye = jnp.where(ti == tj, 1.0, 0.0)
    decay = [jnp.exp(jnp.where(ti >= tj, gcs_[h] - grs_[h], NEG)) for h in heads]
    qs = [qs_[h] * (DN_DK ** -0.5) for h in heads]
    kk = [_mm_nt(ks_[h], ks_[h]) for h in heads]
    nm = [jnp.where(ti > tj, betas_[h] * kk[h] * decay[h], 0.0) for h in heads]
    t = [eye - nm[h] for h in heads]
    p = nm
    step = 2
    while step < c:
        p = [_mm(p[h], p[h]) for h in heads]
        t = [t[h] + _mm(t[h], p[h]) for h in heads]
        step *= 2
    eg = [jnp.exp(gcs_[h]) for h in heads]
    rhs = [jnp.concatenate([betas_[h] * vs_[h], (betas_[h] * eg[h]) * ks_[h]], axis=1) for h in heads]
    sol = [_mm(t[h], rhs[h]) for h in heads]
    resid = [rhs[h] - (sol[h] + _mm_split(nm[h], sol[h])) for h in heads]
    sol = [sol[h] + _mm(t[h], resid[h]) for h in heads]
    qk = [_mm_nt(qs[h], ks_[h]) * decay[h] for h in heads]
    u = [sol[h][:, :DN_DV] - _mm(sol[h][:, DN_DV:], ss_[h]) for h in heads]
    o = [_mm(qs[h] * eg[h], ss_[h]) + _mm(qk[h], u[h]) for h in heads]
    gl = [gcs_[h][c - 1:c] for h in heads]
    s_new = [ss_[h] * jnp.exp(gl[h]) + _mm_tn(ks_[h] * jnp.exp(gl[h] - gcs_[h]), u[h]) for h in heads]
    return o, s_new


def _gla_chunks(qs, ks, gs, vss, ss, m):
    probs = range(len(qs))
    c = qs[0].shape[0]
    nh = len(vss[0])
    heads = range(nh)
    dk = LANES // nh
    dlane = lax.broadcasted_iota(I32, (1, LANES), 1)
    if nh == 1:
        mask = lambda x, h: x
    else:
        hm = [jnp.where(jnp.logical_and(dlane >= h * dk, dlane < (h + 1) * dk), 1.0, 0.0) for h in heads]
        mask = lambda x, h: x * hm[h]
    gs = [gs[i] * LOG2E for i in probs]
    qeg = [qs[i] * jnp.exp2(gs[i]) for i in probs]
    o = [[_mm(mask(qeg[i], h), ss[i]) for h in heads] for i in probs]
    lane = lax.broadcasted_iota(I32, (m, c), 1)
    sub = lax.broadcasted_iota(I32, (m, c), 0)
    blocks = [[[] for _ in heads] for _ in probs]
    for blk in range(c // m):
        r0 = blk * m
        qb = [qs[i][r0:r0 + m] for i in probs]
        gb = [gs[i][r0:r0 + m] for i in probs]
        kb = [ks[i][r0:r0 + m] for i in probs]
        if blk > 0:
            base = [gs[i][r0 - 1:r0] for i in probs]
            kt = [jnp.concatenate([ks[i][:r0] * jnp.exp2(base[i] - gs[i][:r0]), jnp.zeros((c - r0, LANES), F32)], axis=0)
                  for i in probs]
            qt = [qb[i] * jnp.exp2(gb[i] - base[i]) for i in probs]
            att = [[_mm_nt(mask(qt[i], h), kt[i]) for h in heads] for i in probs]
        else:
            att = [[jnp.zeros((m, c), F32) for _ in heads] for _ in probs]
        for j in range(m):
            keep = jnp.logical_and(lane == r0 + j, sub >= j)
            for i in probs:
                prod = qb[i] * (kb[i][j:j + 1] * jnp.exp2(gb[i] - gb[i][j:j + 1]))
                for h in heads:
                    att[i][h] = jnp.where(keep, jnp.sum(mask(prod, h), axis=1, keepdims=True), att[i][h])
        for i in probs:
            for h in heads:
                blocks[i][h].append(att[i][h])
    s_new = []
    for i in probs:
        gl = gs[i][c - 1:c]
        kg = ks[i] * jnp.exp2(gl - gs[i])
        s_i = ss[i] * _row_to_col(jnp.exp2(gl))
        for h in heads:
            att = blocks[i][h][0] if len(blocks[i][h]) == 1 else jnp.concatenate(blocks[i][h], axis=0)
            o[i][h] = o[i][h] + _mm(att, vss[i][h])
            s_i = s_i + _mm_tn(mask(kg, h), vss[i][h])
        s_new.append(s_i)
    return o, s_new


def _tri_incl(c):
    r = lax.broadcasted_iota(I32, (c, c), 0)
    cc = lax.broadcasted_iota(I32, (c, c), 1)
    return jnp.where(r >= cc, 1.0, 0.0).astype(BF16)


def _combine_kernel(pos_ref, *refs, final, tiles_a):
    if final:
        x_ref, info_ref, gain_ref, ys_hbm, outa_ref, outb_ref, ybuf, sem = refs
    else:
        x_ref, info_ref, ys_hbm, out_ref, ybuf, sem = refs
    i = pl.program_id(0)
    n = pl.num_programs(0)
    tm = x_ref.shape[0]
    slot = i % 2

    def gather_start(tile, s):
        def body(r, carry):
            for k in range(TOP_K):
                src = pl.multiple_of(pos_ref[TOP_K * (tile * tm + r) + k] * ROW_TILE, ROW_TILE)
                dst = pl.multiple_of(r * ROW_TILE, ROW_TILE)
                pltpu.make_async_copy(ys_hbm.at[pl.ds(src, ROW_TILE)], ybuf.at[s, k, pl.ds(dst, ROW_TILE)],
                                      sem.at[s]).start(priority=k % 2)
            return carry
        lax.fori_loop(0, tm, body, 0, unroll=8)

    @pl.when(i == 0)
    def _():
        gather_start(0, 0)

    @pl.when(i + 1 < n)
    def _():
        gather_start(i + 1, 1 - slot)

    for k in range(TOP_K):
        pltpu.make_async_copy(ys_hbm.at[pl.ds(0, tm * ROW_TILE)], ybuf.at[slot, k], sem.at[slot]).wait()
    info = info_ref[...]
    y0, y1 = (_load_token_tiles(ybuf.at[slot, k]) for k in range(TOP_K))
    x = x_ref[...] + (info[:, INFO_W:INFO_W + 1] * y0 + info[:, INFO_W + 1:INFO_W + 2] * y1)
    if final:
        h = _rms(x, gain_ref[...])

        @pl.when(i < tiles_a)
        def _():
            outa_ref[...] = h

        @pl.when(i >= tiles_a)
        def _():
            outb_ref[...] = h
    else:
        out_ref[...] = x


def _combine(x, info, pos, ys, final_gain=None, n_a=None):
    n, d = x.shape
    tm = TOKEN_TILE
    final = final_gain is not None
    tiles_a = n_a // tm if final else None
    row = lambda width: pl.BlockSpec((tm, width), lambda i, pos: (i, 0))
    in_specs = [row(d), row(LANES)]
    args = [x, info]
    if final:
        in_specs.append(pl.BlockSpec((1, d), lambda i, pos: (0, 0)))
        args.append(final_gain)
        out_specs = [pl.BlockSpec((tm, d), lambda i, pos: (jnp.minimum(i, tiles_a - 1), 0)),
                     pl.BlockSpec((tm, d), lambda i, pos: (jnp.maximum(i - tiles_a, 0), 0))]
        out_shape = [jax.ShapeDtypeStruct((n_a, d), F32), jax.ShapeDtypeStruct((n - n_a, d), F32)]
    else:
        out_specs = row(d)
        out_shape = jax.ShapeDtypeStruct((n, d), F32)
    in_specs.append(pl.BlockSpec(memory_space=pl.ANY))
    args.append(ys)
    grid_spec = pltpu.PrefetchScalarGridSpec(
        num_scalar_prefetch=1, grid=(n // tm,), in_specs=in_specs, out_specs=out_specs,
        scratch_shapes=[pltpu.VMEM((2, TOP_K, tm * ROW_TILE, LANES), F32), pltpu.SemaphoreType.DMA((2,))])
    return pl.pallas_call(
        functools.partial(_combine_kernel, final=final, tiles_a=tiles_a),
        grid_spec=grid_spec,
        out_shape=out_shape,
        compiler_params=_params(("arbitrary",)),
        name="moe_combine",
    )(pos, *args)


def _project_rows(x_refs, gain_ref, w_ref, p_scr, c):
    x = jnp.concatenate([r[...] for r in x_refs], axis=0) if len(x_refs) > 1 else x_refs[0][...]
    p_scr[...] = jnp.dot(_rms(x, gain_ref[...]).astype(BF16), w_ref[...], preferred_element_type=F32)
    return [p_scr.at[pl.ds(b * c, c)] for b in range(len(x_refs))]


def _ab_mixer_kernel(*refs, c, nseq, zero_init):
    x_refs, refs = refs[:nseq], refs[nseq:]
    if zero_init:
        gain_ref, win_ref, cw_ref, prm_ref, gkw_ref, gkb_ref, o_ref, conv_ref, sdn_ref, sgla_ref, p_scr = refs
    else:
        (gain_ref, win_ref, cw_ref, prm_ref, gkw_ref, gkb_ref, conv_in, sdn_in, sgla_in,
         o_ref, conv_ref, sdn_ref, sgla_ref, p_scr) = refs

    @pl.when(pl.program_id(1) == 0)
    def _():
        if zero_init:
            conv_ref[...] = jnp.zeros_like(conv_ref)
            sdn_ref[...] = jnp.zeros_like(sdn_ref)
            sgla_ref[...] = jnp.zeros_like(sgla_ref)
        else:
            conv_ref[...] = conv_in[...]
            sdn_ref[...] = sdn_in[...]
            sgla_ref[...] = sgla_in[...]

    p_refs = _project_rows(x_refs, gain_ref, win_ref, p_scr, c)

    m = min(GLA_SUB_BLOCK, c)
    tri = _tri_incl(c)
    cw = cw_ref[...]
    a_log, dt_bias = prm_ref[0:1], prm_ref[1:2]
    dn_gain, gla_gain = prm_ref[2:3], prm_ref[3:4]
    n_pairs = GLA_HEADS // GLA_PACK
    conv_old = [conv_ref[b] for b in range(nseq)]
    sdn_old = [[sdn_ref[b, h] for h in range(DN_HEADS)] for b in range(nseq)]
    sgla_old = [[sgla_ref[b, pr] for pr in range(n_pairs)] for b in range(nseq)]
    dn = {key: [] for key in ("q", "k", "v", "gc", "gr", "beta", "s")}
    miscs = []
    for b in range(nseq):
        p_ref = p_refs[b]
        u = p_ref[:, AB_U:AB_U + CONV_CH]
        ucat = jnp.concatenate([conv_old[b], u], axis=0)
        acc = u * cw[CONV_W - 1:CONV_W]
        for j in range(1, CONV_W):
            acc = acc + pltpu.roll(ucat, j, 0)[SUBLANES:SUBLANES + c] * cw[CONV_W - 1 - j:CONV_W - j]
        conv_ref[b] = ucat[c:c + SUBLANES]
        qkv = acc * _sigmoid(acc)

        misc = p_ref[:, AB_MISC:AB_MISC + LANES]
        miscs.append(misc)
        g_all = -jnp.exp(a_log) * _softplus(misc + dt_bias)
        beta_all = _sigmoid(misc)
        gcum = _mm_cum(tri, g_all)
        gcum_t = jnp.concatenate([gcum, jnp.zeros((LANES - c, LANES), F32)], axis=0).T
        for h in range(DN_HEADS):
            lo = h * DN_DK
            q = qkv[:, lo:lo + DN_DK]
            k = qkv[:, DN_QK + lo:DN_QK + lo + DN_DK]
            dn["q"].append(q * lax.rsqrt(jnp.sum(q * q, axis=-1, keepdims=True) + 1e-6))
            dn["k"].append(k * lax.rsqrt(jnp.sum(k * k, axis=-1, keepdims=True) + 1e-6))
            dn["v"].append(qkv[:, 2 * DN_QK + h * DN_DV:2 * DN_QK + (h + 1) * DN_DV])
            dn["gc"].append(gcum[:, MISC_A + h:MISC_A + h + 1])
            dn["gr"].append(gcum_t[MISC_A + h:MISC_A + h + 1, :c])
            dn["beta"].append(beta_all[:, MISC_B + h:MISC_B + h + 1])
            dn["s"].append(sdn_old[b][h])
    os_, ss_ = _delta_chunks(dn["q"], dn["k"], dn["v"], dn["gc"], dn["gr"], dn["beta"], dn["s"])
    for b in range(nseq):
        for h in range(DN_HEADS):
            i = b * DN_HEADS + h
            sdn_ref[b, h] = ss_[i]
            z = p_refs[b][:, AB_Z + h * DN_DV:AB_Z + (h + 1) * DN_DV]
            o_ref[b, :, h * DN_DV:(h + 1) * DN_DV] = _gated_rms(os_[i], z, dn_gain).astype(o_ref.dtype)

    gl = {key: [] for key in ("q", "k", "g", "v", "s")}
    for b in range(nseq):
        p_ref = p_refs[b]
        logits = _mm_split(miscs[b], gkw_ref[...]) + gkb_ref[...]
        logf = -_softplus(-logits) * (1.0 / GLA_NORMALIZER)
        gall = _mm_cum(tri, logf)
        for pair in range(n_pairs):
            lo = pair * LANES
            gl["q"].append(p_ref[:, AB_GQ + lo:AB_GQ + lo + LANES] * (GLA_DK ** -0.5))
            gl["k"].append(p_ref[:, AB_GK + lo:AB_GK + lo + LANES])
            gl["g"].append(gall[:, lo:lo + LANES])
            gl["v"].append([p_ref[:, AB_GV + h * GLA_DV:AB_GV + (h + 1) * GLA_DV]
                            for h in range(pair * GLA_PACK, (pair + 1) * GLA_PACK)])
            gl["s"].append(sgla_old[b][pair])
    os_, ss_ = _gla_chunks(gl["q"], gl["k"], gl["g"], gl["v"], gl["s"], m)
    for b in range(nseq):
        for pair in range(n_pairs):
            i = b * n_pairs + pair
            sgla_ref[b, pair] = ss_[i]
            for hh in range(GLA_PACK):
                h = pair * GLA_PACK + hh
                gate = p_refs[b][:, AB_GG + h * GLA_DV:AB_GG + (h + 1) * GLA_DV]
                o_ref[b, :, DN_V + h * GLA_DV:DN_V + (h + 1) * GLA_DV] = \
                    _gated_rms(os_[i][hh], gate, gla_gain).astype(o_ref.dtype)


def _seq_specs(cols, row0, c, nchunk, nseq):
    blk0 = row0 // c
    return [pl.BlockSpec((c, cols), lambda g, i, s=s: (blk0 + (g * nseq + s) * nchunk + i, 0)) for s in range(nseq)]


def _ab_mixer(x, row0, nb, t, nseq, gain, w_in, cw, prm, gkw, gkb, states):
    c = min(CHUNK, t)
    nchunk = t // c
    n_groups = nb // nseq
    zero_init = states is None
    st_map3 = lambda g, i: (g, 0, 0)
    st_map4 = lambda g, i: (g, 0, 0, 0)
    const = lambda shape: pl.BlockSpec(shape, lambda g, i: (0,) * len(shape))
    consts = [gain, w_in, cw, prm, gkw, gkb]
    in_specs = _seq_specs(D_MODEL, row0, c, nchunk, nseq) + [const(a.shape) for a in consts]
    args = [x] * nseq + consts
    gla_packed = (GLA_HEADS // GLA_PACK, GLA_PACK * GLA_DK, GLA_DV)
    st_specs = [pl.BlockSpec((nseq, SUBLANES, CONV_CH), st_map3),
                pl.BlockSpec((nseq, DN_HEADS, DN_DK, DN_DV), st_map4),
                pl.BlockSpec((nseq,) + gla_packed, st_map4)]
    if not zero_init:
        conv8, sdn, sgla = states
        in_specs += st_specs
        args += [conv8, sdn, sgla.reshape((nb,) + gla_packed)]
    out_shape = [jax.ShapeDtypeStruct((nb, t, D_MODEL), BF16),
                 jax.ShapeDtypeStruct((nb, SUBLANES, CONV_CH), F32),
                 jax.ShapeDtypeStruct((nb, DN_HEADS, DN_DK, DN_DV), F32),
                 jax.ShapeDtypeStruct((nb,) + gla_packed, F32)]
    out_specs = [pl.BlockSpec((nseq, c, D_MODEL), lambda g, i: (g, i, 0))] + st_specs
    o, conv_out, sdn_out, sgla_out = pl.pallas_call(
        functools.partial(_ab_mixer_kernel, c=c, nseq=nseq, zero_init=zero_init),
        grid=(n_groups, nchunk),
        in_specs=in_specs,
        out_specs=out_specs,
        out_shape=out_shape,
        scratch_shapes=[pltpu.VMEM((nseq * c, AB_COLS), F32)],
        compiler_params=_params(("parallel", "arbitrary")),
        name="ab_mixer",
    )(*args)
    return o.reshape(nb * t, D_MODEL), conv_out, sdn_out, sgla_out.reshape(nb, GLA_HEADS, GLA_DK, GLA_DV)


def _hgrn_mixer_kernel(*refs, c, nseq, zero_init, layer):
    x_refs, refs = refs[:nseq], refs[nseq:]
    if zero_init:
        mixgain_ref, win_ref, lb_ref, gain_ref, o_ref, s_ref, p_scr = refs
    else:
        mixgain_ref, win_ref, lb_ref, gain_ref, s_in, o_ref, s_ref, p_scr = refs
    p_refs = _project_rows(x_refs, mixgain_ref, win_ref, p_scr, c)

    @pl.when(pl.program_id(1) == 0)
    def _():
        if zero_init:
            s_ref[...] = jnp.zeros_like(s_ref)
        else:
            s_ref[...] = s_in[...]

    raw = lb_ref[...]
    e = jnp.exp(raw - jnp.max(raw, axis=0, keepdims=True))
    sm = e / jnp.sum(e, axis=0, keepdims=True)
    cum = sm[0:1]
    for l in range(1, layer + 1):
        cum = cum + sm[l:l + 1]
    lb = cum - sm[0:1]

    m = min(GLA_SUB_BLOCK, c)
    tri = _tri_incl(c)
    gain = gain_ref[...]
    pr = {key: [] for key in ("q", "k", "g", "v", "s")}
    for b in range(nseq):
        p_ref = p_refs[b]
        fr = p_ref[:, HG_F:2 * HG_F]
        logf = jnp.log(lb + (1.0 - lb) * _sigmoid(fr))
        kall = (1.0 - lb) * _sigmoid(-fr)
        gall = _mm_cum(tri, logf)
        for h in range(HG_HEADS):
            lo = h * HG_DK
            pr["q"].append(p_ref[:, lo:lo + HG_DK] * (HG_DK ** -0.5))
            pr["k"].append(kall[:, lo:lo + HG_DK])
            pr["g"].append(gall[:, lo:lo + HG_DK])
            pr["v"].append([p_ref[:, 2 * HG_F + h * HG_DV:2 * HG_F + (h + 1) * HG_DV]])
            pr["s"].append(s_ref[b, h])
    os_, ss_ = _gla_chunks(pr["q"], pr["k"], pr["g"], pr["v"], pr["s"], m)
    for b in range(nseq):
        for h in range(HG_HEADS):
            i = b * HG_HEADS + h
            s_ref[b, h] = ss_[i]
            gate = p_refs[b][:, 2 * HG_F + D_MODEL + h * HG_DV:2 * HG_F + D_MODEL + (h + 1) * HG_DV]
            o_ref[b, :, h * HG_DV:(h + 1) * HG_DV] = _gated_rms(os_[i][0], gate, gain).astype(o_ref.dtype)


def _hgrn_mixer(x, row0, nb, t, nseq, mix_gain, w_in, lb, gain, state, layer):
    c = min(CHUNK, t)
    nchunk = t // c
    zero_init = state is None
    const = lambda shape: pl.BlockSpec(shape, lambda g, i: (0,) * len(shape))
    st_spec = pl.BlockSpec((nseq, HG_HEADS, HG_DK, HG_DV), lambda g, i: (g, 0, 0, 0))
    consts = [mix_gain, w_in, lb, gain]
    in_specs = _seq_specs(D_MODEL, row0, c, nchunk, nseq) + [const(a.shape) for a in consts]
    args = [x] * nseq + consts
    if not zero_init:
        in_specs.append(st_spec)
        args.append(state)
    o, s_out = pl.pallas_call(
        functools.partial(_hgrn_mixer_kernel, c=c, nseq=nseq, zero_init=zero_init, layer=layer),
        grid=(nb // nseq, nchunk),
        in_specs=in_specs,
        out_specs=[pl.BlockSpec((nseq, c, D_MODEL), lambda g, i: (g, i, 0)), st_spec],
        out_shape=[jax.ShapeDtypeStruct((nb, t, D_MODEL), BF16),
                   jax.ShapeDtypeStruct((nb, HG_HEADS, HG_DK, HG_DV), F32)],
        scratch_shapes=[pltpu.VMEM((nseq * c, w_in.shape[1]), F32)],
        compiler_params=_params(("parallel", "arbitrary")),
        name="hgrn_mixer",
    )(*args)
    return o.reshape(nb * t, D_MODEL), s_out


def _outproj_router_kernel(*refs, tiles_a, split_x):
    if split_x:
        oa_ref, ob_ref, xa_ref, xb_ref = refs[:4]
        refs = refs[4:]
    else:
        oa_ref, ob_ref, x_ref = refs[:3]
        refs = refs[3:]
    wout_ref, gain_ref, wr_ref, br_ref, x1_ref, h_ref, info_ref, cnt_ref, run_ref = refs
    first = pl.program_id(0) < tiles_a

    @pl.when(pl.program_id(0) == 0)
    def _():
        run_ref[...] = jnp.zeros_like(run_ref)

    o = jnp.where(first, oa_ref[...], ob_ref[...])
    x = jnp.where(first, xa_ref[...], xb_ref[...]) if split_x else x_ref[...]
    x1 = x + jnp.dot(o, wout_ref[...], preferred_element_type=F32)
    x1_ref[...] = x1
    h = _rms(x1, gain_ref[...])
    _store_token_tiles(h_ref, h)
    logit = _mm_hi(h, wr_ref[...]) + br_ref[...]
    lane = lax.broadcasted_iota(I32, logit.shape, 1)
    far = jnp.int32(LANES)
    lg = jnp.where(lane < N_GROUPS, logit, NEG)
    mg = jnp.max(lg, axis=-1, keepdims=True)
    p_g = 1.0 / jnp.sum(jnp.exp(lg - mg), axis=-1, keepdims=True)
    g_top = jnp.min(jnp.where(lg == mg, lane, far), axis=-1, keepdims=True)
    lo = N_GROUPS + g_top * EXPERTS_PER_GROUP
    sel = jnp.logical_and(lane >= lo, lane < lo + EXPERTS_PER_GROUP)
    le = jnp.where(sel, logit, NEG)
    ee = jnp.exp(le - jnp.max(le, axis=-1, keepdims=True))
    pe = jnp.where(sel, ee / jnp.sum(ee, axis=-1, keepdims=True), -1.0)
    p1 = jnp.max(pe, axis=-1, keepdims=True)
    i1 = jnp.min(jnp.where(pe == p1, lane, far), axis=-1, keepdims=True)
    pe2 = jnp.where(lane == i1, -1.0, pe)
    p2 = jnp.max(pe2, axis=-1, keepdims=True)
    i2 = jnp.min(jnp.where(pe2 == p2, lane, far), axis=-1, keepdims=True)
    denom = p1 + p2
    w1 = p_g * (p1 / denom)
    w2 = p_g * (p2 / denom)
    tm = logit.shape[0]
    oh1 = jnp.where(lane == i1, 1.0, 0.0)
    oh2 = jnp.where(lane == i2, 1.0, 0.0)
    both = oh1 + oh2
    tr = lax.broadcasted_iota(I32, (tm, tm), 0)
    tc = lax.broadcasted_iota(I32, (tm, tm), 1)
    before = run_ref[...] + _mm(jnp.where(tr > tc, 1.0, 0.0), both)
    r1 = jnp.sum(oh1 * before, axis=-1, keepdims=True)
    r2 = jnp.sum(oh2 * before, axis=-1, keepdims=True)
    run = run_ref[...] + jnp.sum(both, axis=0, keepdims=True)
    run_ref[...] = run
    cnt_ref[...] = jnp.broadcast_to(run, cnt_ref.shape)
    vals = ((i1 - N_GROUPS).astype(F32), (i2 - N_GROUPS).astype(F32), w1, w2, r1, r2)
    info = jnp.zeros_like(logit)
    for idx, val in enumerate(vals):
        info = jnp.where(lane == idx, val, info)
    info_ref[...] = info


def _outproj_router(o_a, o_b, xs, wout, gain, wr, br):
    n_a, d = o_a.shape
    n = n_a + o_b.shape[0]
    tm = TOKEN_TILE
    tiles_a = n_a // tm
    row = lambda w: pl.BlockSpec((tm, w), lambda i: (i, 0))
    seg_a = pl.BlockSpec((tm, d), lambda i: (jnp.minimum(i, tiles_a - 1), 0))
    seg_b = pl.BlockSpec((tm, d), lambda i: (jnp.maximum(i - tiles_a, 0), 0))
    const = lambda shape: pl.BlockSpec(shape, lambda i: (0,) * len(shape))
    split_x = len(xs) == 2
    return pl.pallas_call(
        functools.partial(_outproj_router_kernel, tiles_a=tiles_a, split_x=split_x),
        grid=(n // tm,),
        in_specs=[seg_a, seg_b] + ([seg_a, seg_b] if split_x else [row(d)])
        + [const((d, d)), const((1, d)), const((d, LANES)), const((1, LANES))],
        out_specs=[row(d), pl.BlockSpec((tm * ROW_TILE, LANES), lambda i: (i, 0)), row(LANES), const((SUBLANES, LANES))],
        out_shape=[jax.ShapeDtypeStruct((n, d), F32), jax.ShapeDtypeStruct((n * ROW_TILE, LANES), F32),
                   jax.ShapeDtypeStruct((n, LANES), F32), jax.ShapeDtypeStruct((SUBLANES, LANES), F32)],
        scratch_shapes=[pltpu.VMEM((1, LANES), F32)],
        compiler_params=_params(("arbitrary",)),
        name="outproj_router",
    )(o_a, o_b, *xs, wout, gain, wr, br)


def _dispatch_kernel(pos_ref, lo_ref, hi_ref, h_ref, xs_hbm, buf, zrow, sem, zsem):
    i = pl.program_id(0)
    n = pl.num_programs(0)
    tm = h_ref.shape[0] // ROW_TILE
    blk = tm * ROW_TILE
    slot = i % 2
    token = lambda r: pl.ds(pl.multiple_of(r * ROW_TILE, ROW_TILE), ROW_TILE)

    def tile_wait(s):
        for _ in range(TOP_K):
            pltpu.make_async_copy(buf.at[s], xs_hbm.at[pl.ds(0, blk)], sem.at[s]).wait()

    zblk = zrow.shape[0]

    def pad_rows(fn, tile_fn):
        for e in range(N_EXPERTS):
            lax.fori_loop(lo_ref[e], hi_ref[e], fn, 0)
        lax.fori_loop(hi_ref[N_EXPERTS - 1] // (zblk // ROW_TILE), xs_hbm.shape[0] // zblk, tile_fn, 0)

    def pad_start(r, carry):
        pltpu.make_async_copy(zrow.at[token(0)], xs_hbm.at[token(r)], zsem.at[0]).start()
        return carry

    def pad_wait(r, carry):
        pltpu.make_async_copy(zrow.at[token(0)], xs_hbm.at[token(0)], zsem.at[0]).wait()
        return carry

    def pad_tile_start(t, carry):
        pltpu.make_async_copy(zrow, xs_hbm.at[pl.ds(pl.multiple_of(t * zblk, zblk), zblk)], zsem.at[0]).start()
        return carry

    def pad_tile_wait(t, carry):
        pltpu.make_async_copy(zrow, xs_hbm.at[pl.ds(0, zblk)], zsem.at[0]).wait()
        return carry

    @pl.when(i == 0)
    def _():
        zrow[...] = jnp.zeros_like(zrow)
        pad_rows(pad_start, pad_tile_start)

    @pl.when(i >= 2)
    def _():
        tile_wait(slot)
    buf[slot] = h_ref[...]

    def body(r, carry):
        for k in range(TOP_K):
            row = pos_ref[TOP_K * (i * tm + r) + k]
            pltpu.make_async_copy(buf.at[slot, token(r)], xs_hbm.at[token(row)], sem.at[slot]).start(priority=k % 2)
        return carry
    lax.fori_loop(0, tm, body, 0, unroll=8)

    @pl.when(i == n - 1)
    def _():
        tile_wait(slot)

        @pl.when(n > 1)
        def _():
            tile_wait(1 - slot)
        pad_rows(pad_wait, pad_tile_wait)


def _dispatch(h, pos, pad_lo, pad_hi, n_rows):
    blk = TOKEN_TILE * ROW_TILE
    n = h.shape[0] // ROW_TILE
    grid_spec = pltpu.PrefetchScalarGridSpec(
        num_scalar_prefetch=3,
        grid=(n // TOKEN_TILE,),
        in_specs=[pl.BlockSpec((blk, LANES), lambda i, pos, lo, hi: (i, 0))],
        out_specs=pl.BlockSpec(memory_space=pl.ANY),
        scratch_shapes=[pltpu.VMEM((2, blk, LANES), F32), pltpu.VMEM((MOE_TILE * ROW_TILE, LANES), F32),
                        pltpu.SemaphoreType.DMA((2,)), pltpu.SemaphoreType.DMA((1,))],
    )
    return pl.pallas_call(
        _dispatch_kernel,
        grid_spec=grid_spec,
        out_shape=jax.ShapeDtypeStruct((n_rows * ROW_TILE, LANES), F32),
        compiler_params=_params(("arbitrary",)),
        name="moe_dispatch",
    )(pos, pad_lo, pad_hi, h)


def _moe_kernel(te_ref, nact_ref, x_ref, w1_ref, w3_ref, w2_ref, y_ref, w1b, w3b, w2b):
    i = pl.program_id(0)

    @pl.when(i < nact_ref[0])
    def _():
        @pl.when(jnp.logical_or(i == 0, te_ref[i] != te_ref[jnp.maximum(i - 1, 0)]))
        def _():
            w1b[...] = w1_ref[0, 0].astype(BF16)
            w3b[...] = w3_ref[0, 0].astype(BF16)
            w2b[...] = w2_ref[0, 0].astype(BF16)

        xb = _load_token_tiles(x_ref).astype(BF16)
        h1 = jnp.dot(xb, w1b[...], preferred_element_type=F32)
        h3 = jnp.dot(xb, w3b[...], preferred_element_type=F32)
        act = (h1 * _sigmoid(h1)) * h3
        _store_token_tiles(y_ref, jnp.dot(act.astype(BF16), w2b[...], preferred_element_type=F32))

    @pl.when(i >= nact_ref[0])
    def _():
        y_ref[...] = jnp.zeros_like(y_ref)


def _moe(xs, te, nact, w1, w3, w2, layer):
    d, f = w1.shape[2], w1.shape[3]
    tm = MOE_TILE
    blk = tm * ROW_TILE
    n_tiles = xs.shape[0] // blk
    last = lambda i, nact: jnp.maximum(jnp.minimum(i, nact[0] - 1), 0)
    grid_spec = pltpu.PrefetchScalarGridSpec(
        num_scalar_prefetch=2,
        grid=(n_tiles,),
        in_specs=[
            pl.BlockSpec((blk, LANES), lambda i, te, nact: (last(i, nact), 0)),
            pl.BlockSpec((1, 1, d, f), lambda i, te, nact: (layer, te[i], 0, 0)),
            pl.BlockSpec((1, 1, d, f), lambda i, te, nact: (layer, te[i], 0, 0)),
            pl.BlockSpec((1, 1, f, d), lambda i, te, nact: (layer, te[i], 0, 0)),
        ],
        out_specs=pl.BlockSpec((blk, LANES), lambda i, te, nact: (i, 0)),
        scratch_shapes=[pltpu.VMEM((d, f), BF16), pltpu.VMEM((d, f), BF16), pltpu.VMEM((f, d), BF16)],
    )
    return pl.pallas_call(
        _moe_kernel,
        grid_spec=grid_spec,
        out_shape=jax.ShapeDtypeStruct(xs.shape, F32),
        compiler_params=_params(("arbitrary",)),
        name="moe_experts",
    )(te, nact, xs, w1, w3, w2)


def _moe_rows(n):
    return ((TOP_K * n + N_EXPERTS * (MOE_TILE - 1)) // MOE_TILE) * MOE_TILE


def _route_tables(info, counts_row, n):
    tm = MOE_TILE
    n_tiles = _moe_rows(n) // tm
    ids = jnp.arange(N_EXPERTS, dtype=I32)
    counts = counts_row[0, N_GROUPS:N_GROUPS + N_EXPERTS].astype(I32)
    padded = ((counts + tm - 1) // tm) * tm
    cum = jnp.cumsum(padded)
    off = cum - padded
    e = info[:, INFO_E:INFO_E + TOP_K].astype(I32)
    rank = info[:, INFO_R:INFO_R + TOP_K].astype(I32)
    pos = (jnp.sum(jnp.where(e[:, :, None] == ids, off, 0), axis=-1) + rank).reshape(TOP_K * n)
    tile_start = jnp.arange(n_tiles, dtype=I32) * tm
    nact = cum[-1] // tm
    te_raw = jnp.sum((cum[None, :] <= tile_start[:, None]).astype(I32), axis=1)
    te_last = jnp.sum((cum <= (nact - 1) * tm).astype(I32))
    te = jnp.where(tile_start < cum[-1], jnp.minimum(te_raw, N_EXPERTS - 1), te_last)
    return pos, off + counts, cum, te, nact.reshape(1)


def _pad_lanes(a, width=LANES):
    return jnp.pad(a, [(0, 0)] * (a.ndim - 1) + [(0, width - a.shape[-1])])


def _seq_group(nb, want):
    g = want
    while nb % g:
        g //= 2
    return g


PROMPT_SEQS_PER_STEP = 4
SAMPLE_SEQS_PER_STEP = 8


def kernel(x_prompt, x_sample, state_conv_ab, state_delta_ab, state_gla_ab, state_hgrn_c, mix_norm, ab_w_in, ab_conv_w, ab_a_log, ab_dt_bias, ab_dn_norm, ab_gk_w2, ab_gk_b, ab_gla_norm, ab_w_out, c_w_in, c_lower_bounds, c_norm, c_w_out, ffn_norm, moe_w_group, moe_b_group, moe_w_expert, moe_b_expert, moe_w1, moe_w3, moe_w2, final_norm):
    bp, tp, d = x_prompt.shape
    bs, ts, _ = x_sample.shape
    n_p, n_s = bp * tp, bs * ts
    n = n_p + n_s
    depth = mix_norm.shape[0]
    xs_res = (x_prompt.reshape(n_p, d), x_sample.reshape(n_s, d))
    pg, sg = _seq_group(bp, PROMPT_SEQS_PER_STEP), _seq_group(bs, SAMPLE_SEQS_PER_STEP)

    info = pos = ys = None
    conv_p, delta_p, gla_p, hgrn_p = [], [], [], []
    conv_s, delta_s, gla_s, hgrn_s = [], [], [], []
    for layer in range(depth):
        j = layer // 2
        gain = mix_norm[layer].reshape(1, d)
        if layer % 2 == 0:
            u_w, a_w, b_w, z_w, gq_w, gk_w, gv_w, lr_w, gg_w = jnp.split(ab_w_in[j], _split_points(AB_SPLITS), axis=1)
            w_in = jnp.concatenate([u_w, z_w, gq_w, gk_w, gv_w, gg_w, _pad_lanes(jnp.concatenate([a_w, b_w, lr_w], axis=1))],
                                   axis=1).astype(BF16)
        else:
            w_in = c_w_in[j].astype(BF16)
        if ys is not None:
            xs_res = (_combine(xs_res[0], info, pos, ys),)
        (x_p, off_p), (x_s, off_s) = ((xs_res[0], 0), (xs_res[-1], 0 if len(xs_res) == 2 else n_p))

        if layer % 2 == 0:
            prm = jnp.concatenate([_pad_lanes(ab_a_log[j].reshape(1, -1)), _pad_lanes(ab_dt_bias[j].reshape(1, -1)),
                                   ab_dn_norm[j].reshape(1, -1), ab_gla_norm[j].reshape(1, -1),
                                   jnp.zeros((4, LANES), F32)], axis=0)
            gkw = jnp.zeros((LANES, GLA_K), F32).at[MISC_LR:MISC_LR + GLA_RANK].set(ab_gk_w2[j])
            gkb = ab_gk_b[j].reshape(1, GLA_K)
            cw = ab_conv_w[j]
            o_p, c8, sd, sgl = _ab_mixer(x_p, off_p, bp, tp, pg, gain, w_in, cw, prm, gkw, gkb, None)
            conv_p.append(c8[:, SUBLANES - (CONV_W - 1):])
            delta_p.append(sd)
            gla_p.append(sgl)
            conv8 = jnp.pad(state_conv_ab[j], ((0, 0), (SUBLANES - (CONV_W - 1), 0), (0, 0)))
            o_s, c8, sd, sgl = _ab_mixer(x_s, off_s, bs, ts, sg, gain, w_in, cw, prm, gkw, gkb,
                                         (conv8, state_delta_ab[j], state_gla_ab[j]))
            conv_s.append(c8[:, SUBLANES - (CONV_W - 1):])
            delta_s.append(sd)
            gla_s.append(sgl)
            w_out = ab_w_out[j]
        else:
            hg_gain = c_norm[j].reshape(1, -1)
            o_p, sh = _hgrn_mixer(x_p, off_p, bp, tp, pg, gain, w_in, c_lower_bounds, hg_gain, None, layer)
            hgrn_p.append(sh)
            o_s, sh = _hgrn_mixer(x_s, off_s, bs, ts, sg, gain, w_in, c_lower_bounds, hg_gain, state_hgrn_c[j], layer)
            hgrn_s.append(sh)
            w_out = c_w_out[j]

        wr = _pad_lanes(jnp.concatenate([moe_w_group[layer], moe_w_expert[layer]], axis=1))
        br = _pad_lanes(jnp.concatenate([moe_b_group[layer], moe_b_expert[layer]]).reshape(1, -1))
        x, hn, info, counts = _outproj_router(o_p, o_s, xs_res, w_out.astype(BF16), ffn_norm[layer].reshape(1, d), wr, br)
        xs_res = (x,)
        pos, pad_lo, pad_hi, te, nact = _route_tables(info, counts, n)
        xsort = _dispatch(hn, pos, pad_lo, pad_hi, _moe_rows(n))
        ys = _moe(xsort, te, nact, moe_w1, moe_w3, moe_w2, layer)

    y_p, y_s = _combine(xs_res[0], info, pos, ys, final_gain=final_norm.reshape(1, d), n_a=n_p)
    y_prompt = y_p.reshape(bp, tp, d)
    y_sample = y_s.reshape(bs, ts, d)
    return (y_prompt, y_sample, jnp.stack(conv_p), jnp.stack(delta_p), jnp.stack(gla_p), jnp.stack(hgrn_p),
            jnp.stack(conv_s), jnp.stack(delta_s), jnp.stack(gla_s), jnp.stack(hgrn_s))


def _split_points(sizes):
    pts, acc = [], 0
    for s in sizes[:-1]:
        acc += s
        pts.append(acc)
    return pts
```

```python
import functools

import jax
import jax.numpy as jnp
from jax import lax
from jax.experimental import pallas as pl
from jax.experimental.pallas import tpu as pltpu

F32 = jnp.float32
BF16 = jnp.bfloat16
I32 = jnp.int32

D_MODEL = 1024
EPS = 1e-6
CHUNK = 64
CONV_W = 4
DN_HEADS, DN_DK, DN_DV = 4, 128, 128
DN_QK = DN_HEADS * DN_DK
DN_V = DN_HEADS * DN_DV
CONV_CH = 2 * DN_QK + DN_V
GLA_HEADS, GLA_DK, GLA_DV = 4, 64, 128
GLA_K = GLA_HEADS * GLA_DK
GLA_V = GLA_HEADS * GLA_DV
GLA_RANK = 16
GLA_NORMALIZER = 16.0
GLA_PACK = 128 // GLA_DK
AB_SPLITS = (CONV_CH, DN_HEADS, DN_HEADS, DN_V, GLA_K, GLA_K, GLA_V, GLA_RANK, GLA_V)
HG_HEADS, HG_DK, HG_DV = 8, 128, 128
HG_F = HG_HEADS * HG_DK
N_GROUPS, EXPERTS_PER_GROUP = 4, 8
N_EXPERTS = N_GROUPS * EXPERTS_PER_GROUP
TOP_K = 2
D_EXPERT = 512

LANES = 128
SUBLANES = 8
VMEM_LIMIT_BYTES = 56 * 1024 * 1024

AB_U, AB_MISC, AB_Z, AB_GQ, AB_GK, AB_GV, AB_GG = 0, 1536, 1664, 2176, 2432, 2688, 3200
AB_COLS = AB_GG + GLA_V
MISC_A, MISC_B, MISC_LR = 0, DN_HEADS, 2 * DN_HEADS

INFO_E, INFO_W, INFO_R = 0, TOP_K, 2 * TOP_K

TOKEN_TILE = 512
MOE_TILE = 256
NEG = -1e30
LOG2E = 1.4426950408889634
GLA_SUB_BLOCK = SUBLANES


def _params(sem):
    return pltpu.CompilerParams(dimension_semantics=sem, vmem_limit_bytes=VMEM_LIMIT_BYTES)


def _mm(a, b):
    return jnp.dot(a.astype(BF16), b.astype(BF16), preferred_element_type=F32)


def _mm_nt(a, b):
    return lax.dot_general(a.astype(BF16), b.astype(BF16), (((1,), (1,)), ((), ())), preferred_element_type=F32)


def _mm_tn(a, b):
    return lax.dot_general(a.astype(BF16), b.astype(BF16), (((0,), (0,)), ((), ())), preferred_element_type=F32)


def _mm_hi(a, b):
    return jnp.dot(a, b, preferred_element_type=F32, precision=lax.Precision.HIGHEST)


def _sigmoid(x):
    return 1.0 / (1.0 + jnp.exp(-x))


def _softplus(x):
    return jnp.maximum(x, 0.0) + jnp.log(1.0 + jnp.exp(-jnp.abs(x)))


def _rms(x, gain):
    return x * lax.rsqrt(jnp.mean(x * x, axis=-1, keepdims=True) + EPS) * gain


def _gated_rms(o, gate, gain):
    return _rms(o, gain) * (gate * _sigmoid(gate))


ROW_TILE = D_MODEL // LANES
assert ROW_TILE == SUBLANES


def _store_token_tiles(ref, x):
    tm = x.shape[0]
    for j in range(ROW_TILE):
        ref[pl.ds(j, tm, stride=ROW_TILE), :] = x[:, j * LANES:(j + 1) * LANES]


def _load_token_tiles(ref):
    tm = ref.shape[0] // ROW_TILE
    return jnp.concatenate([ref[pl.ds(j, tm, stride=ROW_TILE), :] for j in range(ROW_TILE)], axis=1)


def _row_to_col(row):
    n = row.shape[1]
    r = lax.broadcasted_iota(I32, (n, n), 0)
    c = lax.broadcasted_iota(I32, (n, n), 1)
    return jnp.sum(jnp.where(r == c, jnp.broadcast_to(row, (n, n)), 0.0), axis=1, keepdims=True)


def _split2(x):
    hi = x.astype(BF16)
    return hi, (x - hi.astype(F32)).astype(BF16)


def _split3(x):
    hi = x.astype(BF16)
    r = x - hi.astype(F32)
    mid = r.astype(BF16)
    return hi, mid, (r - mid.astype(F32)).astype(BF16)


def _mm_cum(tri, x):
    hi, mid, lo = _split3(x)
    dot = lambda p: jnp.dot(tri, p, preferred_element_type=F32)
    return dot(hi) + (dot(mid) + dot(lo))


def _mm_split(a, b):
    ah, al = _split2(a)
    bh, bl = _split2(b)
    dot = lambda x, y: jnp.dot(x, y, preferred_element_type=F32)
    return dot(ah, bh) + (dot(ah, bl) + dot(al, bh))


def _ticker(fillers):
    pending = list(fillers)

    def tick(flush=False):
        while pending:
            pending.pop(0)()
            if not flush:
                break
    return tick


def _delta_chunks(qs_, ks_, vs_, gcs_, grs_, betas_, ss_, fillers=()):
    tick = _ticker(fillers)
    heads = range(len(qs_))
    c = qs_[0].shape[0]
    ti = lax.broadcasted_iota(I32, (c, c), 0)
    tj = lax.broadcasted_iota(I32, (c, c), 1)
    eye = jnp.where(ti == tj, 1.0, 0.0)
    decay = [jnp.exp(jnp.where(ti >= tj, gcs_[h] - grs_[h], NEG)) for h in heads]
    qs = [qs_[h] * (DN_DK ** -0.5) for h in heads]
    kk = [_mm_nt(ks_[h], ks_[h]) for h in heads]
    nm = [jnp.where(ti > tj, betas_[h] * kk[h] * decay[h], 0.0) for h in heads]
    t = [eye - nm[h] for h in heads]
    p = nm
    step = 2
    while step < c:
        tick()
        p = [_mm(p[h], p[h]) for h in heads]
        t = [t[h] + _mm(t[h], p[h]) for h in heads]
        step *= 2
    eg = [jnp.exp(gcs_[h]) for h in heads]
    rhs = [jnp.concatenate([betas_[h] * vs_[h], (betas_[h] * eg[h]) * ks_[h]], axis=1) for h in heads]
    tick()
    sol = [_mm(t[h], rhs[h]) for h in heads]
    tick()
    resid = [rhs[h] - (sol[h] + _mm_split(nm[h], sol[h])) for h in heads]
    tick()
    sol = [sol[h] + _mm(t[h], resid[h]) for h in heads]
    qk = [_mm_nt(qs[h], ks_[h]) * decay[h] for h in heads]
    tick()
    u = [sol[h][:, :DN_DV] - _mm(sol[h][:, DN_DV:], ss_[h]) for h in heads]
    tick()
    o = [_mm(qs[h] * eg[h], ss_[h]) + _mm(qk[h], u[h]) for h in heads]
    gl = [gcs_[h][c - 1:c] for h in heads]
    s_new = [ss_[h] * jnp.exp(gl[h]) + _mm_tn(ks_[h] * jnp.exp(gl[h] - gcs_[h]), u[h]) for h in heads]
    tick(flush=True)
    return o, s_new


def _gla_chunks(qs, ks, gs, vss, ss, m, fillers=()):
    tick = _ticker(fillers)
    probs = range(len(qs))
    c = qs[0].shape[0]
    nh = len(vss[0])
    heads = range(nh)
    dk = LANES // nh
    dlane = lax.broadcasted_iota(I32, (1, LANES), 1)
    if nh == 1:
        mask = lambda x, h: x
    else:
        hm = [jnp.where(jnp.logical_and(dlane >= h * dk, dlane < (h + 1) * dk), 1.0, 0.0) for h in heads]
        mask = lambda x, h: x * hm[h]
    gs = [gs[i] * LOG2E for i in probs]
    qeg = [qs[i] * jnp.exp2(gs[i]) for i in probs]
    o = [[_mm(mask(qeg[i], h), ss[i]) for h in heads] for i in probs]
    lane = lax.broadcasted_iota(I32, (m, c), 1)
    sub = lax.broadcasted_iota(I32, (m, c), 0)
    blocks = [[[] for _ in heads] for _ in probs]
    for blk in range(c // m):
        tick()
        r0 = blk * m
        qb = [qs[i][r0:r0 + m] for i in probs]
        gb = [gs[i][r0:r0 + m] for i in probs]
        kb = [ks[i][r0:r0 + m] for i in probs]
        if blk > 0:
            base = [gs[i][r0 - 1:r0] for i in probs]
            kt = [jnp.concatenate([ks[i][:r0] * jnp.exp2(base[i] - gs[i][:r0]), jnp.zeros((c - r0, LANES), F32)], axis=0)
                  for i in probs]
            qt = [qb[i] * jnp.exp2(gb[i] - base[i]) for i in probs]
            att = [[_mm_nt(mask(qt[i], h), kt[i]) for h in heads] for i in probs]
        else:
            att = [[jnp.zeros((m, c), F32) for _ in heads] for _ in probs]
        for j in range(m):
            keep = jnp.logical_and(lane == r0 + j, sub >= j)
            for i in probs:
                prod = qb[i] * (kb[i][j:j + 1] * jnp.exp2(gb[i] - gb[i][j:j + 1]))
                for h in heads:
                    att[i][h] = jnp.where(keep, jnp.sum(mask(prod, h), axis=1, keepdims=True), att[i][h])
        for i in probs:
            for h in heads:
                blocks[i][h].append(att[i][h])
    tick(flush=True)
    s_new = []
    for i in probs:
        gl = gs[i][c - 1:c]
        kg = ks[i] * jnp.exp2(gl - gs[i])
        s_i = ss[i] * _row_to_col(jnp.exp2(gl))
        for h in heads:
            v = vss[i][h]() if callable(vss[i][h]) else vss[i][h]
            att = blocks[i][h][0] if len(blocks[i][h]) == 1 else jnp.concatenate(blocks[i][h], axis=0)
            o[i][h] = o[i][h] + _mm(att, v)
            s_i = s_i + _mm_tn(mask(kg, h), v)
        s_new.append(s_i)
    return o, s_new


def _tri_incl(c):
    r = lax.broadcasted_iota(I32, (c, c), 0)
    cc = lax.broadcasted_iota(I32, (c, c), 1)
    return jnp.where(r >= cc, 1.0, 0.0).astype(BF16)


def _combine_kernel(pos_ref, *refs, final, tiles_a):
    if final:
        x_ref, info_ref, gain_ref, ys_hbm, outa_ref, outb_ref, ybuf, sem = refs
    else:
        x_ref, info_ref, ys_hbm, out_ref, ybuf, sem = refs
    i = pl.program_id(0)
    n = pl.num_programs(0)
    tm = x_ref.shape[0]
    slot = i % 2

    def gather_start(tile, s):
        def body(r, carry):
            for k in range(TOP_K):
                src = pl.multiple_of(pos_ref[TOP_K * (tile * tm + r) + k] * ROW_TILE, ROW_TILE)
                dst = pl.multiple_of(r * ROW_TILE, ROW_TILE)
                pltpu.make_async_copy(ys_hbm.at[pl.ds(src, ROW_TILE)], ybuf.at[s, k, pl.ds(dst, ROW_TILE)],
                                      sem.at[s]).start(priority=k % 2)
            return carry
        lax.fori_loop(0, tm, body, 0, unroll=8)

    @pl.when(i == 0)
    def _():
        gather_start(0, 0)

    @pl.when(i + 1 < n)
    def _():
        gather_start(i + 1, 1 - slot)

    for k in range(TOP_K):
        pltpu.make_async_copy(ys_hbm.at[pl.ds(0, tm * ROW_TILE)], ybuf.at[slot, k], sem.at[slot]).wait()
    info = info_ref[...]
    y0, y1 = (_load_token_tiles(ybuf.at[slot, k]) for k in range(TOP_K))
    x = x_ref[...] + (info[:, INFO_W:INFO_W + 1] * y0 + info[:, INFO_W + 1:INFO_W + 2] * y1)
    if final:
        h = _rms(x, gain_ref[...])

        @pl.when(i < tiles_a)
        def _():
            outa_ref[...] = h

        @pl.when(i >= tiles_a)
        def _():
            outb_ref[...] = h
    else:
        out_ref[...] = x


def _combine(x, info, pos, ys, final_gain=None, n_a=None):
    n, d = x.shape
    tm = TOKEN_TILE
    final = final_gain is not None
    tiles_a = n_a // tm if final else None
    row = lambda width: pl.BlockSpec((tm, width), lambda i, pos: (i, 0))
    in_specs = [row(d), row(LANES)]
    args = [x, info]
    if final:
        in_specs.append(pl.BlockSpec((1, d), lambda i, pos: (0, 0)))
        args.append(final_gain)
        out_specs = [pl.BlockSpec((tm, d), lambda i, pos: (jnp.minimum(i, tiles_a - 1), 0)),
                     pl.BlockSpec((tm, d), lambda i, pos: (jnp.maximum(i - tiles_a, 0), 0))]
        out_shape = [jax.ShapeDtypeStruct((n_a, d), F32), jax.ShapeDtypeStruct((n - n_a, d), F32)]
    else:
        out_specs = row(d)
        out_shape = jax.ShapeDtypeStruct((n, d), F32)
    in_specs.append(pl.BlockSpec(memory_space=pl.ANY))
    args.append(ys)
    grid_spec = pltpu.PrefetchScalarGridSpec(
        num_scalar_prefetch=1, grid=(n // tm,), in_specs=in_specs, out_specs=out_specs,
        scratch_shapes=[pltpu.VMEM((2, TOP_K, tm * ROW_TILE, LANES), F32), pltpu.SemaphoreType.DMA((2,))])
    return pl.pallas_call(
        functools.partial(_combine_kernel, final=final, tiles_a=tiles_a),
        grid_spec=grid_spec,
        out_shape=out_shape,
        compiler_params=_params(("arbitrary",)),
        name="moe_combine",
    )(pos, *args)


PROJ_CHUNK = 256


def _project_rows(x_refs, gain_ref, w_ref, p_scr, h_scr, c, first_cols):
    x = jnp.concatenate([r[...] for r in x_refs], axis=0) if len(x_refs) > 1 else x_refs[0][...]
    h_scr[...] = _rms(x, gain_ref[...]).astype(BF16)

    def piece(c0, c1):
        def run():
            p_scr[:, c0:c1] = jnp.dot(h_scr[...], w_ref[:, c0:c1], preferred_element_type=F32)
        return run

    piece(0, first_cols)()
    cols = p_scr.shape[1]
    fillers = [piece(c0, min(c0 + PROJ_CHUNK, cols)) for c0 in range(first_cols, cols, PROJ_CHUNK)]
    return [p_scr.at[pl.ds(b * c, c)] for b in range(len(x_refs))], fillers


def _ab_mixer_kernel(*refs, c, nseq, zero_init):
    x_refs, refs = refs[:nseq], refs[nseq:]
    if zero_init:
        gain_ref, win_ref, cw_ref, prm_ref, gkw_ref, gkb_ref, o_ref, conv_ref, sdn_ref, sgla_ref, p_scr, h_scr = refs
    else:
        (gain_ref, win_ref, cw_ref, prm_ref, gkw_ref, gkb_ref, conv_in, sdn_in, sgla_in,
         o_ref, conv_ref, sdn_ref, sgla_ref, p_scr, h_scr) = refs

    @pl.when(pl.program_id(1) == 0)
    def _():
        if zero_init:
            conv_ref[...] = jnp.zeros_like(conv_ref)
            sdn_ref[...] = jnp.zeros_like(sdn_ref)
            sgla_ref[...] = jnp.zeros_like(sgla_ref)
        else:
            conv_ref[...] = conv_in[...]
            sdn_ref[...] = sdn_in[...]
            sgla_ref[...] = sgla_in[...]

    p_refs, fillers = _project_rows(x_refs, gain_ref, win_ref, p_scr, h_scr, c, AB_Z)

    m = min(GLA_SUB_BLOCK, c)
    tri = _tri_incl(c)
    cw = cw_ref[...]
    a_log, dt_bias = prm_ref[0:1], prm_ref[1:2]
    dn_gain, gla_gain = prm_ref[2:3], prm_ref[3:4]
    n_pairs = GLA_HEADS // GLA_PACK
    conv_old = [conv_ref[b] for b in range(nseq)]
    sdn_old = [[sdn_ref[b, h] for h in range(DN_HEADS)] for b in range(nseq)]
    sgla_old = [[sgla_ref[b, pr] for pr in range(n_pairs)] for b in range(nseq)]
    dn = {key: [] for key in ("q", "k", "v", "gc", "gr", "beta", "s")}
    miscs = []
    for b in range(nseq):
        p_ref = p_refs[b]
        u = p_ref[:, AB_U:AB_U + CONV_CH]
        ucat = jnp.concatenate([conv_old[b], u], axis=0)
        acc = u * cw[CONV_W - 1:CONV_W]
        for j in range(1, CONV_W):
            acc = acc + pltpu.roll(ucat, j, 0)[SUBLANES:SUBLANES + c] * cw[CONV_W - 1 - j:CONV_W - j]
        conv_ref[b] = ucat[c:c + SUBLANES]
        qkv = acc * _sigmoid(acc)

        misc = p_ref[:, AB_MISC:AB_MISC + LANES]
        miscs.append(misc)
        g_all = -jnp.exp(a_log) * _softplus(misc + dt_bias)
        beta_all = _sigmoid(misc)
        gcum = _mm_cum(tri, g_all)
        gcum_t = jnp.concatenate([gcum, jnp.zeros((LANES - c, LANES), F32)], axis=0).T
        for h in range(DN_HEADS):
            lo = h * DN_DK
            q = qkv[:, lo:lo + DN_DK]
            k = qkv[:, DN_QK + lo:DN_QK + lo + DN_DK]
            dn["q"].append(q * lax.rsqrt(jnp.sum(q * q, axis=-1, keepdims=True) + 1e-6))
            dn["k"].append(k * lax.rsqrt(jnp.sum(k * k, axis=-1, keepdims=True) + 1e-6))
            dn["v"].append(qkv[:, 2 * DN_QK + h * DN_DV:2 * DN_QK + (h + 1) * DN_DV])
            dn["gc"].append(gcum[:, MISC_A + h:MISC_A + h + 1])
            dn["gr"].append(gcum_t[MISC_A + h:MISC_A + h + 1, :c])
            dn["beta"].append(beta_all[:, MISC_B + h:MISC_B + h + 1])
            dn["s"].append(sdn_old[b][h])
    os_, ss_ = _delta_chunks(dn["q"], dn["k"], dn["v"], dn["gc"], dn["gr"], dn["beta"], dn["s"], fillers)
    for b in range(nseq):
        for h in range(DN_HEADS):
            i = b * DN_HEADS + h
            sdn_ref[b, h] = ss_[i]
            z = p_refs[b][:, AB_Z + h * DN_DV:AB_Z + (h + 1) * DN_DV]
            o_ref[b, :, h * DN_DV:(h + 1) * DN_DV] = _gated_rms(os_[i], z, dn_gain).astype(o_ref.dtype)

    gl = {key: [] for key in ("q", "k", "g", "v", "s")}
    for b in range(nseq):
        p_ref = p_refs[b]
        logits = _mm_split(miscs[b], gkw_ref[...]) + gkb_ref[...]
        logf = -_softplus(-logits) * (1.0 / GLA_NORMALIZER)
        gall = _mm_cum(tri, logf)
        for pair in range(n_pairs):
            lo = pair * LANES
            gl["q"].append(p_ref[:, AB_GQ + lo:AB_GQ + lo + LANES] * (GLA_DK ** -0.5))
            gl["k"].append(p_ref[:, AB_GK + lo:AB_GK + lo + LANES])
            gl["g"].append(gall[:, lo:lo + LANES])
            gl["v"].append([p_ref[:, AB_GV + h * GLA_DV:AB_GV + (h + 1) * GLA_DV]
                            for h in range(pair * GLA_PACK, (pair + 1) * GLA_PACK)])
            gl["s"].append(sgla_old[b][pair])
    os_, ss_ = _gla_chunks(gl["q"], gl["k"], gl["g"], gl["v"], gl["s"], m)
    for b in range(nseq):
        for pair in range(n_pairs):
            i = b * n_pairs + pair
            sgla_ref[b, pair] = ss_[i]
            for hh in range(GLA_PACK):
                h = pair * GLA_PACK + hh
                gate = p_refs[b][:, AB_GG + h * GLA_DV:AB_GG + (h + 1) * GLA_DV]
                o_ref[b, :, DN_V + h * GLA_DV:DN_V + (h + 1) * GLA_DV] = \
                    _gated_rms(os_[i][hh], gate, gla_gain).astype(o_ref.dtype)


def _seq_specs(cols, row0, c, nchunk, nseq):
    blk0 = row0 // c
    return [pl.BlockSpec((c, cols), lambda g, i, s=s: (blk0 + (g * nseq + s) * nchunk + i, 0)) for s in range(nseq)]


def _ab_mixer(x, row0, nb, t, nseq, gain, w_in, cw, prm, gkw, gkb, states):
    c = min(CHUNK, t)
    nchunk = t // c
    n_groups = nb // nseq
    zero_init = states is None
    st_map3 = lambda g, i: (g, 0, 0)
    st_map4 = lambda g, i: (g, 0, 0, 0)
    const = lambda shape: pl.BlockSpec(shape, lambda g, i: (0,) * len(shape))
    consts = [gain, w_in, cw, prm, gkw, gkb]
    in_specs = _seq_specs(D_MODEL, row0, c, nchunk, nseq) + [const(a.shape) for a in consts]
    args = [x] * nseq + consts
    gla_packed = (GLA_HEADS // GLA_PACK, GLA_PACK * GLA_DK, GLA_DV)
    st_specs = [pl.BlockSpec((nseq, SUBLANES, CONV_CH), st_map3),
                pl.BlockSpec((nseq, DN_HEADS, DN_DK, DN_DV), st_map4),
                pl.BlockSpec((nseq,) + gla_packed, st_map4)]
    if not zero_init:
        conv8, sdn, sgla = states
        in_specs += st_specs
        args += [conv8, sdn, sgla.reshape((nb,) + gla_packed)]
    out_shape = [jax.ShapeDtypeStruct((nb, t, D_MODEL), BF16),
                 jax.ShapeDtypeStruct((nb, SUBLANES, CONV_CH), F32),
                 jax.ShapeDtypeStruct((nb, DN_HEADS, DN_DK, DN_DV), F32),
                 jax.ShapeDtypeStruct((nb,) + gla_packed, F32)]
    out_specs = [pl.BlockSpec((nseq, c, D_MODEL), lambda g, i: (g, i, 0))] + st_specs
    o, conv_out, sdn_out, sgla_out = pl.pallas_call(
        functools.partial(_ab_mixer_kernel, c=c, nseq=nseq, zero_init=zero_init),
        grid=(n_groups, nchunk),
        in_specs=in_specs,
        out_specs=out_specs,
        out_shape=out_shape,
        scratch_shapes=[pltpu.VMEM((nseq * c, AB_COLS), F32), pltpu.VMEM((nseq * c, D_MODEL), BF16)],
        compiler_params=_params(("parallel", "arbitrary")),
        name="ab_mixer",
    )(*args)
    return o.reshape(nb * t, D_MODEL), conv_out, sdn_out, sgla_out.reshape(nb, GLA_HEADS, GLA_DK, GLA_DV)


def _hgrn_mixer_kernel(*refs, c, nseq, zero_init, layer):
    x_refs, refs = refs[:nseq], refs[nseq:]
    if zero_init:
        mixgain_ref, win_ref, lb_ref, gain_ref, o_ref, s_ref, p_scr, h_scr = refs
    else:
        mixgain_ref, win_ref, lb_ref, gain_ref, s_in, o_ref, s_ref, p_scr, h_scr = refs
    p_refs, fillers = _project_rows(x_refs, mixgain_ref, win_ref, p_scr, h_scr, c, 2 * HG_F)

    @pl.when(pl.program_id(1) == 0)
    def _():
        if zero_init:
            s_ref[...] = jnp.zeros_like(s_ref)
        else:
            s_ref[...] = s_in[...]

    raw = lb_ref[...]
    e = jnp.exp(raw - jnp.max(raw, axis=0, keepdims=True))
    sm = e / jnp.sum(e, axis=0, keepdims=True)
    cum = sm[0:1]
    for l in range(1, layer + 1):
        cum = cum + sm[l:l + 1]
    lb = cum - sm[0:1]

    m = min(GLA_SUB_BLOCK, c)
    tri = _tri_incl(c)
    gain = gain_ref[...]
    pr = {key: [] for key in ("q", "k", "g", "v", "s")}
    for b in range(nseq):
        p_ref = p_refs[b]
        fr = p_ref[:, HG_F:2 * HG_F]
        logf = jnp.log(lb + (1.0 - lb) * _sigmoid(fr))
        kall = (1.0 - lb) * _sigmoid(-fr)
        gall = _mm_cum(tri, logf)
        for h in range(HG_HEADS):
            lo = h * HG_DK
            pr["q"].append(p_ref[:, lo:lo + HG_DK] * (HG_DK ** -0.5))
            pr["k"].append(kall[:, lo:lo + HG_DK])
            pr["g"].append(gall[:, lo:lo + HG_DK])
            pr["v"].append([functools.partial(lambda r, h: r[:, 2 * HG_F + h * HG_DV:2 * HG_F + (h + 1) * HG_DV], p_ref, h)])
            pr["s"].append(s_ref[b, h])
    os_, ss_ = _gla_chunks(pr["q"], pr["k"], pr["g"], pr["v"], pr["s"], m, fillers)
    for b in range(nseq):
        for h in range(HG_HEADS):
            i = b * HG_HEADS + h
            s_ref[b, h] = ss_[i]
            gate = p_refs[b][:, 2 * HG_F + D_MODEL + h * HG_DV:2 * HG_F + D_MODEL + (h + 1) * HG_DV]
            o_ref[b, :, h * HG_DV:(h + 1) * HG_DV] = _gated_rms(os_[i][0], gate, gain).astype(o_ref.dtype)


def _hgrn_mixer(x, row0, nb, t, nseq, mix_gain, w_in, lb, gain, state, layer):
    c = min(CHUNK, t)
    nchunk = t // c
    zero_init = state is None
    const = lambda shape: pl.BlockSpec(shape, lambda g, i: (0,) * len(shape))
    st_spec = pl.BlockSpec((nseq, HG_HEADS, HG_DK, HG_DV), lambda g, i: (g, 0, 0, 0))
    consts = [mix_gain, w_in, lb, gain]
    in_specs = _seq_specs(D_MODEL, row0, c, nchunk, nseq) + [const(a.shape) for a in consts]
    args = [x] * nseq + consts
    if not zero_init:
        in_specs.append(st_spec)
        args.append(state)
    o, s_out = pl.pallas_call(
        functools.partial(_hgrn_mixer_kernel, c=c, nseq=nseq, zero_init=zero_init, layer=layer),
        grid=(nb // nseq, nchunk),
        in_specs=in_specs,
        out_specs=[pl.BlockSpec((nseq, c, D_MODEL), lambda g, i: (g, i, 0)), st_spec],
        out_shape=[jax.ShapeDtypeStruct((nb, t, D_MODEL), BF16),
                   jax.ShapeDtypeStruct((nb, HG_HEADS, HG_DK, HG_DV), F32)],
        scratch_shapes=[pltpu.VMEM((nseq * c, w_in.shape[1]), F32), pltpu.VMEM((nseq * c, D_MODEL), BF16)],
        compiler_params=_params(("parallel", "arbitrary")),
        name="hgrn_mixer",
    )(*args)
    return o.reshape(nb * t, D_MODEL), s_out


def _outproj_router_kernel(*refs, tiles_a, split_x):
    if split_x:
        oa_ref, ob_ref, xa_ref, xb_ref = refs[:4]
        refs = refs[4:]
    else:
        oa_ref, ob_ref, x_ref = refs[:3]
        refs = refs[3:]
    wout_ref, gain_ref, wr_ref, br_ref, x1_ref, h_ref, info_ref, cnt_ref, run_ref = refs
    first = pl.program_id(0) < tiles_a

    @pl.when(pl.program_id(0) == 0)
    def _():
        run_ref[...] = jnp.zeros_like(run_ref)

    o = jnp.where(first, oa_ref[...], ob_ref[...])
    x = jnp.where(first, xa_ref[...], xb_ref[...]) if split_x else x_ref[...]
    x1 = x + jnp.dot(o, wout_ref[...], preferred_element_type=F32)
    x1_ref[...] = x1
    h = _rms(x1, gain_ref[...])
    _store_token_tiles(h_ref, h)
    logit = _mm_hi(h, wr_ref[...]) + br_ref[...]
    lane = lax.broadcasted_iota(I32, logit.shape, 1)
    far = jnp.int32(LANES)
    lg = jnp.where(lane < N_GROUPS, logit, NEG)
    mg = jnp.max(lg, axis=-1, keepdims=True)
    p_g = 1.0 / jnp.sum(jnp.exp(lg - mg), axis=-1, keepdims=True)
    g_top = jnp.min(jnp.where(lg == mg, lane, far), axis=-1, keepdims=True)
    lo = N_GROUPS + g_top * EXPERTS_PER_GROUP
    sel = jnp.logical_and(lane >= lo, lane < lo + EXPERTS_PER_GROUP)
    le = jnp.where(sel, logit, NEG)
    ee = jnp.exp(le - jnp.max(le, axis=-1, keepdims=True))
    pe = jnp.where(sel, ee / jnp.sum(ee, axis=-1, keepdims=True), -1.0)
    p1 = jnp.max(pe, axis=-1, keepdims=True)
    i1 = jnp.min(jnp.where(pe == p1, lane, far), axis=-1, keepdims=True)
    pe2 = jnp.where(lane == i1, -1.0, pe)
    p2 = jnp.max(pe2, axis=-1, keepdims=True)
    i2 = jnp.min(jnp.where(pe2 == p2, lane, far), axis=-1, keepdims=True)
    denom = p1 + p2
    w1 = p_g * (p1 / denom)
    w2 = p_g * (p2 / denom)
    tm = logit.shape[0]
    oh1 = jnp.where(lane == i1, 1.0, 0.0)
    oh2 = jnp.where(lane == i2, 1.0, 0.0)
    both = oh1 + oh2
    tr = lax.broadcasted_iota(I32, (tm, tm), 0)
    tc = lax.broadcasted_iota(I32, (tm, tm), 1)
    before = run_ref[...] + _mm(jnp.where(tr > tc, 1.0, 0.0), both)
    r1 = jnp.sum(oh1 * before, axis=-1, keepdims=True)
    r2 = jnp.sum(oh2 * before, axis=-1, keepdims=True)
    run = run_ref[...] + jnp.sum(both, axis=0, keepdims=True)
    run_ref[...] = run
    cnt_ref[...] = jnp.broadcast_to(run, cnt_ref.shape)
    vals = ((i1 - N_GROUPS).astype(F32), (i2 - N_GROUPS).astype(F32), w1, w2, r1, r2)
    info = jnp.zeros_like(logit)
    for idx, val in enumerate(vals):
        info = jnp.where(lane == idx, val, info)
    info_ref[...] = info


def _outproj_router(o_a, o_b, xs, wout, gain, wr, br):
    n_a, d = o_a.shape
    n = n_a + o_b.shape[0]
    tm = TOKEN_TILE
    tiles_a = n_a // tm
    row = lambda w: pl.BlockSpec((tm, w), lambda i: (i, 0))
    seg_a = pl.BlockSpec((tm, d), lambda i: (jnp.minimum(i, tiles_a - 1), 0))
    seg_b = pl.BlockSpec((tm, d), lambda i: (jnp.maximum(i - tiles_a, 0), 0))
    const = lambda shape: pl.BlockSpec(shape, lambda i: (0,) * len(shape))
    split_x = len(xs) == 2
    return pl.pallas_call(
        functools.partial(_outproj_router_kernel, tiles_a=tiles_a, split_x=split_x),
        grid=(n // tm,),
        in_specs=[seg_a, seg_b] + ([seg_a, seg_b] if split_x else [row(d)])
        + [const((d, d)), const((1, d)), const((d, LANES)), const((1, LANES))],
        out_specs=[row(d), pl.BlockSpec((tm * ROW_TILE, LANES), lambda i: (i, 0)), row(LANES), const((SUBLANES, LANES))],
        out_shape=[jax.ShapeDtypeStruct((n, d), F32), jax.ShapeDtypeStruct((n * ROW_TILE, LANES), F32),
                   jax.ShapeDtypeStruct((n, LANES), F32), jax.ShapeDtypeStruct((SUBLANES, LANES), F32)],
        scratch_shapes=[pltpu.VMEM((1, LANES), F32)],
        compiler_params=_params(("arbitrary",)),
        name="outproj_router",
    )(o_a, o_b, *xs, wout, gain, wr, br)


def _dispatch_kernel(pos_ref, lo_ref, hi_ref, h_ref, xs_hbm, buf, zrow, sem, zsem):
    i = pl.program_id(0)
    n = pl.num_programs(0)
    tm = h_ref.shape[0] // ROW_TILE
    blk = tm * ROW_TILE
    slot = i % 2
    token = lambda r: pl.ds(pl.multiple_of(r * ROW_TILE, ROW_TILE), ROW_TILE)

    def tile_wait(s):
        for _ in range(TOP_K):
            pltpu.make_async_copy(buf.at[s], xs_hbm.at[pl.ds(0, blk)], sem.at[s]).wait()

    zblk = zrow.shape[0]

    def pad_rows(fn, tile_fn):
        for e in range(N_EXPERTS):
            lax.fori_loop(lo_ref[e], hi_ref[e], fn, 0)
        lax.fori_loop(hi_ref[N_EXPERTS - 1] // (zblk // ROW_TILE), xs_hbm.shape[0] // zblk, tile_fn, 0)

    def pad_start(r, carry):
        pltpu.make_async_copy(zrow.at[token(0)], xs_hbm.at[token(r)], zsem.at[0]).start()
        return carry

    def pad_wait(r, carry):
        pltpu.make_async_copy(zrow.at[token(0)], xs_hbm.at[token(0)], zsem.at[0]).wait()
        return carry

    def pad_tile_start(t, carry):
        pltpu.make_async_copy(zrow, xs_hbm.at[pl.ds(pl.multiple_of(t * zblk, zblk), zblk)], zsem.at[0]).start()
        return carry

    def pad_tile_wait(t, carry):
        pltpu.make_async_copy(zrow, xs_hbm.at[pl.ds(0, zblk)], zsem.at[0]).wait()
        return carry

    @pl.when(i == 0)
    def _():
        zrow[...] = jnp.zeros_like(zrow)
        pad_rows(pad_start, pad_tile_start)

    @pl.when(i >= 2)
    def _():
        tile_wait(slot)
    buf[slot] = h_ref[...]

    def body(r, carry):
        for k in range(TOP_K):
            row = pos_ref[TOP_K * (i * tm + r) + k]
            pltpu.make_async_copy(buf.at[slot, token(r)], xs_hbm.at[token(row)], sem.at[slot]).start(priority=k % 2)
        return carry
    lax.fori_loop(0, tm, body, 0, unroll=8)

    @pl.when(i == n - 1)
    def _():
        tile_wait(slot)

        @pl.when(n > 1)
        def _():
            tile_wait(1 - slot)
        pad_rows(pad_wait, pad_tile_wait)


def _dispatch(h, pos, pad_lo, pad_hi, n_rows):
    blk = TOKEN_TILE * ROW_TILE
    n = h.shape[0] // ROW_TILE
    grid_spec = pltpu.PrefetchScalarGridSpec(
        num_scalar_prefetch=3,
        grid=(n // TOKEN_TILE,),
        in_specs=[pl.BlockSpec((blk, LANES), lambda i, pos, lo, hi: (i, 0))],
        out_specs=pl.BlockSpec(memory_space=pl.ANY),
        scratch_shapes=[pltpu.VMEM((2, blk, LANES), F32), pltpu.VMEM((MOE_TILE * ROW_TILE, LANES), F32),
                        pltpu.SemaphoreType.DMA((2,)), pltpu.SemaphoreType.DMA((1,))],
    )
    return pl.pallas_call(
        _dispatch_kernel,
        grid_spec=grid_spec,
        out_shape=jax.ShapeDtypeStruct((n_rows * ROW_TILE, LANES), F32),
        compiler_params=_params(("arbitrary",)),
        name="moe_dispatch",
    )(pos, pad_lo, pad_hi, h)


def _moe_kernel(te_ref, nact_ref, x_ref, w1_ref, w3_ref, w2_ref, y_ref, w1b, w3b, w2b):
    i = pl.program_id(0)

    @pl.when(i < nact_ref[0])
    def _():
        @pl.when(jnp.logical_or(i == 0, te_ref[i] != te_ref[jnp.maximum(i - 1, 0)]))
        def _():
            w1b[...] = w1_ref[0, 0].astype(BF16)
            w3b[...] = w3_ref[0, 0].astype(BF16)
            w2b[...] = w2_ref[0, 0].astype(BF16)

        xb = _load_token_tiles(x_ref).astype(BF16)
        h1 = jnp.dot(xb, w1b[...], preferred_element_type=F32)
        h3 = jnp.dot(xb, w3b[...], preferred_element_type=F32)
        act = (h1 * _sigmoid(h1)) * h3
        _store_token_tiles(y_ref, jnp.dot(act.astype(BF16), w2b[...], preferred_element_type=F32))

    @pl.when(i >= nact_ref[0])
    def _():
        y_ref[...] = jnp.zeros_like(y_ref)


def _moe(xs, te, nact, w1, w3, w2, layer):
    d, f = w1.shape[2], w1.shape[3]
    tm = MOE_TILE
    blk = tm * ROW_TILE
    n_tiles = xs.shape[0] // blk
    last = lambda i, nact: jnp.maximum(jnp.minimum(i, nact[0] - 1), 0)
    grid_spec = pltpu.PrefetchScalarGridSpec(
        num_scalar_prefetch=2,
        grid=(n_tiles,),
        in_specs=[
            pl.BlockSpec((blk, LANES), lambda i, te, nact: (last(i, nact), 0)),
            pl.BlockSpec((1, 1, d, f), lambda i, te, nact: (layer, te[i], 0, 0)),
            pl.BlockSpec((1, 1, d, f), lambda i, te, nact: (layer, te[i], 0, 0)),
            pl.BlockSpec((1, 1, f, d), lambda i, te, nact: (layer, te[i], 0, 0)),
        ],
        out_specs=pl.BlockSpec((blk, LANES), lambda i, te, nact: (i, 0)),
        scratch_shapes=[pltpu.VMEM((d, f), BF16), pltpu.VMEM((d, f), BF16), pltpu.VMEM((f, d), BF16)],
    )
    return pl.pallas_call(
        _moe_kernel,
        grid_spec=grid_spec,
        out_shape=jax.ShapeDtypeStruct(xs.shape, F32),
        compiler_params=_params(("arbitrary",)),
        name="moe_experts",
    )(te, nact, xs, w1, w3, w2)


def _moe_rows(n):
    return ((TOP_K * n + N_EXPERTS * (MOE_TILE - 1)) // MOE_TILE) * MOE_TILE


def _route_tables(info, counts_row, n):
    tm = MOE_TILE
    n_tiles = _moe_rows(n) // tm
    ids = jnp.arange(N_EXPERTS, dtype=I32)
    counts = counts_row[0, N_GROUPS:N_GROUPS + N_EXPERTS].astype(I32)
    padded = ((counts + tm - 1) // tm) * tm
    cum = jnp.cumsum(padded)
    off = cum - padded
    e = info[:, INFO_E:INFO_E + TOP_K].astype(I32)
    rank = info[:, INFO_R:INFO_R + TOP_K].astype(I32)
    pos = (jnp.sum(jnp.where(e[:, :, None] == ids, off, 0), axis=-1) + rank).reshape(TOP_K * n)
    tile_start = jnp.arange(n_tiles, dtype=I32) * tm
    nact = cum[-1] // tm
    te_raw = jnp.sum((cum[None, :] <= tile_start[:, None]).astype(I32), axis=1)
    te_last = jnp.sum((cum <= (nact - 1) * tm).astype(I32))
    te = jnp.where(tile_start < cum[-1], jnp.minimum(te_raw, N_EXPERTS - 1), te_last)
    return pos, off + counts, cum, te, nact.reshape(1)


def _pad_lanes(a, width=LANES):
    return jnp.pad(a, [(0, 0)] * (a.ndim - 1) + [(0, width - a.shape[-1])])


def _seq_group(nb, want):
    g = want
    while nb % g:
        g //= 2
    return g


PROMPT_SEQS_PER_STEP = 4
SAMPLE_SEQS_PER_STEP = 8


def kernel(x_prompt, x_sample, state_conv_ab, state_delta_ab, state_gla_ab, state_hgrn_c, mix_norm, ab_w_in, ab_conv_w, ab_a_log, ab_dt_bias, ab_dn_norm, ab_gk_w2, ab_gk_b, ab_gla_norm, ab_w_out, c_w_in, c_lower_bounds, c_norm, c_w_out, ffn_norm, moe_w_group, moe_b_group, moe_w_expert, moe_b_expert, moe_w1, moe_w3, moe_w2, final_norm):
    bp, tp, d = x_prompt.shape
    bs, ts, _ = x_sample.shape
    n_p, n_s = bp * tp, bs * ts
    n = n_p + n_s
    depth = mix_norm.shape[0]
    xs_res = (x_prompt.reshape(n_p, d), x_sample.reshape(n_s, d))
    pg, sg = _seq_group(bp, PROMPT_SEQS_PER_STEP), _seq_group(bs, SAMPLE_SEQS_PER_STEP)

    info = pos = ys = None
    conv_p, delta_p, gla_p, hgrn_p = [], [], [], []
    conv_s, delta_s, gla_s, hgrn_s = [], [], [], []
    for layer in range(depth):
        j = layer // 2
        gain = mix_norm[layer].reshape(1, d)
        if layer % 2 == 0:
            u_w, a_w, b_w, z_w, gq_w, gk_w, gv_w, lr_w, gg_w = jnp.split(ab_w_in[j], _split_points(AB_SPLITS), axis=1)
            w_in = jnp.concatenate([u_w, _pad_lanes(jnp.concatenate([a_w, b_w, lr_w], axis=1)), z_w, gq_w, gk_w, gv_w, gg_w],
                                   axis=1).astype(BF16)
        else:
            w_in = c_w_in[j].astype(BF16)
        if ys is not None:
            xs_res = (_combine(xs_res[0], info, pos, ys),)
        (x_p, off_p), (x_s, off_s) = ((xs_res[0], 0), (xs_res[-1], 0 if len(xs_res) == 2 else n_p))

        if layer % 2 == 0:
            prm = jnp.concatenate([_pad_lanes(ab_a_log[j].reshape(1, -1)), _pad_lanes(ab_dt_bias[j].reshape(1, -1)),
                                   ab_dn_norm[j].reshape(1, -1), ab_gla_norm[j].reshape(1, -1),
                                   jnp.zeros((4, LANES), F32)], axis=0)
            gkw = jnp.zeros((LANES, GLA_K), F32).at[MISC_LR:MISC_LR + GLA_RANK].set(ab_gk_w2[j])
            gkb = ab_gk_b[j].reshape(1, GLA_K)
            cw = ab_conv_w[j]
            o_p, c8, sd, sgl = _ab_mixer(x_p, off_p, bp, tp, pg, gain, w_in, cw, prm, gkw, gkb, None)
            conv_p.append(c8[:, SUBLANES - (CONV_W - 1):])
            delta_p.append(sd)
            gla_p.append(sgl)
            conv8 = jnp.pad(state_conv_ab[j], ((0, 0), (SUBLANES - (CONV_W - 1), 0), (0, 0)))
            o_s, c8, sd, sgl = _ab_mixer(x_s, off_s, bs, ts, sg, gain, w_in, cw, prm, gkw, gkb,
                                         (conv8, state_delta_ab[j], state_gla_ab[j]))
            conv_s.append(c8[:, SUBLANES - (CONV_W - 1):])
            delta_s.append(sd)
            gla_s.append(sgl)
            w_out = ab_w_out[j]
        else:
            hg_gain = c_norm[j].reshape(1, -1)
            o_p, sh = _hgrn_mixer(x_p, off_p, bp, tp, pg, gain, w_in, c_lower_bounds, hg_gain, None, layer)
            hgrn_p.append(sh)
            o_s, sh = _hgrn_mixer(x_s, off_s, bs, ts, sg, gain, w_in, c_lower_bounds, hg_gain, state_hgrn_c[j], layer)
            hgrn_s.append(sh)
            w_out = c_w_out[j]

        wr = _pad_lanes(jnp.concatenate([moe_w_group[layer], moe_w_expert[layer]], axis=1))
        br = _pad_lanes(jnp.concatenate([moe_b_group[layer], moe_b_expert[layer]]).reshape(1, -1))
        x, hn, info, counts = _outproj_router(o_p, o_s, xs_res, w_out.astype(BF16), ffn_norm[layer].reshape(1, d), wr, br)
        xs_res = (x,)
        pos, pad_lo, pad_hi, te, nact = _route_tables(info, counts, n)
        xsort = _dispatch(hn, pos, pad_lo, pad_hi, _moe_rows(n))
        ys = _moe(xsort, te, nact, moe_w1, moe_w3, moe_w2, layer)

    y_p, y_s = _combine(xs_res[0], info, pos, ys, final_gain=final_norm.reshape(1, d), n_a=n_p)
    y_prompt = y_p.reshape(bp, tp, d)
    y_sample = y_s.reshape(bs, ts, d)
    return (y_prompt, y_sample, jnp.stack(conv_p), jnp.stack(delta_p), jnp.stack(gla_p), jnp.stack(hgrn_p),
            jnp.stack(conv_s), jnp.stack(delta_s), jnp.stack(gla_s), jnp.stack(hgrn_s))


def _split_points(sizes):
    pts, acc = [], 0
    for s in sizes[:-1]:
        acc += s
        pts.append(acc)
    return pts
```

```python
import functools

import jax
import jax.numpy as jnp
from jax import lax
from jax.experimental import pallas as pl
from jax.experimental.pallas import tpu as pltpu

F32 = jnp.float32
BF16 = jnp.bfloat16
I32 = jnp.int32

D_MODEL = 1024
EPS = 1e-6
CHUNK = 64
CONV_W = 4
DN_HEADS, DN_DK, DN_DV = 4, 128, 128
DN_QK = DN_HEADS * DN_DK
DN_V = DN_HEADS * DN_DV
CONV_CH = 2 * DN_QK + DN_V
GLA_HEADS, GLA_DK, GLA_DV = 4, 64, 128
GLA_K = GLA_HEADS * GLA_DK
GLA_V = GLA_HEADS * GLA_DV
GLA_RANK = 16
GLA_NORMALIZER = 16.0
GLA_PACK = 128 // GLA_DK
AB_SPLITS = (CONV_CH, DN_HEADS, DN_HEADS, DN_V, GLA_K, GLA_K, GLA_V, GLA_RANK, GLA_V)
HG_HEADS, HG_DK, HG_DV = 8, 128, 128
HG_F = HG_HEADS * HG_DK
N_GROUPS, EXPERTS_PER_GROUP = 4, 8
N_EXPERTS = N_GROUPS * EXPERTS_PER_GROUP
TOP_K = 2
L2_EPS = 1e-6

LANES = 128
SUBLANES = 8
VMEM_LIMIT_BYTES = 56 * 1024 * 1024

AB_U, AB_MISC, AB_Z, AB_GQ, AB_GK, AB_GV, AB_GG = 0, 1536, 1664, 2176, 2432, 2688, 3200
AB_COLS = AB_GG + GLA_V
MISC_A, MISC_B, MISC_LR = 0, DN_HEADS, 2 * DN_HEADS

INFO_E, INFO_W, INFO_R = 0, TOP_K, 2 * TOP_K

TOKEN_TILE = 512
MOE_TILE = 256
NEG = -1e30
LOG2E = 1.4426950408889634
GLA_SUB_BLOCK = SUBLANES


def _params(sem):
    return pltpu.CompilerParams(dimension_semantics=sem, vmem_limit_bytes=VMEM_LIMIT_BYTES)


def _mm(a, b):
    return jnp.dot(a.astype(BF16), b.astype(BF16), preferred_element_type=F32)


def _mm_nt(a, b):
    return lax.dot_general(a.astype(BF16), b.astype(BF16), (((1,), (1,)), ((), ())), preferred_element_type=F32)


def _mm_tn(a, b):
    return lax.dot_general(a.astype(BF16), b.astype(BF16), (((0,), (0,)), ((), ())), preferred_element_type=F32)


def _sigmoid(x):
    return 1.0 / (1.0 + jnp.exp(-x))


def _softplus(x):
    return jnp.maximum(x, 0.0) + jnp.log(1.0 + jnp.exp(-jnp.abs(x)))


def _rms(x, gain):
    return x * lax.rsqrt(jnp.mean(x * x, axis=-1, keepdims=True) + EPS) * gain


def _gated_rms(o, gate, gain):
    return _rms(o, gain) * (gate * _sigmoid(gate))


ROW_TILE = D_MODEL // LANES
assert ROW_TILE == SUBLANES


def _store_token_tiles(ref, x):
    tm = x.shape[0]
    for j in range(ROW_TILE):
        ref[pl.ds(j, tm, stride=ROW_TILE), :] = x[:, j * LANES:(j + 1) * LANES]


def _load_token_tiles(ref):
    tm = ref.shape[0] // ROW_TILE
    return jnp.concatenate([ref[pl.ds(j, tm, stride=ROW_TILE), :] for j in range(ROW_TILE)], axis=1)


def _row_to_col(row):
    n = row.shape[1]
    r = lax.broadcasted_iota(I32, (n, n), 0)
    c = lax.broadcasted_iota(I32, (n, n), 1)
    return jnp.sum(jnp.where(r == c, jnp.broadcast_to(row, (n, n)), 0.0), axis=1, keepdims=True)


def _split2(x):
    hi = x.astype(BF16)
    return hi, (x - hi.astype(F32)).astype(BF16)


def _split3(x):
    hi = x.astype(BF16)
    r = x - hi.astype(F32)
    mid = r.astype(BF16)
    return hi, mid, (r - mid.astype(F32)).astype(BF16)


def _mm_cum(tri, x):
    hi, mid, lo = _split3(x)
    dot = lambda p: jnp.dot(tri, p, preferred_element_type=F32)
    return dot(hi) + (dot(mid) + dot(lo))


def _mm_split(a, b):
    ah, al = _split2(a)
    bh, bl = _split2(b)
    dot = lambda x, y: jnp.dot(x, y, preferred_element_type=F32)
    return dot(ah, bh) + (dot(ah, bl) + dot(al, bh))


def _ticker(fillers):
    pending = list(fillers)

    def tick(flush=False):
        while pending:
            pending.pop(0)()
            if not flush:
                break
    return tick


def _delta_chunks(qs_, ks_, vs_, gcs_, grs_, betas_, ss_, fillers=()):
    tick = _ticker(fillers)
    heads = range(len(qs_))
    c = qs_[0].shape[0]
    ti = lax.broadcasted_iota(I32, (c, c), 0)
    tj = lax.broadcasted_iota(I32, (c, c), 1)
    eye = jnp.where(ti == tj, 1.0, 0.0)
    decay = [jnp.exp(jnp.where(ti >= tj, gcs_[h] - grs_[h], NEG)) for h in heads]
    qs = [qs_[h] * (DN_DK ** -0.5) for h in heads]
    kk = [_mm_nt(ks_[h], ks_[h]) for h in heads]
    nm = [jnp.where(ti > tj, betas_[h] * kk[h] * decay[h], 0.0) for h in heads]
    t = [eye - nm[h] for h in heads]
    p = nm
    step = 2
    while step < c:
        tick()
        p = [_mm(p[h], p[h]) for h in heads]
        t = [t[h] + _mm(t[h], p[h]) for h in heads]
        step *= 2
    eg = [jnp.exp(gcs_[h]) for h in heads]
    rhs = [jnp.concatenate([betas_[h] * vs_[h], (betas_[h] * eg[h]) * ks_[h]], axis=1) for h in heads]
    tick()
    sol = [_mm(t[h], rhs[h]) for h in heads]
    tick()
    resid = [rhs[h] - (sol[h] + _mm_split(nm[h], sol[h])) for h in heads]
    tick()
    sol = [sol[h] + _mm(t[h], resid[h]) for h in heads]
    qk = [_mm_nt(qs[h], ks_[h]) * decay[h] for h in heads]
    tick()
    u = [sol[h][:, :DN_DV] - _mm(sol[h][:, DN_DV:], ss_[h]) for h in heads]
    tick()
    o = [_mm(qs[h] * eg[h], ss_[h]) + _mm(qk[h], u[h]) for h in heads]
    gl = [gcs_[h][c - 1:c] for h in heads]
    s_new = [ss_[h] * jnp.exp(gl[h]) + _mm_tn(ks_[h] * jnp.exp(gl[h] - gcs_[h]), u[h]) for h in heads]
    tick(flush=True)
    return o, s_new


def _gla_chunks(qs, ks, gs, vss, ss, m, fillers=()):
    tick = _ticker(fillers)
    probs = range(len(qs))
    c = qs[0].shape[0]
    nh = len(vss[0])
    heads = range(nh)
    dk = LANES // nh
    dlane = lax.broadcasted_iota(I32, (1, LANES), 1)
    if nh == 1:
        mask = lambda x, h: x
    else:
        hm = [jnp.where(jnp.logical_and(dlane >= h * dk, dlane < (h + 1) * dk), 1.0, 0.0) for h in heads]
        mask = lambda x, h: x * hm[h]
    gs = [gs[i] * LOG2E for i in probs]
    qeg = [qs[i] * jnp.exp2(gs[i]) for i in probs]
    o = [[_mm(mask(qeg[i], h), ss[i]) for h in heads] for i in probs]
    lane = lax.broadcasted_iota(I32, (m, c), 1)
    sub = lax.broadcasted_iota(I32, (m, c), 0)
    blocks = [[[] for _ in heads] for _ in probs]
    for blk in range(c // m):
        tick()
        r0 = blk * m
        qb = [qs[i][r0:r0 + m] for i in probs]
        gb = [gs[i][r0:r0 + m] for i in probs]
        kb = [ks[i][r0:r0 + m] for i in probs]
        if blk > 0:
            base = [gs[i][r0 - 1:r0] for i in probs]
            kt = [jnp.concatenate([ks[i][:r0] * jnp.exp2(base[i] - gs[i][:r0]), jnp.zeros((c - r0, LANES), F32)], axis=0)
                  for i in probs]
            qt = [qb[i] * jnp.exp2(gb[i] - base[i]) for i in probs]
            att = [[_mm_nt(mask(qt[i], h), kt[i]) for h in heads] for i in probs]
        else:
            att = [[jnp.zeros((m, c), F32) for _ in heads] for _ in probs]
        for j in range(m):
            keep = jnp.logical_and(lane == r0 + j, sub >= j)
            for i in probs:
                prod = qb[i] * (kb[i][j:j + 1] * jnp.exp2(gb[i] - gb[i][j:j + 1]))
                for h in heads:
                    att[i][h] = jnp.where(keep, jnp.sum(mask(prod, h), axis=1, keepdims=True), att[i][h])
        for i in probs:
            for h in heads:
                blocks[i][h].append(att[i][h])
    tick(flush=True)
    s_new = []
    for i in probs:
        gl = gs[i][c - 1:c]
        kg = ks[i] * jnp.exp2(gl - gs[i])
        s_i = ss[i] * _row_to_col(jnp.exp2(gl))
        for h in heads:
            v = vss[i][h]() if callable(vss[i][h]) else vss[i][h]
            att = blocks[i][h][0] if len(blocks[i][h]) == 1 else jnp.concatenate(blocks[i][h], axis=0)
            o[i][h] = o[i][h] + _mm(att, v)
            s_i = s_i + _mm_tn(mask(kg, h), v)
        s_new.append(s_i)
    return o, s_new


def _tri_incl(c):
    r = lax.broadcasted_iota(I32, (c, c), 0)
    cc = lax.broadcasted_iota(I32, (c, c), 1)
    return jnp.where(r >= cc, 1.0, 0.0).astype(BF16)


def _combine_kernel(pos_ref, *refs, final, tiles_a):
    if final:
        x_ref, info_ref, gain_ref, ys_hbm, outa_ref, outb_ref, ybuf, sem = refs
    else:
        x_ref, info_ref, ys_hbm, out_ref, ybuf, sem = refs
    i = pl.program_id(0)
    n = pl.num_programs(0)
    tm = x_ref.shape[0]
    slot = i % 2

    def gather_start(tile, s):
        def body(r, carry):
            for k in range(TOP_K):
                src = pl.multiple_of(pos_ref[TOP_K * (tile * tm + r) + k] * ROW_TILE, ROW_TILE)
                dst = pl.multiple_of(r * ROW_TILE, ROW_TILE)
                pltpu.make_async_copy(ys_hbm.at[pl.ds(src, ROW_TILE)], ybuf.at[s, k, pl.ds(dst, ROW_TILE)],
                                      sem.at[s]).start(priority=k % 2)
            return carry
        lax.fori_loop(0, tm, body, 0, unroll=8)

    @pl.when(i == 0)
    def _():
        gather_start(0, 0)

    @pl.when(i + 1 < n)
    def _():
        gather_start(i + 1, 1 - slot)

    for k in range(TOP_K):
        pltpu.make_async_copy(ys_hbm.at[pl.ds(0, tm * ROW_TILE)], ybuf.at[slot, k], sem.at[slot]).wait()
    info = info_ref[...]
    y0, y1 = (_load_token_tiles(ybuf.at[slot, k]) for k in range(TOP_K))
    x = x_ref[...] + (info[:, INFO_W:INFO_W + 1] * y0 + info[:, INFO_W + 1:INFO_W + 2] * y1)
    if final:
        h = _rms(x, gain_ref[...])

        @pl.when(i < tiles_a)
        def _():
            outa_ref[...] = h

        @pl.when(i >= tiles_a)
        def _():
            outb_ref[...] = h
    else:
        out_ref[...] = x


def _combine(x, info, pos, ys, final_gain=None, n_a=None):
    n, d = x.shape
    tm = TOKEN_TILE
    final = final_gain is not None
    tiles_a = n_a // tm if final else None
    row = lambda width: pl.BlockSpec((tm, width), lambda i, pos: (i, 0))
    in_specs = [row(d), row(LANES)]
    args = [x, info]
    if final:
        in_specs.append(pl.BlockSpec((1, d), lambda i, pos: (0, 0)))
        args.append(final_gain)
        out_specs = [pl.BlockSpec((tm, d), lambda i, pos: (jnp.minimum(i, tiles_a - 1), 0)),
                     pl.BlockSpec((tm, d), lambda i, pos: (jnp.maximum(i - tiles_a, 0), 0))]
        out_shape = [jax.ShapeDtypeStruct((n_a, d), F32), jax.ShapeDtypeStruct((n - n_a, d), F32)]
    else:
        out_specs = row(d)
        out_shape = jax.ShapeDtypeStruct((n, d), F32)
    in_specs.append(pl.BlockSpec(memory_space=pl.ANY))
    args.append(ys)
    grid_spec = pltpu.PrefetchScalarGridSpec(
        num_scalar_prefetch=1, grid=(n // tm,), in_specs=in_specs, out_specs=out_specs,
        scratch_shapes=[pltpu.VMEM((2, TOP_K, tm * ROW_TILE, LANES), F32), pltpu.SemaphoreType.DMA((2,))])
    return pl.pallas_call(
        functools.partial(_combine_kernel, final=final, tiles_a=tiles_a),
        grid_spec=grid_spec,
        out_shape=out_shape,
        compiler_params=_params(("arbitrary",)),
        name="moe_combine",
    )(pos, *args)


PROJ_CHUNK = 256


def _project_rows(x_refs, gain_ref, w_ref, p_scr, h_scr, c, first_cols):
    x = jnp.concatenate([r[...] for r in x_refs], axis=0) if len(x_refs) > 1 else x_refs[0][...]
    h_scr[...] = _rms(x, gain_ref[...]).astype(BF16)

    def piece(c0, c1):
        def run():
            p_scr[:, c0:c1] = jnp.dot(h_scr[...], w_ref[:, c0:c1], preferred_element_type=F32)
        return run

    piece(0, first_cols)()
    cols = p_scr.shape[1]
    fillers = [piece(c0, min(c0 + PROJ_CHUNK, cols)) for c0 in range(first_cols, cols, PROJ_CHUNK)]
    return [p_scr.at[pl.ds(b * c, c)] for b in range(len(x_refs))], fillers


def _ab_mixer_kernel(*refs, c, nseq, zero_init):
    x_refs, refs = refs[:nseq], refs[nseq:]
    if zero_init:
        gain_ref, win_ref, cw_ref, prm_ref, gkw_ref, gkb_ref, o_ref, conv_ref, sdn_ref, sgla_ref, p_scr, h_scr = refs
    else:
        (gain_ref, win_ref, cw_ref, prm_ref, gkw_ref, gkb_ref, conv_in, sdn_in, sgla_in,
         o_ref, conv_ref, sdn_ref, sgla_ref, p_scr, h_scr) = refs

    @pl.when(pl.program_id(1) == 0)
    def _():
        if zero_init:
            conv_ref[...] = jnp.zeros_like(conv_ref)
            sdn_ref[...] = jnp.zeros_like(sdn_ref)
            sgla_ref[...] = jnp.zeros_like(sgla_ref)
        else:
            conv_ref[...] = conv_in[...]
            sdn_ref[...] = sdn_in[...]
            sgla_ref[...] = sgla_in[...]

    p_refs, fillers = _project_rows(x_refs, gain_ref, win_ref, p_scr, h_scr, c, AB_Z)

    m = min(GLA_SUB_BLOCK, c)
    tri = _tri_incl(c)
    cw = cw_ref[...]
    a_log, dt_bias = prm_ref[0:1], prm_ref[1:2]
    dn_gain, gla_gain = prm_ref[2:3], prm_ref[3:4]
    n_pairs = GLA_HEADS // GLA_PACK
    conv_old = [conv_ref[b] for b in range(nseq)]
    sdn_old = [[sdn_ref[b, h] for h in range(DN_HEADS)] for b in range(nseq)]
    sgla_old = [[sgla_ref[b, pr] for pr in range(n_pairs)] for b in range(nseq)]
    dn = {key: [] for key in ("q", "k", "v", "gc", "gr", "beta", "s")}
    miscs = []
    for b in range(nseq):
        p_ref = p_refs[b]
        u = p_ref[:, AB_U:AB_U + CONV_CH]
        ucat = jnp.concatenate([conv_old[b], u], axis=0)
        acc = u * cw[CONV_W - 1:CONV_W]
        for j in range(1, CONV_W):
            acc = acc + pltpu.roll(ucat, j, 0)[SUBLANES:SUBLANES + c] * cw[CONV_W - 1 - j:CONV_W - j]
        conv_ref[b] = ucat[c:c + SUBLANES]
        qkv = acc * _sigmoid(acc)

        misc = p_ref[:, AB_MISC:AB_MISC + LANES]
        miscs.append(misc)
        g_all = -jnp.exp(a_log) * _softplus(misc + dt_bias)
        beta_all = _sigmoid(misc)
        gcum = _mm_cum(tri, g_all)
        gcum_t = jnp.concatenate([gcum, jnp.zeros((LANES - c, LANES), F32)], axis=0).T
        for h in range(DN_HEADS):
            lo = h * DN_DK
            q = qkv[:, lo:lo + DN_DK]
            k = qkv[:, DN_QK + lo:DN_QK + lo + DN_DK]
            dn["q"].append(q * lax.rsqrt(jnp.sum(q * q, axis=-1, keepdims=True) + L2_EPS))
            dn["k"].append(k * lax.rsqrt(jnp.sum(k * k, axis=-1, keepdims=True) + L2_EPS))
            dn["v"].append(qkv[:, 2 * DN_QK + h * DN_DV:2 * DN_QK + (h + 1) * DN_DV])
            dn["gc"].append(gcum[:, MISC_A + h:MISC_A + h + 1])
            dn["gr"].append(gcum_t[MISC_A + h:MISC_A + h + 1, :c])
            dn["beta"].append(beta_all[:, MISC_B + h:MISC_B + h + 1])
            dn["s"].append(sdn_old[b][h])
    os_, ss_ = _delta_chunks(dn["q"], dn["k"], dn["v"], dn["gc"], dn["gr"], dn["beta"], dn["s"], fillers)
    for b in range(nseq):
        for h in range(DN_HEADS):
            i = b * DN_HEADS + h
            sdn_ref[b, h] = ss_[i]
            z = p_refs[b][:, AB_Z + h * DN_DV:AB_Z + (h + 1) * DN_DV]
            o_ref[b, :, h * DN_DV:(h + 1) * DN_DV] = _gated_rms(os_[i], z, dn_gain).astype(o_ref.dtype)

    gl = {key: [] for key in ("q", "k", "g", "v", "s")}
    for b in range(nseq):
        p_ref = p_refs[b]
        logits = _mm_split(miscs[b], gkw_ref[...]) + gkb_ref[...]
        logf = -_softplus(-logits) * (1.0 / GLA_NORMALIZER)
        gall = _mm_cum(tri, logf)
        for pair in range(n_pairs):
            lo = pair * LANES
            gl["q"].append(p_ref[:, AB_GQ + lo:AB_GQ + lo + LANES] * (GLA_DK ** -0.5))
            gl["k"].append(p_ref[:, AB_GK + lo:AB_GK + lo + LANES])
            gl["g"].append(gall[:, lo:lo + LANES])
            gl["v"].append([p_ref[:, AB_GV + h * GLA_DV:AB_GV + (h + 1) * GLA_DV]
                            for h in range(pair * GLA_PACK, (pair + 1) * GLA_PACK)])
            gl["s"].append(sgla_old[b][pair])
    os_, ss_ = _gla_chunks(gl["q"], gl["k"], gl["g"], gl["v"], gl["s"], m)
    for b in range(nseq):
        for pair in range(n_pairs):
            i = b * n_pairs + pair
            sgla_ref[b, pair] = ss_[i]
            for hh in range(GLA_PACK):
                h = pair * GLA_PACK + hh
                gate = p_refs[b][:, AB_GG + h * GLA_DV:AB_GG + (h + 1) * GLA_DV]
                o_ref[b, :, DN_V + h * GLA_DV:DN_V + (h + 1) * GLA_DV] = \
                    _gated_rms(os_[i][hh], gate, gla_gain).astype(o_ref.dtype)


def _seq_specs(cols, row0, c, nchunk, nseq):
    blk0 = row0 // c
    return [pl.BlockSpec((c, cols), lambda g, i, s=s: (blk0 + (g * nseq + s) * nchunk + i, 0)) for s in range(nseq)]


def _ab_mixer(x, row0, nb, t, nseq, gain, w_in, cw, prm, gkw, gkb, states):
    c = min(CHUNK, t)
    nchunk = t // c
    n_groups = nb // nseq
    zero_init = states is None
    st_map3 = lambda g, i: (g, 0, 0)
    st_map4 = lambda g, i: (g, 0, 0, 0)
    const = lambda shape: pl.BlockSpec(shape, lambda g, i: (0,) * len(shape))
    consts = [gain, w_in, cw, prm, gkw, gkb]
    in_specs = _seq_specs(D_MODEL, row0, c, nchunk, nseq) + [const(a.shape) for a in consts]
    args = [x] * nseq + consts
    gla_packed = (GLA_HEADS // GLA_PACK, GLA_PACK * GLA_DK, GLA_DV)
    st_specs = [pl.BlockSpec((nseq, SUBLANES, CONV_CH), st_map3),
                pl.BlockSpec((nseq, DN_HEADS, DN_DK, DN_DV), st_map4),
                pl.BlockSpec((nseq,) + gla_packed, st_map4)]
    if not zero_init:
        conv8, sdn, sgla = states
        in_specs += st_specs
        args += [conv8, sdn, sgla.reshape((nb,) + gla_packed)]
    out_shape = [jax.ShapeDtypeStruct((nb, t, D_MODEL), BF16),
                 jax.ShapeDtypeStruct((nb, SUBLANES, CONV_CH), F32),
                 jax.ShapeDtypeStruct((nb, DN_HEADS, DN_DK, DN_DV), F32),
                 jax.ShapeDtypeStruct((nb,) + gla_packed, F32)]
    out_specs = [pl.BlockSpec((nseq, c, D_MODEL), lambda g, i: (g, i, 0))] + st_specs
    o, conv_out, sdn_out, sgla_out = pl.pallas_call(
        functools.partial(_ab_mixer_kernel, c=c, nseq=nseq, zero_init=zero_init),
        grid=(n_groups, nchunk),
        in_specs=in_specs,
        out_specs=out_specs,
        out_shape=out_shape,
        scratch_shapes=[pltpu.VMEM((nseq * c, AB_COLS), F32), pltpu.VMEM((nseq * c, D_MODEL), BF16)],
        compiler_params=_params(("parallel", "arbitrary")),
        name="ab_mixer",
    )(*args)
    return o.reshape(nb * t, D_MODEL), conv_out, sdn_out, sgla_out.reshape(nb, GLA_HEADS, GLA_DK, GLA_DV)


def _hgrn_mixer_kernel(*refs, c, nseq, zero_init, layer):
    x_refs, refs = refs[:nseq], refs[nseq:]
    if zero_init:
        mixgain_ref, win_ref, lb_ref, gain_ref, o_ref, s_ref, p_scr, h_scr = refs
    else:
        mixgain_ref, win_ref, lb_ref, gain_ref, s_in, o_ref, s_ref, p_scr, h_scr = refs
    p_refs, fillers = _project_rows(x_refs, mixgain_ref, win_ref, p_scr, h_scr, c, 2 * HG_F)

    @pl.when(pl.program_id(1) == 0)
    def _():
        if zero_init:
            s_ref[...] = jnp.zeros_like(s_ref)
        else:
            s_ref[...] = s_in[...]

    raw = lb_ref[...]
    e = jnp.exp(raw - jnp.max(raw, axis=0, keepdims=True))
    sm = e / jnp.sum(e, axis=0, keepdims=True)
    cum = sm[0:1]
    for l in range(1, layer + 1):
        cum = cum + sm[l:l + 1]
    lb = cum - sm[0:1]

    m = min(GLA_SUB_BLOCK, c)
    tri = _tri_incl(c)
    gain = gain_ref[...]
    pr = {key: [] for key in ("q", "k", "g", "v", "s")}
    for b in range(nseq):
        p_ref = p_refs[b]
        fr = p_ref[:, HG_F:2 * HG_F]
        logf = jnp.log(lb + (1.0 - lb) * _sigmoid(fr))
        kall = (1.0 - lb) * _sigmoid(-fr)
        gall = _mm_cum(tri, logf)
        for h in range(HG_HEADS):
            lo = h * HG_DK
            pr["q"].append(p_ref[:, lo:lo + HG_DK] * (HG_DK ** -0.5))
            pr["k"].append(kall[:, lo:lo + HG_DK])
            pr["g"].append(gall[:, lo:lo + HG_DK])
            pr["v"].append([functools.partial(lambda r, h: r[:, 2 * HG_F + h * HG_DV:2 * HG_F + (h + 1) * HG_DV], p_ref, h)])
            pr["s"].append(s_ref[b, h])
    os_, ss_ = _gla_chunks(pr["q"], pr["k"], pr["g"], pr["v"], pr["s"], m, fillers)
    for b in range(nseq):
        for h in range(HG_HEADS):
            i = b * HG_HEADS + h
            s_ref[b, h] = ss_[i]
            gate = p_refs[b][:, 2 * HG_F + D_MODEL + h * HG_DV:2 * HG_F + D_MODEL + (h + 1) * HG_DV]
            o_ref[b, :, h * HG_DV:(h + 1) * HG_DV] = _gated_rms(os_[i][0], gate, gain).astype(o_ref.dtype)


def _hgrn_mixer(x, row0, nb, t, nseq, mix_gain, w_in, lb, gain, state, layer):
    c = min(CHUNK, t)
    nchunk = t // c
    zero_init = state is None
    const = lambda shape: pl.BlockSpec(shape, lambda g, i: (0,) * len(shape))
    st_spec = pl.BlockSpec((nseq, HG_HEADS, HG_DK, HG_DV), lambda g, i: (g, 0, 0, 0))
    consts = [mix_gain, w_in, lb, gain]
    in_specs = _seq_specs(D_MODEL, row0, c, nchunk, nseq) + [const(a.shape) for a in consts]
    args = [x] * nseq + consts
    if not zero_init:
        in_specs.append(st_spec)
        args.append(state)
    o, s_out = pl.pallas_call(
        functools.partial(_hgrn_mixer_kernel, c=c, nseq=nseq, zero_init=zero_init, layer=layer),
        grid=(nb // nseq, nchunk),
        in_specs=in_specs,
        out_specs=[pl.BlockSpec((nseq, c, D_MODEL), lambda g, i: (g, i, 0)), st_spec],
        out_shape=[jax.ShapeDtypeStruct((nb, t, D_MODEL), BF16),
                   jax.ShapeDtypeStruct((nb, HG_HEADS, HG_DK, HG_DV), F32)],
        scratch_shapes=[pltpu.VMEM((nseq * c, w_in.shape[1]), F32), pltpu.VMEM((nseq * c, D_MODEL), BF16)],
        compiler_params=_params(("parallel", "arbitrary")),
        name="hgrn_mixer",
    )(*args)
    return o.reshape(nb * t, D_MODEL), s_out


def _outproj_router_kernel(*refs, tiles_a, split_x):
    if split_x:
        oa_ref, ob_ref, xa_ref, xb_ref = refs[:4]
        refs = refs[4:]
    else:
        oa_ref, ob_ref, x_ref = refs[:3]
        refs = refs[3:]
    wout_ref, gain_ref, wr_ref, br_ref, x1_ref, h_ref, info_ref, cnt_ref, run_ref = refs
    first = pl.program_id(0) < tiles_a

    @pl.when(pl.program_id(0) == 0)
    def _():
        run_ref[...] = jnp.zeros_like(run_ref)

    o = jnp.where(first, oa_ref[...], ob_ref[...])
    x = jnp.where(first, xa_ref[...], xb_ref[...]) if split_x else x_ref[...]
    x1 = x + jnp.dot(o, wout_ref[...], preferred_element_type=F32)
    x1_ref[...] = x1
    h = _rms(x1, gain_ref[...])
    _store_token_tiles(h_ref, h)
    logit = _mm_split(h, wr_ref[...]) + br_ref[...]
    lane = lax.broadcasted_iota(I32, logit.shape, 1)
    far = jnp.int32(LANES)
    lg = jnp.where(lane < N_GROUPS, logit, NEG)
    mg = jnp.max(lg, axis=-1, keepdims=True)
    p_g = 1.0 / jnp.sum(jnp.exp(lg - mg), axis=-1, keepdims=True)
    g_top = jnp.min(jnp.where(lg == mg, lane, far), axis=-1, keepdims=True)
    lo = N_GROUPS + g_top * EXPERTS_PER_GROUP
    sel = jnp.logical_and(lane >= lo, lane < lo + EXPERTS_PER_GROUP)
    le = jnp.where(sel, logit, NEG)
    ee = jnp.exp(le - jnp.max(le, axis=-1, keepdims=True))
    pe = jnp.where(sel, ee / jnp.sum(ee, axis=-1, keepdims=True), -1.0)
    p1 = jnp.max(pe, axis=-1, keepdims=True)
    i1 = jnp.min(jnp.where(pe == p1, lane, far), axis=-1, keepdims=True)
    pe2 = jnp.where(lane == i1, -1.0, pe)
    p2 = jnp.max(pe2, axis=-1, keepdims=True)
    i2 = jnp.min(jnp.where(pe2 == p2, lane, far), axis=-1, keepdims=True)
    denom = p1 + p2
    w1 = p_g * (p1 / denom)
    w2 = p_g * (p2 / denom)
    tm = logit.shape[0]
    oh1 = jnp.where(lane == i1, 1.0, 0.0)
    oh2 = jnp.where(lane == i2, 1.0, 0.0)
    both = oh1 + oh2
    tr = lax.broadcasted_iota(I32, (tm, tm), 0)
    tc = lax.broadcasted_iota(I32, (tm, tm), 1)
    before = run_ref[...] + _mm(jnp.where(tr > tc, 1.0, 0.0), both)
    r1 = jnp.sum(oh1 * before, axis=-1, keepdims=True)
    r2 = jnp.sum(oh2 * before, axis=-1, keepdims=True)
    run = run_ref[...] + jnp.sum(both, axis=0, keepdims=True)
    run_ref[...] = run
    cnt_ref[...] = jnp.broadcast_to(run, cnt_ref.shape)
    vals = ((i1 - N_GROUPS).astype(F32), (i2 - N_GROUPS).astype(F32), w1, w2, r1, r2)
    info = jnp.zeros_like(logit)
    for idx, val in enumerate(vals):
        info = jnp.where(lane == idx, val, info)
    info_ref[...] = info


def _outproj_router(o_a, o_b, xs, wout, gain, wr, br):
    n_a, d = o_a.shape
    n = n_a + o_b.shape[0]
    tm = TOKEN_TILE
    tiles_a = n_a // tm
    row = lambda w: pl.BlockSpec((tm, w), lambda i: (i, 0))
    seg_a = pl.BlockSpec((tm, d), lambda i: (jnp.minimum(i, tiles_a - 1), 0))
    seg_b = pl.BlockSpec((tm, d), lambda i: (jnp.maximum(i - tiles_a, 0), 0))
    const = lambda shape: pl.BlockSpec(shape, lambda i: (0,) * len(shape))
    split_x = len(xs) == 2
    return pl.pallas_call(
        functools.partial(_outproj_router_kernel, tiles_a=tiles_a, split_x=split_x),
        grid=(n // tm,),
        in_specs=[seg_a, seg_b] + ([seg_a, seg_b] if split_x else [row(d)])
        + [const((d, d)), const((1, d)), const((d, LANES)), const((1, LANES))],
        out_specs=[row(d), pl.BlockSpec((tm * ROW_TILE, LANES), lambda i: (i, 0)), row(LANES), const((SUBLANES, LANES))],
        out_shape=[jax.ShapeDtypeStruct((n, d), F32), jax.ShapeDtypeStruct((n * ROW_TILE, LANES), F32),
                   jax.ShapeDtypeStruct((n, LANES), F32), jax.ShapeDtypeStruct((SUBLANES, LANES), F32)],
        scratch_shapes=[pltpu.VMEM((1, LANES), F32)],
        compiler_params=_params(("arbitrary",)),
        name="outproj_router",
    )(o_a, o_b, *xs, wout, gain, wr, br)


def _dispatch_kernel(pos_ref, lo_ref, hi_ref, h_ref, xs_hbm, buf, zrow, sem, zsem):
    i = pl.program_id(0)
    n = pl.num_programs(0)
    tm = h_ref.shape[0] // ROW_TILE
    blk = tm * ROW_TILE
    slot = i % 2
    token = lambda r: pl.ds(pl.multiple_of(r * ROW_TILE, ROW_TILE), ROW_TILE)

    def tile_wait(s):
        for _ in range(TOP_K):
            pltpu.make_async_copy(buf.at[s], xs_hbm.at[pl.ds(0, blk)], sem.at[s]).wait()

    zblk = zrow.shape[0]

    def pad_rows(fn, tile_fn):
        for e in range(N_EXPERTS):
            lax.fori_loop(lo_ref[e], hi_ref[e], fn, 0)
        lax.fori_loop(hi_ref[N_EXPERTS - 1] // (zblk // ROW_TILE), xs_hbm.shape[0] // zblk, tile_fn, 0)

    def pad_start(r, carry):
        pltpu.make_async_copy(zrow.at[token(0)], xs_hbm.at[token(r)], zsem.at[0]).start()
        return carry

    def pad_wait(r, carry):
        pltpu.make_async_copy(zrow.at[token(0)], xs_hbm.at[token(0)], zsem.at[0]).wait()
        return carry

    def pad_tile_start(t, carry):
        pltpu.make_async_copy(zrow, xs_hbm.at[pl.ds(pl.multiple_of(t * zblk, zblk), zblk)], zsem.at[0]).start()
        return carry

    def pad_tile_wait(t, carry):
        pltpu.make_async_copy(zrow, xs_hbm.at[pl.ds(0, zblk)], zsem.at[0]).wait()
        return carry

    @pl.when(i == 0)
    def _():
        zrow[...] = jnp.zeros_like(zrow)
        pad_rows(pad_start, pad_tile_start)

    @pl.when(i >= 2)
    def _():
        tile_wait(slot)
    buf[slot] = h_ref[...]

    def body(r, carry):
        for k in range(TOP_K):
            row = pos_ref[TOP_K * (i * tm + r) + k]
            pltpu.make_async_copy(buf.at[slot, token(r)], xs_hbm.at[token(row)], sem.at[slot]).start(priority=k % 2)
        return carry
    lax.fori_loop(0, tm, body, 0, unroll=8)

    @pl.when(i == n - 1)
    def _():
        tile_wait(slot)

        @pl.when(n > 1)
        def _():
            tile_wait(1 - slot)
        pad_rows(pad_wait, pad_tile_wait)


def _dispatch(h, pos, pad_lo, pad_hi, n_rows):
    blk = TOKEN_TILE * ROW_TILE
    n = h.shape[0] // ROW_TILE
    grid_spec = pltpu.PrefetchScalarGridSpec(
        num_scalar_prefetch=3,
        grid=(n // TOKEN_TILE,),
        in_specs=[pl.BlockSpec((blk, LANES), lambda i, pos, lo, hi: (i, 0))],
        out_specs=pl.BlockSpec(memory_space=pl.ANY),
        scratch_shapes=[pltpu.VMEM((2, blk, LANES), F32), pltpu.VMEM((MOE_TILE * ROW_TILE, LANES), F32),
                        pltpu.SemaphoreType.DMA((2,)), pltpu.SemaphoreType.DMA((1,))],
    )
    return pl.pallas_call(
        _dispatch_kernel,
        grid_spec=grid_spec,
        out_shape=jax.ShapeDtypeStruct((n_rows * ROW_TILE, LANES), F32),
        compiler_params=_params(("arbitrary",)),
        name="moe_dispatch",
    )(pos, pad_lo, pad_hi, h)


def _moe_kernel(te_ref, nact_ref, x_ref, w1_ref, w3_ref, w2_ref, y_ref, w1b, w3b, w2b):
    i = pl.program_id(0)

    @pl.when(i < nact_ref[0])
    def _():
        @pl.when(jnp.logical_or(i == 0, te_ref[i] != te_ref[jnp.maximum(i - 1, 0)]))
        def _():
            w1b[...] = w1_ref[0, 0].astype(BF16)
            w3b[...] = w3_ref[0, 0].astype(BF16)
            w2b[...] = w2_ref[0, 0].astype(BF16)

        xb = _load_token_tiles(x_ref).astype(BF16)
        h1 = jnp.dot(xb, w1b[...], preferred_element_type=F32)
        h3 = jnp.dot(xb, w3b[...], preferred_element_type=F32)
        act = (h1 * _sigmoid(h1)) * h3
        _store_token_tiles(y_ref, jnp.dot(act.astype(BF16), w2b[...], preferred_element_type=F32))

    @pl.when(i >= nact_ref[0])
    def _():
        y_ref[...] = jnp.zeros_like(y_ref)


def _moe(xs, te, nact, w1, w3, w2, layer):
    d, f = w1.shape[2], w1.shape[3]
    tm = MOE_TILE
    blk = tm * ROW_TILE
    n_tiles = xs.shape[0] // blk
    last = lambda i, nact: jnp.maximum(jnp.minimum(i, nact[0] - 1), 0)
    grid_spec = pltpu.PrefetchScalarGridSpec(
        num_scalar_prefetch=2,
        grid=(n_tiles,),
        in_specs=[
            pl.BlockSpec((blk, LANES), lambda i, te, nact: (last(i, nact), 0)),
            pl.BlockSpec((1, 1, d, f), lambda i, te, nact: (layer, te[i], 0, 0)),
            pl.BlockSpec((1, 1, d, f), lambda i, te, nact: (layer, te[i], 0, 0)),
            pl.BlockSpec((1, 1, f, d), lambda i, te, nact: (layer, te[i], 0, 0)),
        ],
        out_specs=pl.BlockSpec((blk, LANES), lambda i, te, nact: (i, 0)),
        scratch_shapes=[pltpu.VMEM((d, f), BF16), pltpu.VMEM((d, f), BF16), pltpu.VMEM((f, d), BF16)],
    )
    return pl.pallas_call(
        _moe_kernel,
        grid_spec=grid_spec,
        out_shape=jax.ShapeDtypeStruct(xs.shape, F32),
        compiler_params=_params(("arbitrary",)),
        name="moe_experts",
    )(te, nact, xs, w1, w3, w2)


def _moe_rows(n):
    return ((TOP_K * n + N_EXPERTS * (MOE_TILE - 1)) // MOE_TILE) * MOE_TILE


def _route_tables(info, counts_row, n):
    tm = MOE_TILE
    n_tiles = _moe_rows(n) // tm
    ids = jnp.arange(N_EXPERTS, dtype=I32)
    counts = counts_row[0, N_GROUPS:N_GROUPS + N_EXPERTS].astype(I32)
    padded = ((counts + tm - 1) // tm) * tm
    cum = jnp.cumsum(padded)
    off = cum - padded
    e = info[:, INFO_E:INFO_E + TOP_K].astype(I32)
    rank = info[:, INFO_R:INFO_R + TOP_K].astype(I32)
    pos = (jnp.sum(jnp.where(e[:, :, None] == ids, off, 0), axis=-1) + rank).reshape(TOP_K * n)
    tile_start = jnp.arange(n_tiles, dtype=I32) * tm
    nact = cum[-1] // tm
    te_raw = jnp.sum((cum[None, :] <= tile_start[:, None]).astype(I32), axis=1)
    te_last = jnp.sum((cum <= (nact - 1) * tm).astype(I32))
    te = jnp.where(tile_start < cum[-1], jnp.minimum(te_raw, N_EXPERTS - 1), te_last)
    return pos, off + counts, cum, te, nact.reshape(1)


def _pad_lanes(a, width=LANES):
    return jnp.pad(a, [(0, 0)] * (a.ndim - 1) + [(0, width - a.shape[-1])])


def _seq_group(nb, want):
    g = want
    while nb % g:
        g //= 2
    return g


PROMPT_SEQS_PER_STEP = 4
SAMPLE_SEQS_PER_STEP = 8


def kernel(x_prompt, x_sample, state_conv_ab, state_delta_ab, state_gla_ab, state_hgrn_c, mix_norm, ab_w_in, ab_conv_w, ab_a_log, ab_dt_bias, ab_dn_norm, ab_gk_w2, ab_gk_b, ab_gla_norm, ab_w_out, c_w_in, c_lower_bounds, c_norm, c_w_out, ffn_norm, moe_w_group, moe_b_group, moe_w_expert, moe_b_expert, moe_w1, moe_w3, moe_w2, final_norm):
    bp, tp, d = x_prompt.shape
    bs, ts, _ = x_sample.shape
    n_p, n_s = bp * tp, bs * ts
    n = n_p + n_s
    depth = mix_norm.shape[0]
    xs_res = (x_prompt.reshape(n_p, d), x_sample.reshape(n_s, d))
    pg, sg = _seq_group(bp, PROMPT_SEQS_PER_STEP), _seq_group(bs, SAMPLE_SEQS_PER_STEP)

    info = pos = ys = None
    conv_p, delta_p, gla_p, hgrn_p = [], [], [], []
    conv_s, delta_s, gla_s, hgrn_s = [], [], [], []
    for layer in range(depth):
        j = layer // 2
        gain = mix_norm[layer].reshape(1, d)
        if layer % 2 == 0:
            u_w, a_w, b_w, z_w, gq_w, gk_w, gv_w, lr_w, gg_w = jnp.split(ab_w_in[j], _split_points(AB_SPLITS), axis=1)
            w_in = jnp.concatenate([u_w, _pad_lanes(jnp.concatenate([a_w, b_w, lr_w], axis=1)), z_w, gq_w, gk_w, gv_w, gg_w],
                                   axis=1).astype(BF16)
        else:
            w_in = c_w_in[j].astype(BF16)
        if ys is not None:
            xs_res = (_combine(xs_res[0], info, pos, ys),)
        (x_p, off_p), (x_s, off_s) = ((xs_res[0], 0), (xs_res[-1], 0 if len(xs_res) == 2 else n_p))

        if layer % 2 == 0:
            prm = jnp.concatenate([_pad_lanes(ab_a_log[j].reshape(1, -1)), _pad_lanes(ab_dt_bias[j].reshape(1, -1)),
                                   ab_dn_norm[j].reshape(1, -1), ab_gla_norm[j].reshape(1, -1),
                                   jnp.zeros((4, LANES), F32)], axis=0)
            gkw = jnp.zeros((LANES, GLA_K), F32).at[MISC_LR:MISC_LR + GLA_RANK].set(ab_gk_w2[j])
            gkb = ab_gk_b[j].reshape(1, GLA_K)
            cw = ab_conv_w[j]
            o_p, c8, sd, sgl = _ab_mixer(x_p, off_p, bp, tp, pg, gain, w_in, cw, prm, gkw, gkb, None)
            conv_p.append(c8[:, SUBLANES - (CONV_W - 1):])
            delta_p.append(sd)
            gla_p.append(sgl)
            conv8 = jnp.pad(state_conv_ab[j], ((0, 0), (SUBLANES - (CONV_W - 1), 0), (0, 0)))
            o_s, c8, sd, sgl = _ab_mixer(x_s, off_s, bs, ts, sg, gain, w_in, cw, prm, gkw, gkb,
                                         (conv8, state_delta_ab[j], state_gla_ab[j]))
            conv_s.append(c8[:, SUBLANES - (CONV_W - 1):])
            delta_s.append(sd)
            gla_s.append(sgl)
            w_out = ab_w_out[j]
        else:
            hg_gain = c_norm[j].reshape(1, -1)
            o_p, sh = _hgrn_mixer(x_p, off_p, bp, tp, pg, gain, w_in, c_lower_bounds, hg_gain, None, layer)
            hgrn_p.append(sh)
            o_s, sh = _hgrn_mixer(x_s, off_s, bs, ts, sg, gain, w_in, c_lower_bounds, hg_gain, state_hgrn_c[j], layer)
            hgrn_s.append(sh)
            w_out = c_w_out[j]

        wr = _pad_lanes(jnp.concatenate([moe_w_group[layer], moe_w_expert[layer]], axis=1))
        br = _pad_lanes(jnp.concatenate([moe_b_group[layer], moe_b_expert[layer]]).reshape(1, -1))
        x, hn, info, counts = _outproj_router(o_p, o_s, xs_res, w_out.astype(BF16), ffn_norm[layer].reshape(1, d), wr, br)
        xs_res = (x,)
        pos, pad_lo, pad_hi, te, nact = _route_tables(info, counts, n)
        xsort = _dispatch(hn, pos, pad_lo, pad_hi, _moe_rows(n))
        ys = _moe(xsort, te, nact, moe_w1, moe_w3, moe_w2, layer)

    y_p, y_s = _combine(xs_res[0], info, pos, ys, final_gain=final_norm.reshape(1, d), n_a=n_p)
    y_prompt = y_p.reshape(bp, tp, d)
    y_sample = y_s.reshape(bs, ts, d)
    return (y_prompt, y_sample, jnp.stack(conv_p), jnp.stack(delta_p), jnp.stack(gla_p), jnp.stack(hgrn_p),
            jnp.stack(conv_s), jnp.stack(delta_s), jnp.stack(gla_s), jnp.stack(hgrn_s))


def _split_points(sizes):
    pts, acc = [], 0
    for s in sizes[:-1]:
        acc += s
        pts.append(acc)
    return pts
```

```python
import functools

import jax
import jax.numpy as jnp
from jax import lax
from jax.experimental import pallas as pl
from jax.experimental.pallas import tpu as pltpu

F32 = jnp.float32
BF16 = jnp.bfloat16
I32 = jnp.int32

D_MODEL = 1024
EPS = 1e-6
CHUNK = 64
CONV_W = 4
DN_HEADS, DN_DK, DN_DV = 4, 128, 128
DN_QK = DN_HEADS * DN_DK
DN_V = DN_HEADS * DN_DV
CONV_CH = 2 * DN_QK + DN_V
GLA_HEADS, GLA_DK, GLA_DV = 4, 64, 128
GLA_K = GLA_HEADS * GLA_DK
GLA_V = GLA_HEADS * GLA_DV
GLA_RANK = 16
GLA_NORMALIZER = 16.0
GLA_PACK = 128 // GLA_DK
AB_SPLITS = (CONV_CH, DN_HEADS, DN_HEADS, DN_V, GLA_K, GLA_K, GLA_V, GLA_RANK, GLA_V)
HG_HEADS, HG_DK, HG_DV = 8, 128, 128
HG_F = HG_HEADS * HG_DK
N_GROUPS, EXPERTS_PER_GROUP = 4, 8
N_EXPERTS = N_GROUPS * EXPERTS_PER_GROUP
TOP_K = 2
L2_EPS = 1e-6

LANES = 128
SUBLANES = 8
VMEM_LIMIT_BYTES = 56 * 1024 * 1024

AB_U, AB_MISC, AB_Z, AB_GQ, AB_GK, AB_GV, AB_GG = 0, 1536, 1664, 2176, 2432, 2688, 3200
AB_COLS = AB_GG + GLA_V
MISC_A, MISC_B, MISC_LR = 0, DN_HEADS, 2 * DN_HEADS

INFO_E, INFO_W, INFO_R = 0, TOP_K, 2 * TOP_K

TOKEN_TILE = 512
MOE_TILE = 256
NEG = -1e30
LOG2E = 1.4426950408889634
GLA_SUB_BLOCK = SUBLANES


def _params(sem):
    return pltpu.CompilerParams(dimension_semantics=sem, vmem_limit_bytes=VMEM_LIMIT_BYTES)


def _mm(a, b):
    return jnp.dot(a.astype(BF16), b.astype(BF16), preferred_element_type=F32)


def _mm_nt(a, b):
    return lax.dot_general(a.astype(BF16), b.astype(BF16), (((1,), (1,)), ((), ())), preferred_element_type=F32)


def _mm_tn(a, b):
    return lax.dot_general(a.astype(BF16), b.astype(BF16), (((0,), (0,)), ((), ())), preferred_element_type=F32)


def _sigmoid(x):
    return 1.0 / (1.0 + jnp.exp(-x))


def _softplus(x):
    return jnp.maximum(x, 0.0) + jnp.log(1.0 + jnp.exp(-jnp.abs(x)))


def _rms(x, gain):
    return x * lax.rsqrt(jnp.mean(x * x, axis=-1, keepdims=True) + EPS) * gain


def _gated_rms(o, gate, gain):
    return _rms(o, gain) * (gate * _sigmoid(gate))


ROW_TILE = D_MODEL // LANES
assert ROW_TILE == SUBLANES


def _store_token_tiles(ref, x):
    tm = x.shape[0]
    for j in range(ROW_TILE):
        ref[pl.ds(j, tm, stride=ROW_TILE), :] = x[:, j * LANES:(j + 1) * LANES]


def _load_token_tiles(ref):
    tm = ref.shape[0] // ROW_TILE
    return jnp.concatenate([ref[pl.ds(j, tm, stride=ROW_TILE), :] for j in range(ROW_TILE)], axis=1)


def _row_to_col(row):
    n = row.shape[1]
    r = lax.broadcasted_iota(I32, (n, n), 0)
    c = lax.broadcasted_iota(I32, (n, n), 1)
    return jnp.sum(jnp.where(r == c, jnp.broadcast_to(row, (n, n)), 0.0), axis=1, keepdims=True)


def _split2(x):
    hi = x.astype(BF16)
    return hi, (x - hi.astype(F32)).astype(BF16)


def _split3(x):
    hi = x.astype(BF16)
    r = x - hi.astype(F32)
    mid = r.astype(BF16)
    return hi, mid, (r - mid.astype(F32)).astype(BF16)


def _mm_cum(tri, x):
    hi, mid, lo = _split3(x)
    dot = lambda p: jnp.dot(tri, p, preferred_element_type=F32)
    return dot(hi) + (dot(mid) + dot(lo))


def _mm_split(a, b):
    ah, al = _split2(a)
    bh, bl = _split2(b)
    dot = lambda x, y: jnp.dot(x, y, preferred_element_type=F32)
    return dot(ah, bh) + (dot(ah, bl) + dot(al, bh))


def _ticker(fillers):
    pending = list(fillers)

    def tick(flush=False):
        while pending:
            pending.pop(0)()
            if not flush:
                break
    return tick


def _delta_chunks(qs_, ks_, vs_, gcs_, grs_, betas_, ss_, fillers=()):
    tick = _ticker(fillers)
    heads = range(len(qs_))
    c = qs_[0].shape[0]
    ti = lax.broadcasted_iota(I32, (c, c), 0)
    tj = lax.broadcasted_iota(I32, (c, c), 1)
    eye = jnp.where(ti == tj, 1.0, 0.0)
    decay = [jnp.exp(jnp.where(ti >= tj, gcs_[h] - grs_[h], NEG)) for h in heads]
    qs = [qs_[h] * (DN_DK ** -0.5) for h in heads]
    kk = [_mm_nt(ks_[h], ks_[h]) for h in heads]
    nm = [jnp.where(ti > tj, betas_[h] * kk[h] * decay[h], 0.0) for h in heads]
    t = [eye - nm[h] for h in heads]
    p = nm
    step = 2
    while step < c:
        tick()
        p = [_mm(p[h], p[h]) for h in heads]
        t = [t[h] + _mm(t[h], p[h]) for h in heads]
        step *= 2
    eg = [jnp.exp(gcs_[h]) for h in heads]
    rhs = [jnp.concatenate([betas_[h] * vs_[h], (betas_[h] * eg[h]) * ks_[h]], axis=1) for h in heads]
    tick()
    sol = [_mm(t[h], rhs[h]) for h in heads]
    tick()
    resid = [rhs[h] - (sol[h] + _mm_split(nm[h], sol[h])) for h in heads]
    tick()
    sol = [sol[h] + _mm(t[h], resid[h]) for h in heads]
    qk = [_mm_nt(qs[h], ks_[h]) * decay[h] for h in heads]
    tick()
    u = [sol[h][:, :DN_DV] - _mm(sol[h][:, DN_DV:], ss_[h]) for h in heads]
    tick()
    o = [_mm(qs[h] * eg[h], ss_[h]) + _mm(qk[h], u[h]) for h in heads]
    gl = [gcs_[h][c - 1:c] for h in heads]
    s_new = [ss_[h] * jnp.exp(gl[h]) + _mm_tn(ks_[h] * jnp.exp(gl[h] - gcs_[h]), u[h]) for h in heads]
    tick(flush=True)
    return o, s_new


def _gla_chunks(qs, ks, gs, vss, ss, m, fillers=()):
    tick = _ticker(fillers)
    probs = range(len(qs))
    c = qs[0].shape[0]
    nh = len(vss[0])
    heads = range(nh)
    dk = LANES // nh
    dlane = lax.broadcasted_iota(I32, (1, LANES), 1)
    if nh == 1:
        mask = lambda x, h: x
    else:
        hm = [jnp.where(jnp.logical_and(dlane >= h * dk, dlane < (h + 1) * dk), 1.0, 0.0) for h in heads]
        mask = lambda x, h: x * hm[h]
    gs = [gs[i] * LOG2E for i in probs]
    qeg = [qs[i] * jnp.exp2(gs[i]) for i in probs]
    o = [[_mm(mask(qeg[i], h), ss[i]) for h in heads] for i in probs]
    lane = lax.broadcasted_iota(I32, (m, c), 1)
    sub = lax.broadcasted_iota(I32, (m, c), 0)
    blocks = [[[] for _ in heads] for _ in probs]
    for blk in range(c // m):
        tick()
        r0 = blk * m
        qb = [qs[i][r0:r0 + m] for i in probs]
        gb = [gs[i][r0:r0 + m] for i in probs]
        kb = [ks[i][r0:r0 + m] for i in probs]
        if blk > 0:
            base = [gs[i][r0 - 1:r0] for i in probs]
            kt = [jnp.concatenate([ks[i][:r0] * jnp.exp2(base[i] - gs[i][:r0]), jnp.zeros((c - r0, LANES), F32)], axis=0)
                  for i in probs]
            qt = [qb[i] * jnp.exp2(gb[i] - base[i]) for i in probs]
            att = [[_mm_nt(mask(qt[i], h), kt[i]) for h in heads] for i in probs]
        else:
            att = [[jnp.zeros((m, c), F32) for _ in heads] for _ in probs]
        for j in range(m):
            keep = jnp.logical_and(lane == r0 + j, sub >= j)
            for i in probs:
                prod = qb[i] * (kb[i][j:j + 1] * jnp.exp2(gb[i] - gb[i][j:j + 1]))
                for h in heads:
                    att[i][h] = jnp.where(keep, jnp.sum(mask(prod, h), axis=1, keepdims=True), att[i][h])
        for i in probs:
            for h in heads:
                blocks[i][h].append(att[i][h])
    tick(flush=True)
    s_new = []
    for i in probs:
        gl = gs[i][c - 1:c]
        kg = ks[i] * jnp.exp2(gl - gs[i])
        s_i = ss[i] * _row_to_col(jnp.exp2(gl))
        for h in heads:
            v = vss[i][h]() if callable(vss[i][h]) else vss[i][h]
            att = blocks[i][h][0] if len(blocks[i][h]) == 1 else jnp.concatenate(blocks[i][h], axis=0)
            o[i][h] = o[i][h] + _mm(att, v)
            s_i = s_i + _mm_tn(mask(kg, h), v)
        s_new.append(s_i)
    return o, s_new


def _tri_incl(c):
    r = lax.broadcasted_iota(I32, (c, c), 0)
    cc = lax.broadcasted_iota(I32, (c, c), 1)
    return jnp.where(r >= cc, 1.0, 0.0).astype(BF16)


def _combine_kernel(pos_ref, *refs, final, tiles_a):
    if final:
        x_ref, info_ref, gain_ref, ys_hbm, outa_ref, outb_ref, ybuf, sem = refs
    else:
        x_ref, info_ref, ys_hbm, out_ref, ybuf, sem = refs
    i = pl.program_id(0)
    n = pl.num_programs(0)
    tm = x_ref.shape[0]
    slot = i % 2

    def gather_start(tile, s):
        def body(r, carry):
            for k in range(TOP_K):
                src = pl.multiple_of(pos_ref[TOP_K * (tile * tm + r) + k] * ROW_TILE, ROW_TILE)
                dst = pl.multiple_of(r * ROW_TILE, ROW_TILE)
                pltpu.make_async_copy(ys_hbm.at[pl.ds(src, ROW_TILE)], ybuf.at[s, k, pl.ds(dst, ROW_TILE)],
                                      sem.at[s]).start(priority=k % 2)
            return carry
        lax.fori_loop(0, tm, body, 0, unroll=8)

    @pl.when(i == 0)
    def _():
        gather_start(0, 0)

    @pl.when(i + 1 < n)
    def _():
        gather_start(i + 1, 1 - slot)

    for k in range(TOP_K):
        pltpu.make_async_copy(ys_hbm.at[pl.ds(0, tm * ROW_TILE)], ybuf.at[slot, k], sem.at[slot]).wait()
    info = info_ref[...]
    y0, y1 = (_load_token_tiles(ybuf.at[slot, k]) for k in range(TOP_K))
    x = x_ref[...] + (info[:, INFO_W:INFO_W + 1] * y0 + info[:, INFO_W + 1:INFO_W + 2] * y1)
    if final:
        h = _rms(x, gain_ref[...])

        @pl.when(i < tiles_a)
        def _():
            outa_ref[...] = h

        @pl.when(i >= tiles_a)
        def _():
            outb_ref[...] = h
    else:
        out_ref[...] = x


def _combine(x, info, pos, ys, final_gain=None, n_a=None):
    n, d = x.shape
    tm = TOKEN_TILE
    final = final_gain is not None
    tiles_a = n_a // tm if final else None
    row = lambda width: pl.BlockSpec((tm, width), lambda i, pos: (i, 0))
    in_specs = [row(d), row(LANES)]
    args = [x, info]
    if final:
        in_specs.append(pl.BlockSpec((1, d), lambda i, pos: (0, 0)))
        args.append(final_gain)
        out_specs = [pl.BlockSpec((tm, d), lambda i, pos: (jnp.minimum(i, tiles_a - 1), 0)),
                     pl.BlockSpec((tm, d), lambda i, pos: (jnp.maximum(i - tiles_a, 0), 0))]
        out_shape = [jax.ShapeDtypeStruct((n_a, d), F32), jax.ShapeDtypeStruct((n - n_a, d), F32)]
    else:
        out_specs = row(d)
        out_shape = jax.ShapeDtypeStruct((n, d), F32)
    in_specs.append(pl.BlockSpec(memory_space=pl.ANY))
    args.append(ys)
    grid_spec = pltpu.PrefetchScalarGridSpec(
        num_scalar_prefetch=1, grid=(n // tm,), in_specs=in_specs, out_specs=out_specs,
        scratch_shapes=[pltpu.VMEM((2, TOP_K, tm * ROW_TILE, LANES), F32), pltpu.SemaphoreType.DMA((2,))])
    return pl.pallas_call(
        functools.partial(_combine_kernel, final=final, tiles_a=tiles_a),
        grid_spec=grid_spec,
        out_shape=out_shape,
        compiler_params=_params(("arbitrary",)),
        name="moe_combine",
    )(pos, *args)


PROJ_CHUNK = 256


def _project_rows(x_refs, gain_ref, w_ref, p_scr, h_scr, c, first_cols):
    x = jnp.concatenate([r[...] for r in x_refs], axis=0) if len(x_refs) > 1 else x_refs[0][...]
    h_scr[...] = _rms(x, gain_ref[...]).astype(BF16)

    def piece(c0, c1):
        def run():
            p_scr[:, c0:c1] = jnp.dot(h_scr[...], w_ref[:, c0:c1], preferred_element_type=F32)
        return run

    piece(0, first_cols)()
    cols = p_scr.shape[1]
    fillers = [piece(c0, min(c0 + PROJ_CHUNK, cols)) for c0 in range(first_cols, cols, PROJ_CHUNK)]
    return [p_scr.at[pl.ds(b * c, c)] for b in range(len(x_refs))], fillers


def _ab_mixer_kernel(*refs, c, nseq, zero_init):
    x_refs, refs = refs[:nseq], refs[nseq:]
    if zero_init:
        gain_ref, win_ref, cw_ref, prm_ref, gkw_ref, gkb_ref, o_ref, conv_ref, sdn_ref, sgla_ref, p_scr, h_scr = refs
    else:
        (gain_ref, win_ref, cw_ref, prm_ref, gkw_ref, gkb_ref, conv_in, sdn_in, sgla_in,
         o_ref, conv_ref, sdn_ref, sgla_ref, p_scr, h_scr) = refs

    @pl.when(pl.program_id(1) == 0)
    def _():
        if zero_init:
            conv_ref[...] = jnp.zeros_like(conv_ref)
            sdn_ref[...] = jnp.zeros_like(sdn_ref)
            sgla_ref[...] = jnp.zeros_like(sgla_ref)
        else:
            conv_ref[...] = conv_in[...]
            sdn_ref[...] = sdn_in[...]
            sgla_ref[...] = sgla_in[...]

    p_refs, fillers = _project_rows(x_refs, gain_ref, win_ref, p_scr, h_scr, c, AB_Z)

    m = min(GLA_SUB_BLOCK, c)
    tri = _tri_incl(c)
    cw = cw_ref[...]
    a_log, dt_bias = prm_ref[0:1], prm_ref[1:2]
    dn_gain, gla_gain = prm_ref[2:3], prm_ref[3:4]
    n_pairs = GLA_HEADS // GLA_PACK
    conv_old = [conv_ref[b] for b in range(nseq)]
    sdn_old = [[sdn_ref[b, h] for h in range(DN_HEADS)] for b in range(nseq)]
    sgla_old = [[sgla_ref[b, pr] for pr in range(n_pairs)] for b in range(nseq)]
    dn = {key: [] for key in ("q", "k", "v", "gc", "gr", "beta", "s")}
    miscs = []
    for b in range(nseq):
        p_ref = p_refs[b]
        u = p_ref[:, AB_U:AB_U + CONV_CH]
        ucat = jnp.concatenate([conv_old[b], u], axis=0)
        acc = u * cw[CONV_W - 1:CONV_W]
        for j in range(1, CONV_W):
            acc = acc + pltpu.roll(ucat, j, 0)[SUBLANES:SUBLANES + c] * cw[CONV_W - 1 - j:CONV_W - j]
        conv_ref[b] = ucat[c:c + SUBLANES]
        qkv = acc * _sigmoid(acc)

        misc = p_ref[:, AB_MISC:AB_MISC + LANES]
        miscs.append(misc)
        g_all = -jnp.exp(a_log) * _softplus(misc + dt_bias)
        beta_all = _sigmoid(misc)
        gcum = _mm_cum(tri, g_all)
        gcum_t = jnp.concatenate([gcum, jnp.zeros((LANES - c, LANES), F32)], axis=0).T
        for h in range(DN_HEADS):
            lo = h * DN_DK
            q = qkv[:, lo:lo + DN_DK]
            k = qkv[:, DN_QK + lo:DN_QK + lo + DN_DK]
            dn["q"].append(q * lax.rsqrt(jnp.sum(q * q, axis=-1, keepdims=True) + L2_EPS))
            dn["k"].append(k * lax.rsqrt(jnp.sum(k * k, axis=-1, keepdims=True) + L2_EPS))
            dn["v"].append(qkv[:, 2 * DN_QK + h * DN_DV:2 * DN_QK + (h + 1) * DN_DV])
            dn["gc"].append(gcum[:, MISC_A + h:MISC_A + h + 1])
            dn["gr"].append(gcum_t[MISC_A + h:MISC_A + h + 1, :c])
            dn["beta"].append(beta_all[:, MISC_B + h:MISC_B + h + 1])
            dn["s"].append(sdn_old[b][h])
    os_, ss_ = _delta_chunks(dn["q"], dn["k"], dn["v"], dn["gc"], dn["gr"], dn["beta"], dn["s"], fillers)
    for b in range(nseq):
        for h in range(DN_HEADS):
            i = b * DN_HEADS + h
            sdn_ref[b, h] = ss_[i]
            z = p_refs[b][:, AB_Z + h * DN_DV:AB_Z + (h + 1) * DN_DV]
            o_ref[b, :, h * DN_DV:(h + 1) * DN_DV] = _gated_rms(os_[i], z, dn_gain).astype(o_ref.dtype)

    gl = {key: [] for key in ("q", "k", "g", "v", "s")}
    for b in range(nseq):
        p_ref = p_refs[b]
        logits = _mm_split(miscs[b], gkw_ref[...]) + gkb_ref[...]
        logf = -_softplus(-logits) * (1.0 / GLA_NORMALIZER)
        gall = _mm_cum(tri, logf)
        for pair in range(n_pairs):
            lo = pair * LANES
            gl["q"].append(p_ref[:, AB_GQ + lo:AB_GQ + lo + LANES] * (GLA_DK ** -0.5))
            gl["k"].append(p_ref[:, AB_GK + lo:AB_GK + lo + LANES])
            gl["g"].append(gall[:, lo:lo + LANES])
            gl["v"].append([p_ref[:, AB_GV + h * GLA_DV:AB_GV + (h + 1) * GLA_DV]
                            for h in range(pair * GLA_PACK, (pair + 1) * GLA_PACK)])
            gl["s"].append(sgla_old[b][pair])
    os_, ss_ = _gla_chunks(gl["q"], gl["k"], gl["g"], gl["v"], gl["s"], m)
    for b in range(nseq):
        for pair in range(n_pairs):
            i = b * n_pairs + pair
            sgla_ref[b, pair] = ss_[i]
            for hh in range(GLA_PACK):
                h = pair * GLA_PACK + hh
                gate = p_refs[b][:, AB_GG + h * GLA_DV:AB_GG + (h + 1) * GLA_DV]
                o_ref[b, :, DN_V + h * GLA_DV:DN_V + (h + 1) * GLA_DV] = \
                    _gated_rms(os_[i][hh], gate, gla_gain).astype(o_ref.dtype)


def _seq_specs(cols, row0, c, nchunk, nseq):
    blk0 = row0 // c
    return [pl.BlockSpec((c, cols), lambda g, i, s=s: (blk0 + (g * nseq + s) * nchunk + i, 0)) for s in range(nseq)]


def _ab_mixer(x, row0, nb, t, nseq, gain, w_in, cw, prm, gkw, gkb, states):
    c = min(CHUNK, t)
    nchunk = t // c
    n_groups = nb // nseq
    zero_init = states is None
    st_map3 = lambda g, i: (g, 0, 0)
    st_map4 = lambda g, i: (g, 0, 0, 0)
    const = lambda shape: pl.BlockSpec(shape, lambda g, i: (0,) * len(shape))
    consts = [gain, w_in, cw, prm, gkw, gkb]
    in_specs = _seq_specs(D_MODEL, row0, c, nchunk, nseq) + [const(a.shape) for a in consts]
    args = [x] * nseq + consts
    gla_packed = (GLA_HEADS // GLA_PACK, GLA_PACK * GLA_DK, GLA_DV)
    st_specs = [pl.BlockSpec((nseq, SUBLANES, CONV_CH), st_map3),
                pl.BlockSpec((nseq, DN_HEADS, DN_DK, DN_DV), st_map4),
                pl.BlockSpec((nseq,) + gla_packed, st_map4)]
    if not zero_init:
        conv8, sdn, sgla = states
        in_specs += st_specs
        args += [conv8, sdn, sgla.reshape((nb,) + gla_packed)]
    out_shape = [jax.ShapeDtypeStruct((nb, t, D_MODEL), BF16),
                 jax.ShapeDtypeStruct((nb, SUBLANES, CONV_CH), F32),
                 jax.ShapeDtypeStruct((nb, DN_HEADS, DN_DK, DN_DV), F32),
                 jax.ShapeDtypeStruct((nb,) + gla_packed, F32)]
    out_specs = [pl.BlockSpec((nseq, c, D_MODEL), lambda g, i: (g, i, 0))] + st_specs
    o, conv_out, sdn_out, sgla_out = pl.pallas_call(
        functools.partial(_ab_mixer_kernel, c=c, nseq=nseq, zero_init=zero_init),
        grid=(n_groups, nchunk),
        in_specs=in_specs,
        out_specs=out_specs,
        out_shape=out_shape,
        scratch_shapes=[pltpu.VMEM((nseq * c, AB_COLS), F32), pltpu.VMEM((nseq * c, D_MODEL), BF16)],
        compiler_params=_params(("parallel", "arbitrary")),
        name="ab_mixer",
    )(*args)
    return o.reshape(nb * t, D_MODEL), conv_out, sdn_out, sgla_out.reshape(nb, GLA_HEADS, GLA_DK, GLA_DV)


def _hgrn_mixer_kernel(*refs, c, nseq, zero_init, layer):
    x_refs, refs = refs[:nseq], refs[nseq:]
    if zero_init:
        mixgain_ref, win_ref, lb_ref, gain_ref, o_ref, s_ref, p_scr, h_scr = refs
    else:
        mixgain_ref, win_ref, lb_ref, gain_ref, s_in, o_ref, s_ref, p_scr, h_scr = refs
    p_refs, fillers = _project_rows(x_refs, mixgain_ref, win_ref, p_scr, h_scr, c, 2 * HG_F)

    @pl.when(pl.program_id(1) == 0)
    def _():
        if zero_init:
            s_ref[...] = jnp.zeros_like(s_ref)
        else:
            s_ref[...] = s_in[...]

    raw = lb_ref[...]
    e = jnp.exp(raw - jnp.max(raw, axis=0, keepdims=True))
    sm = e / jnp.sum(e, axis=0, keepdims=True)
    cum = sm[0:1]
    for l in range(1, layer + 1):
        cum = cum + sm[l:l + 1]
    lb = cum - sm[0:1]

    m = min(GLA_SUB_BLOCK, c)
    tri = _tri_incl(c)
    gain = gain_ref[...]
    pr = {key: [] for key in ("q", "k", "g", "v", "s")}
    for b in range(nseq):
        p_ref = p_refs[b]
        fr = p_ref[:, HG_F:2 * HG_F]
        logf = jnp.log(lb + (1.0 - lb) * _sigmoid(fr))
        kall = (1.0 - lb) * _sigmoid(-fr)
        gall = _mm_cum(tri, logf)
        for h in range(HG_HEADS):
            lo = h * HG_DK
            pr["q"].append(p_ref[:, lo:lo + HG_DK] * (HG_DK ** -0.5))
            pr["k"].append(kall[:, lo:lo + HG_DK])
            pr["g"].append(gall[:, lo:lo + HG_DK])
            pr["v"].append([functools.partial(lambda r, h: r[:, 2 * HG_F + h * HG_DV:2 * HG_F + (h + 1) * HG_DV], p_ref, h)])
            pr["s"].append(s_ref[b, h])
    os_, ss_ = _gla_chunks(pr["q"], pr["k"], pr["g"], pr["v"], pr["s"], m, fillers)
    for b in range(nseq):
        for h in range(HG_HEADS):
            i = b * HG_HEADS + h
            s_ref[b, h] = ss_[i]
            gate = p_refs[b][:, 2 * HG_F + D_MODEL + h * HG_DV:2 * HG_F + D_MODEL + (h + 1) * HG_DV]
            o_ref[b, :, h * HG_DV:(h + 1) * HG_DV] = _gated_rms(os_[i][0], gate, gain).astype(o_ref.dtype)


def _hgrn_mixer(x, row0, nb, t, nseq, mix_gain, w_in, lb, gain, state, layer):
    c = min(CHUNK, t)
    nchunk = t // c
    zero_init = state is None
    const = lambda shape: pl.BlockSpec(shape, lambda g, i: (0,) * len(shape))
    st_spec = pl.BlockSpec((nseq, HG_HEADS, HG_DK, HG_DV), lambda g, i: (g, 0, 0, 0))
    consts = [mix_gain, w_in, lb, gain]
    in_specs = _seq_specs(D_MODEL, row0, c, nchunk, nseq) + [const(a.shape) for a in consts]
    args = [x] * nseq + consts
    if not zero_init:
        in_specs.append(st_spec)
        args.append(state)
    o, s_out = pl.pallas_call(
        functools.partial(_hgrn_mixer_kernel, c=c, nseq=nseq, zero_init=zero_init, layer=layer),
        grid=(nb // nseq, nchunk),
        in_specs=in_specs,
        out_specs=[pl.BlockSpec((nseq, c, D_MODEL), lambda g, i: (g, i, 0)), st_spec],
        out_shape=[jax.ShapeDtypeStruct((nb, t, D_MODEL), BF16),
                   jax.ShapeDtypeStruct((nb, HG_HEADS, HG_DK, HG_DV), F32)],
        scratch_shapes=[pltpu.VMEM((nseq * c, w_in.shape[1]), F32), pltpu.VMEM((nseq * c, D_MODEL), BF16)],
        compiler_params=_params(("parallel", "arbitrary")),
        name="hgrn_mixer",
    )(*args)
    return o.reshape(nb * t, D_MODEL), s_out


def _outproj_router_kernel(*refs, tiles_a, split_x):
    if split_x:
        oa_ref, ob_ref, xa_ref, xb_ref = refs[:4]
        refs = refs[4:]
    else:
        oa_ref, ob_ref, x_ref = refs[:3]
        refs = refs[3:]
    wout_ref, gain_ref, wr_ref, br_ref, x1_ref, h_ref, info_ref, cnt_ref, run_ref = refs
    first = pl.program_id(0) < tiles_a

    @pl.when(pl.program_id(0) == 0)
    def _():
        run_ref[...] = jnp.zeros_like(run_ref)

    o = jnp.where(first, oa_ref[...], ob_ref[...])
    x = jnp.where(first, xa_ref[...], xb_ref[...]) if split_x else x_ref[...]
    x1 = x + jnp.dot(o, wout_ref[...], preferred_element_type=F32)
    x1_ref[...] = x1
    h = _rms(x1, gain_ref[...])
    _store_token_tiles(h_ref, h)
    logit = _mm_split(h, wr_ref[...]) + br_ref[...]
    lane = lax.broadcasted_iota(I32, logit.shape, 1)
    far = jnp.int32(LANES)
    lg = jnp.where(lane < N_GROUPS, logit, NEG)
    mg = jnp.max(lg, axis=-1, keepdims=True)
    p_g = 1.0 / jnp.sum(jnp.exp(lg - mg), axis=-1, keepdims=True)
    g_top = jnp.min(jnp.where(lg == mg, lane, far), axis=-1, keepdims=True)
    lo = N_GROUPS + g_top * EXPERTS_PER_GROUP
    sel = jnp.logical_and(lane >= lo, lane < lo + EXPERTS_PER_GROUP)
    le = jnp.where(sel, logit, NEG)
    ee = jnp.exp(le - jnp.max(le, axis=-1, keepdims=True))
    pe = jnp.where(sel, ee / jnp.sum(ee, axis=-1, keepdims=True), -1.0)
    p1 = jnp.max(pe, axis=-1, keepdims=True)
    i1 = jnp.min(jnp.where(pe == p1, lane, far), axis=-1, keepdims=True)
    pe2 = jnp.where(lane == i1, -1.0, pe)
    p2 = jnp.max(pe2, axis=-1, keepdims=True)
    i2 = jnp.min(jnp.where(pe2 == p2, lane, far), axis=-1, keepdims=True)
    denom = p1 + p2
    w1 = p_g * (p1 / denom)
    w2 = p_g * (p2 / denom)
    tm = logit.shape[0]
    oh1 = jnp.where(lane == i1, 1.0, 0.0)
    oh2 = jnp.where(lane == i2, 1.0, 0.0)
    both = oh1 + oh2
    tr = lax.broadcasted_iota(I32, (tm, tm), 0)
    tc = lax.broadcasted_iota(I32, (tm, tm), 1)
    before = run_ref[...] + _mm(jnp.where(tr > tc, 1.0, 0.0), both)
    r1 = jnp.sum(oh1 * before, axis=-1, keepdims=True)
    r2 = jnp.sum(oh2 * before, axis=-1, keepdims=True)
    run = run_ref[...] + jnp.sum(both, axis=0, keepdims=True)
    run_ref[...] = run
    cnt_ref[...] = jnp.broadcast_to(run, cnt_ref.shape)
    vals = ((i1 - N_GROUPS).astype(F32), (i2 - N_GROUPS).astype(F32), w1, w2, r1, r2)
    info = jnp.zeros_like(logit)
    for idx, val in enumerate(vals):
        info = jnp.where(lane == idx, val, info)
    info_ref[...] = info


def _outproj_router(o_a, o_b, xs, wout, gain, wr, br):
    n_a, d = o_a.shape
    n = n_a + o_b.shape[0]
    tm = TOKEN_TILE
    tiles_a = n_a // tm
    row = lambda w: pl.BlockSpec((tm, w), lambda i: (i, 0))
    seg_a = pl.BlockSpec((tm, d), lambda i: (jnp.minimum(i, tiles_a - 1), 0))
    seg_b = pl.BlockSpec((tm, d), lambda i: (jnp.maximum(i - tiles_a, 0), 0))
    const = lambda shape: pl.BlockSpec(shape, lambda i: (0,) * len(shape))
    split_x = len(xs) == 2
    return pl.pallas_call(
        functools.partial(_outproj_router_kernel, tiles_a=tiles_a, split_x=split_x),
        grid=(n // tm,),
        in_specs=[seg_a, seg_b] + ([seg_a, seg_b] if split_x else [row(d)])
        + [const((d, d)), const((1, d)), const((d, LANES)), const((1, LANES))],
        out_specs=[row(d), pl.BlockSpec((tm * ROW_TILE, LANES), lambda i: (i, 0)), row(LANES), const((SUBLANES, LANES))],
        out_shape=[jax.ShapeDtypeStruct((n, d), F32), jax.ShapeDtypeStruct((n * ROW_TILE, LANES), F32),
                   jax.ShapeDtypeStruct((n, LANES), F32), jax.ShapeDtypeStruct((SUBLANES, LANES), F32)],
        scratch_shapes=[pltpu.VMEM((1, LANES), F32)],
        compiler_params=_params(("arbitrary",)),
        name="outproj_router",
    )(o_a, o_b, *xs, wout, gain, wr, br)


def _dispatch_kernel(pos_ref, lo_ref, hi_ref, h_ref, xs_hbm, buf, zrow, sem, zsem):
    i = pl.program_id(0)
    n = pl.num_programs(0)
    tm = h_ref.shape[0] // ROW_TILE
    blk = tm * ROW_TILE
    slot = i % 2
    token = lambda r: pl.ds(pl.multiple_of(r * ROW_TILE, ROW_TILE), ROW_TILE)

    def tile_wait(s):
        for _ in range(TOP_K):
            pltpu.make_async_copy(buf.at[s], xs_hbm.at[pl.ds(0, blk)], sem.at[s]).wait()

    zblk = zrow.shape[0]

    def pad_rows(fn, tile_fn):
        for e in range(N_EXPERTS):
            lax.fori_loop(lo_ref[e], hi_ref[e], fn, 0)
        lax.fori_loop(hi_ref[N_EXPERTS - 1] // (zblk // ROW_TILE), xs_hbm.shape[0] // zblk, tile_fn, 0)

    def pad_start(r, carry):
        pltpu.make_async_copy(zrow.at[token(0)], xs_hbm.at[token(r)], zsem.at[0]).start()
        return carry

    def pad_wait(r, carry):
        pltpu.make_async_copy(zrow.at[token(0)], xs_hbm.at[token(0)], zsem.at[0]).wait()
        return carry

    def pad_tile_start(t, carry):
        pltpu.make_async_copy(zrow, xs_hbm.at[pl.ds(pl.multiple_of(t * zblk, zblk), zblk)], zsem.at[0]).start()
        return carry

    def pad_tile_wait(t, carry):
        pltpu.make_async_copy(zrow, xs_hbm.at[pl.ds(0, zblk)], zsem.at[0]).wait()
        return carry

    @pl.when(i == 0)
    def _():
        zrow[...] = jnp.zeros_like(zrow)
        pad_rows(pad_start, pad_tile_start)

    @pl.when(i >= 2)
    def _():
        tile_wait(slot)
    buf[slot] = h_ref[...]

    def body(r, carry):
        for k in range(TOP_K):
            row = pos_ref[TOP_K * (i * tm + r) + k]
            pltpu.make_async_copy(buf.at[slot, token(r)], xs_hbm.at[token(row)], sem.at[slot]).start(priority=k % 2)
        return carry
    lax.fori_loop(0, tm, body, 0, unroll=8)

    @pl.when(i == n - 1)
    def _():
        tile_wait(slot)

        @pl.when(n > 1)
        def _():
            tile_wait(1 - slot)
        pad_rows(pad_wait, pad_tile_wait)


def _dispatch(h, pos, pad_lo, pad_hi, n_rows):
    blk = TOKEN_TILE * ROW_TILE
    n = h.shape[0] // ROW_TILE
    grid_spec = pltpu.PrefetchScalarGridSpec(
        num_scalar_prefetch=3,
        grid=(n // TOKEN_TILE,),
        in_specs=[pl.BlockSpec((blk, LANES), lambda i, pos, lo, hi: (i, 0))],
        out_specs=pl.BlockSpec(memory_space=pl.ANY),
        scratch_shapes=[pltpu.VMEM((2, blk, LANES), F32), pltpu.VMEM((MOE_TILE * ROW_TILE, LANES), F32),
                        pltpu.SemaphoreType.DMA((2,)), pltpu.SemaphoreType.DMA((1,))],
    )
    return pl.pallas_call(
        _dispatch_kernel,
        grid_spec=grid_spec,
        out_shape=jax.ShapeDtypeStruct((n_rows * ROW_TILE, LANES), F32),
        compiler_params=_params(("arbitrary",)),
        name="moe_dispatch",
    )(pos, pad_lo, pad_hi, h)


def _moe_kernel(te_ref, nact_ref, x_ref, w1_ref, w3_ref, w2_ref, y_ref, w1b, w3b, w2b):
    i = pl.program_id(0)

    @pl.when(i < nact_ref[0])
    def _():
        @pl.when(jnp.logical_or(i == 0, te_ref[i] != te_ref[jnp.maximum(i - 1, 0)]))
        def _():
            w1b[...] = w1_ref[0, 0].astype(BF16)
            w3b[...] = w3_ref[0, 0].astype(BF16)
            w2b[...] = w2_ref[0, 0].astype(BF16)

        xb = _load_token_tiles(x_ref).astype(BF16)
        h1 = jnp.dot(xb, w1b[...], preferred_element_type=F32)
        h3 = jnp.dot(xb, w3b[...], preferred_element_type=F32)
        act = (h1 * _sigmoid(h1)) * h3
        _store_token_tiles(y_ref, jnp.dot(act.astype(BF16), w2b[...], preferred_element_type=F32))

    @pl.when(i >= nact_ref[0])
    def _():
        y_ref[...] = jnp.zeros_like(y_ref)


def _moe(xs, te, nact, w1, w3, w2, layer):
    d, f = w1.shape[2], w1.shape[3]
    tm = MOE_TILE
    blk = tm * ROW_TILE
    n_tiles = xs.shape[0] // blk
    last = lambda i, nact: jnp.maximum(jnp.minimum(i, nact[0] - 1), 0)
    grid_spec = pltpu.PrefetchScalarGridSpec(
        num_scalar_prefetch=2,
        grid=(n_tiles,),
        in_specs=[
            pl.BlockSpec((blk, LANES), lambda i, te, nact: (last(i, nact), 0)),
            pl.BlockSpec((1, 1, d, f), lambda i, te, nact: (layer, te[i], 0, 0)),
            pl.BlockSpec((1, 1, d, f), lambda i, te, nact: (layer, te[i], 0, 0)),
            pl.BlockSpec((1, 1, f, d), lambda i, te, nact: (layer, te[i], 0, 0)),
        ],
        out_specs=pl.BlockSpec((blk, LANES), lambda i, te, nact: (i, 0)),
        scratch_shapes=[pltpu.VMEM((d, f), BF16), pltpu.VMEM((d, f), BF16), pltpu.VMEM((f, d), BF16)],
    )
    return pl.pallas_call(
        _moe_kernel,
        grid_spec=grid_spec,
        out_shape=jax.ShapeDtypeStruct(xs.shape, F32),
        compiler_params=_params(("arbitrary",)),
        name="moe_experts",
    )(te, nact, xs, w1, w3, w2)


def _moe_rows(n):
    return ((TOP_K * n + N_EXPERTS * (MOE_TILE - 1)) // MOE_TILE) * MOE_TILE


def _route_tables(info, counts_row, n):
    tm = MOE_TILE
    n_tiles = _moe_rows(n) // tm
    ids = jnp.arange(N_EXPERTS, dtype=I32)
    counts = counts_row[0, N_GROUPS:N_GROUPS + N_EXPERTS].astype(I32)
    padded = ((counts + tm - 1) // tm) * tm
    cum = jnp.cumsum(padded)
    off = cum - padded
    e = info[:, INFO_E:INFO_E + TOP_K].astype(I32)
    rank = info[:, INFO_R:INFO_R + TOP_K].astype(I32)
    pos = (jnp.sum(jnp.where(e[:, :, None] == ids, off, 0), axis=-1) + rank).reshape(TOP_K * n)
    tile_start = jnp.arange(n_tiles, dtype=I32) * tm
    nact = cum[-1] // tm
    te_raw = jnp.sum((cum[None, :] <= tile_start[:, None]).astype(I32), axis=1)
    te_last = jnp.sum((cum <= (nact - 1) * tm).astype(I32))
    te = jnp.where(tile_start < cum[-1], jnp.minimum(te_raw, N_EXPERTS - 1), te_last)
    return pos, off + counts, cum, te, nact.reshape(1)


def _pad_lanes(a, width=LANES):
    return jnp.pad(a, [(0, 0)] * (a.ndim - 1) + [(0, width - a.shape[-1])])


def _seq_group(nb, want):
    g = want
    while nb % g:
        g //= 2
    return g


PROMPT_SEQS_PER_STEP = 8
SAMPLE_SEQS_PER_STEP = 16


def kernel(x_prompt, x_sample, state_conv_ab, state_delta_ab, state_gla_ab, state_hgrn_c, mix_norm, ab_w_in, ab_conv_w, ab_a_log, ab_dt_bias, ab_dn_norm, ab_gk_w2, ab_gk_b, ab_gla_norm, ab_w_out, c_w_in, c_lower_bounds, c_norm, c_w_out, ffn_norm, moe_w_group, moe_b_group, moe_w_expert, moe_b_expert, moe_w1, moe_w3, moe_w2, final_norm):
    bp, tp, d = x_prompt.shape
    bs, ts, _ = x_sample.shape
    n_p, n_s = bp * tp, bs * ts
    n = n_p + n_s
    depth = mix_norm.shape[0]
    xs_res = (x_prompt.reshape(n_p, d), x_sample.reshape(n_s, d))
    pg, sg = _seq_group(bp, PROMPT_SEQS_PER_STEP), _seq_group(bs, SAMPLE_SEQS_PER_STEP)

    info = pos = ys = None
    conv_p, delta_p, gla_p, hgrn_p = [], [], [], []
    conv_s, delta_s, gla_s, hgrn_s = [], [], [], []
    for layer in range(depth):
        j = layer // 2
        gain = mix_norm[layer].reshape(1, d)
        if layer % 2 == 0:
            u_w, a_w, b_w, z_w, gq_w, gk_w, gv_w, lr_w, gg_w = jnp.split(ab_w_in[j], _split_points(AB_SPLITS), axis=1)
            w_in = jnp.concatenate([u_w, _pad_lanes(jnp.concatenate([a_w, b_w, lr_w], axis=1)), z_w, gq_w, gk_w, gv_w, gg_w],
                                   axis=1).astype(BF16)
        else:
            w_in = c_w_in[j].astype(BF16)
        if ys is not None:
            xs_res = (_combine(xs_res[0], info, pos, ys),)
        (x_p, off_p), (x_s, off_s) = ((xs_res[0], 0), (xs_res[-1], 0 if len(xs_res) == 2 else n_p))

        if layer % 2 == 0:
            prm = jnp.concatenate([_pad_lanes(ab_a_log[j].reshape(1, -1)), _pad_lanes(ab_dt_bias[j].reshape(1, -1)),
                                   ab_dn_norm[j].reshape(1, -1), ab_gla_norm[j].reshape(1, -1),
                                   jnp.zeros((4, LANES), F32)], axis=0)
            gkw = jnp.zeros((LANES, GLA_K), F32).at[MISC_LR:MISC_LR + GLA_RANK].set(ab_gk_w2[j])
            gkb = ab_gk_b[j].reshape(1, GLA_K)
            cw = ab_conv_w[j]
            o_p, c8, sd, sgl = _ab_mixer(x_p, off_p, bp, tp, pg, gain, w_in, cw, prm, gkw, gkb, None)
            conv_p.append(c8[:, SUBLANES - (CONV_W - 1):])
            delta_p.append(sd)
            gla_p.append(sgl)
            conv8 = jnp.pad(state_conv_ab[j], ((0, 0), (SUBLANES - (CONV_W - 1), 0), (0, 0)))
            o_s, c8, sd, sgl = _ab_mixer(x_s, off_s, bs, ts, sg, gain, w_in, cw, prm, gkw, gkb,
                                         (conv8, state_delta_ab[j], state_gla_ab[j]))
            conv_s.append(c8[:, SUBLANES - (CONV_W - 1):])
            delta_s.append(sd)
            gla_s.append(sgl)
            w_out = ab_w_out[j]
        else:
            hg_gain = c_norm[j].reshape(1, -1)
            o_p, sh = _hgrn_mixer(x_p, off_p, bp, tp, pg, gain, w_in, c_lower_bounds, hg_gain, None, layer)
            hgrn_p.append(sh)
            o_s, sh = _hgrn_mixer(x_s, off_s, bs, ts, sg, gain, w_in, c_lower_bounds, hg_gain, state_hgrn_c[j], layer)
            hgrn_s.append(sh)
            w_out = c_w_out[j]

        wr = _pad_lanes(jnp.concatenate([moe_w_group[layer], moe_w_expert[layer]], axis=1))
        br = _pad_lanes(jnp.concatenate([moe_b_group[layer], moe_b_expert[layer]]).reshape(1, -1))
        x, hn, info, counts = _outproj_router(o_p, o_s, xs_res, w_out.astype(BF16), ffn_norm[layer].reshape(1, d), wr, br)
        xs_res = (x,)
        pos, pad_lo, pad_hi, te, nact = _route_tables(info, counts, n)
        xsort = _dispatch(hn, pos, pad_lo, pad_hi, _moe_rows(n))
        ys = _moe(xsort, te, nact, moe_w1, moe_w3, moe_w2, layer)

    y_p, y_s = _combine(xs_res[0], info, pos, ys, final_gain=final_norm.reshape(1, d), n_a=n_p)
    y_prompt = y_p.reshape(bp, tp, d)
    y_sample = y_s.reshape(bs, ts, d)
    return (y_prompt, y_sample, jnp.stack(conv_p), jnp.stack(delta_p), jnp.stack(gla_p), jnp.stack(hgrn_p),
            jnp.stack(conv_s), jnp.stack(delta_s), jnp.stack(gla_s), jnp.stack(hgrn_s))


def _split_points(sizes):
    pts, acc = [], 0
    for s in sizes[:-1]:
        acc += s
        pts.append(acc)
    return pts
```

```python
import functools

import jax
import jax.numpy as jnp
from jax import lax
from jax.experimental import pallas as pl
from jax.experimental.pallas import tpu as pltpu

F32 = jnp.float32
BF16 = jnp.bfloat16
I32 = jnp.int32

D_MODEL = 1024
EPS = 1e-6
CHUNK = 64
CONV_W = 4
DN_HEADS, DN_DK, DN_DV = 4, 128, 128
DN_QK = DN_HEADS * DN_DK
DN_V = DN_HEADS * DN_DV
CONV_CH = 2 * DN_QK + DN_V
GLA_HEADS, GLA_DK, GLA_DV = 4, 64, 128
GLA_K = GLA_HEADS * GLA_DK
GLA_V = GLA_HEADS * GLA_DV
GLA_RANK = 16
GLA_NORMALIZER = 16.0
GLA_PACK = 128 // GLA_DK
AB_SPLITS = (CONV_CH, DN_HEADS, DN_HEADS, DN_V, GLA_K, GLA_K, GLA_V, GLA_RANK, GLA_V)
HG_HEADS, HG_DK, HG_DV = 8, 128, 128
HG_F = HG_HEADS * HG_DK
N_GROUPS, EXPERTS_PER_GROUP = 4, 8
N_EXPERTS = N_GROUPS * EXPERTS_PER_GROUP
TOP_K = 2
L2_EPS = 1e-6

LANES = 128
SUBLANES = 8
VMEM_LIMIT_BYTES = 56 * 1024 * 1024

AB_U, AB_MISC, AB_Z, AB_GQ, AB_GK, AB_GV, AB_GG = 0, 1536, 1664, 2176, 2432, 2688, 3200
AB_COLS = AB_GG + GLA_V
MISC_A, MISC_B, MISC_LR = 0, DN_HEADS, 2 * DN_HEADS

INFO_E, INFO_W, INFO_R = 0, TOP_K, 2 * TOP_K

TOKEN_TILE = 1024
MOE_TILE = 256
NEG = -1e30
LOG2E = 1.4426950408889634
GLA_SUB_BLOCK = SUBLANES


def _params(sem):
    return pltpu.CompilerParams(dimension_semantics=sem, vmem_limit_bytes=VMEM_LIMIT_BYTES)


def _mm(a, b):
    return jnp.dot(a.astype(BF16), b.astype(BF16), preferred_element_type=F32)


def _mm_nt(a, b):
    return lax.dot_general(a.astype(BF16), b.astype(BF16), (((1,), (1,)), ((), ())), preferred_element_type=F32)


def _mm_tn(a, b):
    return lax.dot_general(a.astype(BF16), b.astype(BF16), (((0,), (0,)), ((), ())), preferred_element_type=F32)


def _sigmoid(x):
    return 1.0 / (1.0 + jnp.exp(-x))


def _softplus(x):
    return jnp.maximum(x, 0.0) + jnp.log(1.0 + jnp.exp(-jnp.abs(x)))


def _rms(x, gain):
    return x * lax.rsqrt(jnp.mean(x * x, axis=-1, keepdims=True) + EPS) * gain


def _gated_rms(o, gate, gain):
    return _rms(o, gain) * (gate * _sigmoid(gate))


ROW_TILE = D_MODEL // LANES
assert ROW_TILE == SUBLANES


def _store_token_tiles(ref, x):
    tm = x.shape[0]
    for j in range(ROW_TILE):
        ref[pl.ds(j, tm, stride=ROW_TILE), :] = x[:, j * LANES:(j + 1) * LANES]


def _load_token_tiles(ref):
    tm = ref.shape[0] // ROW_TILE
    return jnp.concatenate([ref[pl.ds(j, tm, stride=ROW_TILE), :] for j in range(ROW_TILE)], axis=1)


def _row_to_col(row):
    n = row.shape[1]
    r = lax.broadcasted_iota(I32, (n, n), 0)
    c = lax.broadcasted_iota(I32, (n, n), 1)
    return jnp.sum(jnp.where(r == c, jnp.broadcast_to(row, (n, n)), 0.0), axis=1, keepdims=True)


def _split2(x):
    hi = x.astype(BF16)
    return hi, (x - hi.astype(F32)).astype(BF16)


def _split3(x):
    hi = x.astype(BF16)
    r = x - hi.astype(F32)
    mid = r.astype(BF16)
    return hi, mid, (r - mid.astype(F32)).astype(BF16)


def _mm_cum(tri, x):
    hi, mid, lo = _split3(x)
    dot = lambda p: jnp.dot(tri, p, preferred_element_type=F32)
    return dot(hi) + (dot(mid) + dot(lo))


def _mm_split(a, b):
    ah, al = _split2(a)
    bh, bl = _split2(b)
    dot = lambda x, y: jnp.dot(x, y, preferred_element_type=F32)
    return dot(ah, bh) + (dot(ah, bl) + dot(al, bh))


def _ticker(fillers):
    pending = list(fillers)

    def tick(flush=False):
        while pending:
            pending.pop(0)()
            if not flush:
                break
    return tick


def _delta_chunks(qs_, ks_, vs_, gcs_, grs_, betas_, ss_, fillers=()):
    tick = _ticker(fillers)
    heads = range(len(qs_))
    c = qs_[0].shape[0]
    ti = lax.broadcasted_iota(I32, (c, c), 0)
    tj = lax.broadcasted_iota(I32, (c, c), 1)
    eye = jnp.where(ti == tj, 1.0, 0.0)
    decay = [jnp.exp(jnp.where(ti >= tj, gcs_[h] - grs_[h], NEG)) for h in heads]
    qs = [qs_[h] * (DN_DK ** -0.5) for h in heads]
    kk = [_mm_nt(ks_[h], ks_[h]) for h in heads]
    nm = [jnp.where(ti > tj, betas_[h] * kk[h] * decay[h], 0.0) for h in heads]
    t = [eye - nm[h] for h in heads]
    p = nm
    step = 2
    while step < c:
        tick()
        p = [_mm(p[h], p[h]) for h in heads]
        t = [t[h] + _mm(t[h], p[h]) for h in heads]
        step *= 2
    eg = [jnp.exp(gcs_[h]) for h in heads]
    rhs = [jnp.concatenate([betas_[h] * vs_[h], (betas_[h] * eg[h]) * ks_[h]], axis=1) for h in heads]
    tick()
    sol = [_mm(t[h], rhs[h]) for h in heads]
    tick()
    resid = [rhs[h] - (sol[h] + _mm_split(nm[h], sol[h])) for h in heads]
    tick()
    sol = [sol[h] + _mm(t[h], resid[h]) for h in heads]
    qk = [_mm_nt(qs[h], ks_[h]) * decay[h] for h in heads]
    tick()
    u = [sol[h][:, :DN_DV] - _mm(sol[h][:, DN_DV:], ss_[h]) for h in heads]
    tick()
    o = [_mm(qs[h] * eg[h], ss_[h]) + _mm(qk[h], u[h]) for h in heads]
    gl = [gcs_[h][c - 1:c] for h in heads]
    s_new = [ss_[h] * jnp.exp(gl[h]) + _mm_tn(ks_[h] * jnp.exp(gl[h] - gcs_[h]), u[h]) for h in heads]
    tick(flush=True)
    return o, s_new


def _gla_chunks(qs, ks, gs, vss, ss, m, fillers=()):
    tick = _ticker(fillers)
    probs = range(len(qs))
    c = qs[0].shape[0]
    nh = len(vss[0])
    heads = range(nh)
    dk = LANES // nh
    dlane = lax.broadcasted_iota(I32, (1, LANES), 1)
    if nh == 1:
        mask = lambda x, h: x
    else:
        hm = [jnp.where(jnp.logical_and(dlane >= h * dk, dlane < (h + 1) * dk), 1.0, 0.0) for h in heads]
        mask = lambda x, h: x * hm[h]
    gs = [gs[i] * LOG2E for i in probs]
    qeg = [qs[i] * jnp.exp2(gs[i]) for i in probs]
    o = [[_mm(mask(qeg[i], h), ss[i]) for h in heads] for i in probs]
    lane = lax.broadcasted_iota(I32, (m, c), 1)
    sub = lax.broadcasted_iota(I32, (m, c), 0)
    blocks = [[[] for _ in heads] for _ in probs]
    for blk in range(c // m):
        tick()
        r0 = blk * m
        qb = [qs[i][r0:r0 + m] for i in probs]
        gb = [gs[i][r0:r0 + m] for i in probs]
        kb = [ks[i][r0:r0 + m] for i in probs]
        if blk > 0:
            base = [gs[i][r0 - 1:r0] for i in probs]
            kt = [jnp.concatenate([ks[i][:r0] * jnp.exp2(base[i] - gs[i][:r0]), jnp.zeros((c - r0, LANES), F32)], axis=0)
                  for i in probs]
            qt = [qb[i] * jnp.exp2(gb[i] - base[i]) for i in probs]
            att = [[_mm_nt(mask(qt[i], h), kt[i]) for h in heads] for i in probs]
        else:
            att = [[jnp.zeros((m, c), F32) for _ in heads] for _ in probs]
        for j in range(m):
            keep = jnp.logical_and(lane == r0 + j, sub >= j)
            for i in probs:
                prod = qb[i] * (kb[i][j:j + 1] * jnp.exp2(gb[i] - gb[i][j:j + 1]))
                for h in heads:
                    att[i][h] = jnp.where(keep, jnp.sum(mask(prod, h), axis=1, keepdims=True), att[i][h])
        for i in probs:
            for h in heads:
                blocks[i][h].append(att[i][h])
    tick(flush=True)
    s_new = []
    for i in probs:
        gl = gs[i][c - 1:c]
        kg = ks[i] * jnp.exp2(gl - gs[i])
        s_i = ss[i] * _row_to_col(jnp.exp2(gl))
        for h in heads:
            v = vss[i][h]() if callable(vss[i][h]) else vss[i][h]
            att = blocks[i][h][0] if len(blocks[i][h]) == 1 else jnp.concatenate(blocks[i][h], axis=0)
            o[i][h] = o[i][h] + _mm(att, v)
            s_i = s_i + _mm_tn(mask(kg, h), v)
        s_new.append(s_i)
    return o, s_new


def _tri_incl(c):
    r = lax.broadcasted_iota(I32, (c, c), 0)
    cc = lax.broadcasted_iota(I32, (c, c), 1)
    return jnp.where(r >= cc, 1.0, 0.0).astype(BF16)


def _combine_kernel(pos_ref, *refs, final, tiles_a):
    if final:
        x_ref, info_ref, gain_ref, ys_hbm, outa_ref, outb_ref, ybuf, sem = refs
    else:
        x_ref, info_ref, ys_hbm, out_ref, ybuf, sem = refs
    i = pl.program_id(0)
    n = pl.num_programs(0)
    tm = x_ref.shape[0]
    slot = i % 2

    def gather_start(tile, s):
        def body(r, carry):
            for k in range(TOP_K):
                src = pl.multiple_of(pos_ref[TOP_K * (tile * tm + r) + k] * ROW_TILE, ROW_TILE)
                dst = pl.multiple_of(r * ROW_TILE, ROW_TILE)
                pltpu.make_async_copy(ys_hbm.at[pl.ds(src, ROW_TILE)], ybuf.at[s, k, pl.ds(dst, ROW_TILE)],
                                      sem.at[s]).start(priority=k % 2)
            return carry
        lax.fori_loop(0, tm, body, 0, unroll=8)

    @pl.when(i == 0)
    def _():
        gather_start(0, 0)

    @pl.when(i + 1 < n)
    def _():
        gather_start(i + 1, 1 - slot)

    for k in range(TOP_K):
        pltpu.make_async_copy(ys_hbm.at[pl.ds(0, tm * ROW_TILE)], ybuf.at[slot, k], sem.at[slot]).wait()
    info = info_ref[...]
    y0, y1 = (_load_token_tiles(ybuf.at[slot, k]) for k in range(TOP_K))
    x = x_ref[...] + (info[:, INFO_W:INFO_W + 1] * y0 + info[:, INFO_W + 1:INFO_W + 2] * y1)
    if final:
        h = _rms(x, gain_ref[...])

        @pl.when(i < tiles_a)
        def _():
            outa_ref[...] = h

        @pl.when(i >= tiles_a)
        def _():
            outb_ref[...] = h
    else:
        out_ref[...] = x


def _combine(x, info, pos, ys, final_gain=None, n_a=None):
    n, d = x.shape
    tm = TOKEN_TILE
    final = final_gain is not None
    tiles_a = n_a // tm if final else None
    row = lambda width: pl.BlockSpec((tm, width), lambda i, pos: (i, 0))
    in_specs = [row(d), row(LANES)]
    args = [x, info]
    if final:
        in_specs.append(pl.BlockSpec((1, d), lambda i, pos: (0, 0)))
        args.append(final_gain)
        out_specs = [pl.BlockSpec((tm, d), lambda i, pos: (jnp.minimum(i, tiles_a - 1), 0)),
                     pl.BlockSpec((tm, d), lambda i, pos: (jnp.maximum(i - tiles_a, 0), 0))]
        out_shape = [jax.ShapeDtypeStruct((n_a, d), F32), jax.ShapeDtypeStruct((n - n_a, d), F32)]
    else:
        out_specs = row(d)
        out_shape = jax.ShapeDtypeStruct((n, d), F32)
    in_specs.append(pl.BlockSpec(memory_space=pl.ANY))
    args.append(ys)
    grid_spec = pltpu.PrefetchScalarGridSpec(
        num_scalar_prefetch=1, grid=(n // tm,), in_specs=in_specs, out_specs=out_specs,
        scratch_shapes=[pltpu.VMEM((2, TOP_K, tm * ROW_TILE, LANES), F32), pltpu.SemaphoreType.DMA((2,))])
    return pl.pallas_call(
        functools.partial(_combine_kernel, final=final, tiles_a=tiles_a),
        grid_spec=grid_spec,
        out_shape=out_shape,
        compiler_params=_params(("arbitrary",)),
        name="moe_combine",
    )(pos, *args)


PROJ_CHUNK = 256


def _project_rows(x_refs, gain_ref, w_ref, p_scr, h_scr, c, first_cols):
    x = jnp.concatenate([r[...] for r in x_refs], axis=0) if len(x_refs) > 1 else x_refs[0][...]
    h_scr[...] = _rms(x, gain_ref[...]).astype(BF16)

    def piece(c0, c1):
        def run():
            p_scr[:, c0:c1] = jnp.dot(h_scr[...], w_ref[:, c0:c1], preferred_element_type=F32)
        return run

    piece(0, first_cols)()
    cols = p_scr.shape[1]
    fillers = [piece(c0, min(c0 + PROJ_CHUNK, cols)) for c0 in range(first_cols, cols, PROJ_CHUNK)]
    return [p_scr.at[pl.ds(b * c, c)] for b in range(len(x_refs))], fillers


def _ab_mixer_kernel(*refs, c, nseq, zero_init):
    x_refs, refs = refs[:nseq], refs[nseq:]
    if zero_init:
        gain_ref, win_ref, cw_ref, prm_ref, gkw_ref, gkb_ref, o_ref, conv_ref, sdn_ref, sgla_ref, p_scr, h_scr = refs
    else:
        (gain_ref, win_ref, cw_ref, prm_ref, gkw_ref, gkb_ref, conv_in, sdn_in, sgla_in,
         o_ref, conv_ref, sdn_ref, sgla_ref, p_scr, h_scr) = refs

    @pl.when(pl.program_id(1) == 0)
    def _():
        if zero_init:
            conv_ref[...] = jnp.zeros_like(conv_ref)
            sdn_ref[...] = jnp.zeros_like(sdn_ref)
            sgla_ref[...] = jnp.zeros_like(sgla_ref)
        else:
            conv_ref[...] = conv_in[...]
            sdn_ref[...] = sdn_in[...]
            sgla_ref[...] = sgla_in[...]

    p_refs, fillers = _project_rows(x_refs, gain_ref, win_ref, p_scr, h_scr, c, AB_Z)

    m = min(GLA_SUB_BLOCK, c)
    tri = _tri_incl(c)
    cw = cw_ref[...]
    a_log, dt_bias = prm_ref[0:1], prm_ref[1:2]
    dn_gain, gla_gain = prm_ref[2:3], prm_ref[3:4]
    n_pairs = GLA_HEADS // GLA_PACK
    conv_old = [conv_ref[b] for b in range(nseq)]
    sdn_old = [[sdn_ref[b, h] for h in range(DN_HEADS)] for b in range(nseq)]
    sgla_old = [[sgla_ref[b, pr] for pr in range(n_pairs)] for b in range(nseq)]
    dn = {key: [] for key in ("q", "k", "v", "gc", "gr", "beta", "s")}
    miscs = []
    for b in range(nseq):
        p_ref = p_refs[b]
        u = p_ref[:, AB_U:AB_U + CONV_CH]
        ucat = jnp.concatenate([conv_old[b], u], axis=0)
        acc = u * cw[CONV_W - 1:CONV_W]
        for j in range(1, CONV_W):
            acc = acc + pltpu.roll(ucat, j, 0)[SUBLANES:SUBLANES + c] * cw[CONV_W - 1 - j:CONV_W - j]
        conv_ref[b] = ucat[c:c + SUBLANES]
        qkv = acc * _sigmoid(acc)

        misc = p_ref[:, AB_MISC:AB_MISC + LANES]
        miscs.append(misc)
        g_all = -jnp.exp(a_log) * _softplus(misc + dt_bias)
        beta_all = _sigmoid(misc)
        gcum = _mm_cum(tri, g_all)
        gcum_t = jnp.concatenate([gcum, jnp.zeros((LANES - c, LANES), F32)], axis=0).T
        for h in range(DN_HEADS):
            lo = h * DN_DK
            q = qkv[:, lo:lo + DN_DK]
            k = qkv[:, DN_QK + lo:DN_QK + lo + DN_DK]
            dn["q"].append(q * lax.rsqrt(jnp.sum(q * q, axis=-1, keepdims=True) + L2_EPS))
            dn["k"].append(k * lax.rsqrt(jnp.sum(k * k, axis=-1, keepdims=True) + L2_EPS))
            dn["v"].append(qkv[:, 2 * DN_QK + h * DN_DV:2 * DN_QK + (h + 1) * DN_DV])
            dn["gc"].append(gcum[:, MISC_A + h:MISC_A + h + 1])
            dn["gr"].append(gcum_t[MISC_A + h:MISC_A + h + 1, :c])
            dn["beta"].append(beta_all[:, MISC_B + h:MISC_B + h + 1])
            dn["s"].append(sdn_old[b][h])
    os_, ss_ = _delta_chunks(dn["q"], dn["k"], dn["v"], dn["gc"], dn["gr"], dn["beta"], dn["s"], fillers)
    for b in range(nseq):
        for h in range(DN_HEADS):
            i = b * DN_HEADS + h
            sdn_ref[b, h] = ss_[i]
            z = p_refs[b][:, AB_Z + h * DN_DV:AB_Z + (h + 1) * DN_DV]
            o_ref[b, :, h * DN_DV:(h + 1) * DN_DV] = _gated_rms(os_[i], z, dn_gain).astype(o_ref.dtype)

    gl = {key: [] for key in ("q", "k", "g", "v", "s")}
    for b in range(nseq):
        p_ref = p_refs[b]
        logits = _mm_split(miscs[b], gkw_ref[...]) + gkb_ref[...]
        logf = -_softplus(-logits) * (1.0 / GLA_NORMALIZER)
        gall = _mm_cum(tri, logf)
        for pair in range(n_pairs):
            lo = pair * LANES
            gl["q"].append(p_ref[:, AB_GQ + lo:AB_GQ + lo + LANES] * (GLA_DK ** -0.5))
            gl["k"].append(p_ref[:, AB_GK + lo:AB_GK + lo + LANES])
            gl["g"].append(gall[:, lo:lo + LANES])
            gl["v"].append([p_ref[:, AB_GV + h * GLA_DV:AB_GV + (h + 1) * GLA_DV]
                            for h in range(pair * GLA_PACK, (pair + 1) * GLA_PACK)])
            gl["s"].append(sgla_old[b][pair])
    os_, ss_ = _gla_chunks(gl["q"], gl["k"], gl["g"], gl["v"], gl["s"], m)
    for b in range(nseq):
        for pair in range(n_pairs):
            i = b * n_pairs + pair
            sgla_ref[b, pair] = ss_[i]
            for hh in range(GLA_PACK):
                h = pair * GLA_PACK + hh
                gate = p_refs[b][:, AB_GG + h * GLA_DV:AB_GG + (h + 1) * GLA_DV]
                o_ref[b, :, DN_V + h * GLA_DV:DN_V + (h + 1) * GLA_DV] = \
                    _gated_rms(os_[i][hh], gate, gla_gain).astype(o_ref.dtype)


def _seq_specs(cols, row0, c, nchunk, nseq):
    blk0 = row0 // c
    return [pl.BlockSpec((c, cols), lambda g, i, s=s: (blk0 + (g * nseq + s) * nchunk + i, 0)) for s in range(nseq)]


def _ab_mixer(x, row0, nb, t, nseq, gain, w_in, cw, prm, gkw, gkb, states):
    c = min(CHUNK, t)
    nchunk = t // c
    n_groups = nb // nseq
    zero_init = states is None
    st_map3 = lambda g, i: (g, 0, 0)
    st_map4 = lambda g, i: (g, 0, 0, 0)
    const = lambda shape: pl.BlockSpec(shape, lambda g, i: (0,) * len(shape))
    consts = [gain, w_in, cw, prm, gkw, gkb]
    in_specs = _seq_specs(D_MODEL, row0, c, nchunk, nseq) + [const(a.shape) for a in consts]
    args = [x] * nseq + consts
    gla_packed = (GLA_HEADS // GLA_PACK, GLA_PACK * GLA_DK, GLA_DV)
    st_specs = [pl.BlockSpec((nseq, SUBLANES, CONV_CH), st_map3),
                pl.BlockSpec((nseq, DN_HEADS, DN_DK, DN_DV), st_map4),
                pl.BlockSpec((nseq,) + gla_packed, st_map4)]
    if not zero_init:
        conv8, sdn, sgla = states
        in_specs += st_specs
        args += [conv8, sdn, sgla.reshape((nb,) + gla_packed)]
    out_shape = [jax.ShapeDtypeStruct((nb, t, D_MODEL), BF16),
                 jax.ShapeDtypeStruct((nb, SUBLANES, CONV_CH), F32),
                 jax.ShapeDtypeStruct((nb, DN_HEADS, DN_DK, DN_DV), F32),
                 jax.ShapeDtypeStruct((nb,) + gla_packed, F32)]
    out_specs = [pl.BlockSpec((nseq, c, D_MODEL), lambda g, i: (g, i, 0))] + st_specs
    o, conv_out, sdn_out, sgla_out = pl.pallas_call(
        functools.partial(_ab_mixer_kernel, c=c, nseq=nseq, zero_init=zero_init),
        grid=(n_groups, nchunk),
        in_specs=in_specs,
        out_specs=out_specs,
        out_shape=out_shape,
        scratch_shapes=[pltpu.VMEM((nseq * c, AB_COLS), F32), pltpu.VMEM((nseq * c, D_MODEL), BF16)],
        compiler_params=_params(("parallel", "arbitrary")),
        name="ab_mixer",
    )(*args)
    return o.reshape(nb * t, D_MODEL), conv_out, sdn_out, sgla_out.reshape(nb, GLA_HEADS, GLA_DK, GLA_DV)


def _hgrn_mixer_kernel(*refs, c, nseq, zero_init, layer):
    x_refs, refs = refs[:nseq], refs[nseq:]
    if zero_init:
        mixgain_ref, win_ref, lb_ref, gain_ref, o_ref, s_ref, p_scr, h_scr = refs
    else:
        mixgain_ref, win_ref, lb_ref, gain_ref, s_in, o_ref, s_ref, p_scr, h_scr = refs
    p_refs, fillers = _project_rows(x_refs, mixgain_ref, win_ref, p_scr, h_scr, c, 2 * HG_F)

    @pl.when(pl.program_id(1) == 0)
    def _():
        if zero_init:
            s_ref[...] = jnp.zeros_like(s_ref)
        else:
            s_ref[...] = s_in[...]

    raw = lb_ref[...]
    e = jnp.exp(raw - jnp.max(raw, axis=0, keepdims=True))
    sm = e / jnp.sum(e, axis=0, keepdims=True)
    cum = sm[0:1]
    for l in range(1, layer + 1):
        cum = cum + sm[l:l + 1]
    lb = cum - sm[0:1]

    m = min(GLA_SUB_BLOCK, c)
    tri = _tri_incl(c)
    gain = gain_ref[...]
    pr = {key: [] for key in ("q", "k", "g", "v", "s")}
    for b in range(nseq):
        p_ref = p_refs[b]
        fr = p_ref[:, HG_F:2 * HG_F]
        logf = jnp.log(lb + (1.0 - lb) * _sigmoid(fr))
        kall = (1.0 - lb) * _sigmoid(-fr)
        gall = _mm_cum(tri, logf)
        for h in range(HG_HEADS):
            lo = h * HG_DK
            pr["q"].append(p_ref[:, lo:lo + HG_DK] * (HG_DK ** -0.5))
            pr["k"].append(kall[:, lo:lo + HG_DK])
            pr["g"].append(gall[:, lo:lo + HG_DK])
            pr["v"].append([functools.partial(lambda r, h: r[:, 2 * HG_F + h * HG_DV:2 * HG_F + (h + 1) * HG_DV], p_ref, h)])
            pr["s"].append(s_ref[b, h])
    os_, ss_ = _gla_chunks(pr["q"], pr["k"], pr["g"], pr["v"], pr["s"], m, fillers)
    for b in range(nseq):
        for h in range(HG_HEADS):
            i = b * HG_HEADS + h
            s_ref[b, h] = ss_[i]
            gate = p_refs[b][:, 2 * HG_F + D_MODEL + h * HG_DV:2 * HG_F + D_MODEL + (h + 1) * HG_DV]
            o_ref[b, :, h * HG_DV:(h + 1) * HG_DV] = _gated_rms(os_[i][0], gate, gain).astype(o_ref.dtype)


def _hgrn_mixer(x, row0, nb, t, nseq, mix_gain, w_in, lb, gain, state, layer):
    c = min(CHUNK, t)
    nchunk = t // c
    zero_init = state is None
    const = lambda shape: pl.BlockSpec(shape, lambda g, i: (0,) * len(shape))
    st_spec = pl.BlockSpec((nseq, HG_HEADS, HG_DK, HG_DV), lambda g, i: (g, 0, 0, 0))
    consts = [mix_gain, w_in, lb, gain]
    in_specs = _seq_specs(D_MODEL, row0, c, nchunk, nseq) + [const(a.shape) for a in consts]
    args = [x] * nseq + consts
    if not zero_init:
        in_specs.append(st_spec)
        args.append(state)
    o, s_out = pl.pallas_call(
        functools.partial(_hgrn_mixer_kernel, c=c, nseq=nseq, zero_init=zero_init, layer=layer),
        grid=(nb // nseq, nchunk),
        in_specs=in_specs,
        out_specs=[pl.BlockSpec((nseq, c, D_MODEL), lambda g, i: (g, i, 0)), st_spec],
        out_shape=[jax.ShapeDtypeStruct((nb, t, D_MODEL), BF16),
                   jax.ShapeDtypeStruct((nb, HG_HEADS, HG_DK, HG_DV), F32)],
        scratch_shapes=[pltpu.VMEM((nseq * c, w_in.shape[1]), F32), pltpu.VMEM((nseq * c, D_MODEL), BF16)],
        compiler_params=_params(("parallel", "arbitrary")),
        name="hgrn_mixer",
    )(*args)
    return o.reshape(nb * t, D_MODEL), s_out


def _outproj_router_kernel(*refs, tiles_a, split_x):
    if split_x:
        oa_ref, ob_ref, xa_ref, xb_ref = refs[:4]
        refs = refs[4:]
    else:
        oa_ref, ob_ref, x_ref = refs[:3]
        refs = refs[3:]
    wout_ref, gain_ref, wr_ref, br_ref, x1_ref, h_ref, info_ref, cnt_ref, run_ref = refs
    first = pl.program_id(0) < tiles_a

    @pl.when(pl.program_id(0) == 0)
    def _():
        run_ref[...] = jnp.zeros_like(run_ref)

    o = jnp.where(first, oa_ref[...], ob_ref[...])
    x = jnp.where(first, xa_ref[...], xb_ref[...]) if split_x else x_ref[...]
    x1 = x + jnp.dot(o, wout_ref[...], preferred_element_type=F32)
    x1_ref[...] = x1
    h = _rms(x1, gain_ref[...])
    _store_token_tiles(h_ref, h)
    logit = _mm_split(h, wr_ref[...]) + br_ref[...]
    lane = lax.broadcasted_iota(I32, logit.shape, 1)
    far = jnp.int32(LANES)
    lg = jnp.where(lane < N_GROUPS, logit, NEG)
    mg = jnp.max(lg, axis=-1, keepdims=True)
    p_g = 1.0 / jnp.sum(jnp.exp(lg - mg), axis=-1, keepdims=True)
    g_top = jnp.min(jnp.where(lg == mg, lane, far), axis=-1, keepdims=True)
    lo = N_GROUPS + g_top * EXPERTS_PER_GROUP
    sel = jnp.logical_and(lane >= lo, lane < lo + EXPERTS_PER_GROUP)
    le = jnp.where(sel, logit, NEG)
    ee = jnp.exp(le - jnp.max(le, axis=-1, keepdims=True))
    pe = jnp.where(sel, ee / jnp.sum(ee, axis=-1, keepdims=True), -1.0)
    p1 = jnp.max(pe, axis=-1, keepdims=True)
    i1 = jnp.min(jnp.where(pe == p1, lane, far), axis=-1, keepdims=True)
    pe2 = jnp.where(lane == i1, -1.0, pe)
    p2 = jnp.max(pe2, axis=-1, keepdims=True)
    i2 = jnp.min(jnp.where(pe2 == p2, lane, far), axis=-1, keepdims=True)
    denom = p1 + p2
    w1 = p_g * (p1 / denom)
    w2 = p_g * (p2 / denom)
    tm = logit.shape[0]
    oh1 = jnp.where(lane == i1, 1.0, 0.0)
    oh2 = jnp.where(lane == i2, 1.0, 0.0)
    both = oh1 + oh2
    tr = lax.broadcasted_iota(I32, (tm, tm), 0)
    tc = lax.broadcasted_iota(I32, (tm, tm), 1)
    before = run_ref[...] + _mm(jnp.where(tr > tc, 1.0, 0.0), both)
    r1 = jnp.sum(oh1 * before, axis=-1, keepdims=True)
    r2 = jnp.sum(oh2 * before, axis=-1, keepdims=True)
    run = run_ref[...] + jnp.sum(both, axis=0, keepdims=True)
    run_ref[...] = run
    cnt_ref[...] = jnp.broadcast_to(run, cnt_ref.shape)
    vals = ((i1 - N_GROUPS).astype(F32), (i2 - N_GROUPS).astype(F32), w1, w2, r1, r2)
    info = jnp.zeros_like(logit)
    for idx, val in enumerate(vals):
        info = jnp.where(lane == idx, val, info)
    info_ref[...] = info


def _outproj_router(o_a, o_b, xs, wout, gain, wr, br):
    n_a, d = o_a.shape
    n = n_a + o_b.shape[0]
    tm = TOKEN_TILE
    tiles_a = n_a // tm
    row = lambda w: pl.BlockSpec((tm, w), lambda i: (i, 0))
    seg_a = pl.BlockSpec((tm, d), lambda i: (jnp.minimum(i, tiles_a - 1), 0))
    seg_b = pl.BlockSpec((tm, d), lambda i: (jnp.maximum(i - tiles_a, 0), 0))
    const = lambda shape: pl.BlockSpec(shape, lambda i: (0,) * len(shape))
    split_x = len(xs) == 2
    return pl.pallas_call(
        functools.partial(_outproj_router_kernel, tiles_a=tiles_a, split_x=split_x),
        grid=(n // tm,),
        in_specs=[seg_a, seg_b] + ([seg_a, seg_b] if split_x else [row(d)])
        + [const((d, d)), const((1, d)), const((d, LANES)), const((1, LANES))],
        out_specs=[row(d), pl.BlockSpec((tm * ROW_TILE, LANES), lambda i: (i, 0)), row(LANES), const((SUBLANES, LANES))],
        out_shape=[jax.ShapeDtypeStruct((n, d), F32), jax.ShapeDtypeStruct((n * ROW_TILE, LANES), F32),
                   jax.ShapeDtypeStruct((n, LANES), F32), jax.ShapeDtypeStruct((SUBLANES, LANES), F32)],
        scratch_shapes=[pltpu.VMEM((1, LANES), F32)],
        compiler_params=_params(("arbitrary",)),
        name="outproj_router",
    )(o_a, o_b, *xs, wout, gain, wr, br)


def _dispatch_kernel(pos_ref, lo_ref, hi_ref, h_ref, xs_hbm, buf, zrow, sem, zsem):
    i = pl.program_id(0)
    n = pl.num_programs(0)
    tm = h_ref.shape[0] // ROW_TILE
    blk = tm * ROW_TILE
    slot = i % 2
    token = lambda r: pl.ds(pl.multiple_of(r * ROW_TILE, ROW_TILE), ROW_TILE)

    def tile_wait(s):
        for _ in range(TOP_K):
            pltpu.make_async_copy(buf.at[s], xs_hbm.at[pl.ds(0, blk)], sem.at[s]).wait()

    zblk = zrow.shape[0]

    def pad_rows(fn, tile_fn):
        for e in range(N_EXPERTS):
            lax.fori_loop(lo_ref[e], hi_ref[e], fn, 0)
        lax.fori_loop(hi_ref[N_EXPERTS - 1] // (zblk // ROW_TILE), xs_hbm.shape[0] // zblk, tile_fn, 0)

    def pad_start(r, carry):
        pltpu.make_async_copy(zrow.at[token(0)], xs_hbm.at[token(r)], zsem.at[0]).start()
        return carry

    def pad_wait(r, carry):
        pltpu.make_async_copy(zrow.at[token(0)], xs_hbm.at[token(0)], zsem.at[0]).wait()
        return carry

    def pad_tile_start(t, carry):
        pltpu.make_async_copy(zrow, xs_hbm.at[pl.ds(pl.multiple_of(t * zblk, zblk), zblk)], zsem.at[0]).start()
        return carry

    def pad_tile_wait(t, carry):
        pltpu.make_async_copy(zrow, xs_hbm.at[pl.ds(0, zblk)], zsem.at[0]).wait()
        return carry

    @pl.when(i == 0)
    def _():
        zrow[...] = jnp.zeros_like(zrow)
        pad_rows(pad_start, pad_tile_start)

    @pl.when(i >= 2)
    def _():
        tile_wait(slot)
    buf[slot] = h_ref[...]

    def body(r, carry):
        for k in range(TOP_K):
            row = pos_ref[TOP_K * (i * tm + r) + k]
            pltpu.make_async_copy(buf.at[slot, token(r)], xs_hbm.at[token(row)], sem.at[slot]).start(priority=k % 2)
        return carry
    lax.fori_loop(0, tm, body, 0, unroll=8)

    @pl.when(i == n - 1)
    def _():
        tile_wait(slot)

        @pl.when(n > 1)
        def _():
            tile_wait(1 - slot)
        pad_rows(pad_wait, pad_tile_wait)


def _dispatch(h, pos, pad_lo, pad_hi, n_rows):
    blk = TOKEN_TILE * ROW_TILE
    n = h.shape[0] // ROW_TILE
    grid_spec = pltpu.PrefetchScalarGridSpec(
        num_scalar_prefetch=3,
        grid=(n // TOKEN_TILE,),
        in_specs=[pl.BlockSpec((blk, LANES), lambda i, pos, lo, hi: (i, 0))],
        out_specs=pl.BlockSpec(memory_space=pl.ANY),
        scratch_shapes=[pltpu.VMEM((2, blk, LANES), F32), pltpu.VMEM((MOE_TILE * ROW_TILE, LANES), F32),
                        pltpu.SemaphoreType.DMA((2,)), pltpu.SemaphoreType.DMA((1,))],
    )
    return pl.pallas_call(
        _dispatch_kernel,
        grid_spec=grid_spec,
        out_shape=jax.ShapeDtypeStruct((n_rows * ROW_TILE, LANES), F32),
        compiler_params=_params(("arbitrary",)),
        name="moe_dispatch",
    )(pos, pad_lo, pad_hi, h)


def _moe_kernel(te_ref, nact_ref, x_ref, w1_ref, w3_ref, w2_ref, y_ref, w1b, w3b, w2b):
    i = pl.program_id(0)

    @pl.when(i < nact_ref[0])
    def _():
        @pl.when(jnp.logical_or(i == 0, te_ref[i] != te_ref[jnp.maximum(i - 1, 0)]))
        def _():
            w1b[...] = w1_ref[0, 0].astype(BF16)
            w3b[...] = w3_ref[0, 0].astype(BF16)
            w2b[...] = w2_ref[0, 0].astype(BF16)

        xb = _load_token_tiles(x_ref).astype(BF16)
        h1 = jnp.dot(xb, w1b[...], preferred_element_type=F32)
        h3 = jnp.dot(xb, w3b[...], preferred_element_type=F32)
        act = (h1 * _sigmoid(h1)) * h3
        _store_token_tiles(y_ref, jnp.dot(act.astype(BF16), w2b[...], preferred_element_type=F32))

    @pl.when(i >= nact_ref[0])
    def _():
        y_ref[...] = jnp.zeros_like(y_ref)


def _moe(xs, te, nact, w1, w3, w2, layer):
    d, f = w1.shape[2], w1.shape[3]
    tm = MOE_TILE
    blk = tm * ROW_TILE
    n_tiles = xs.shape[0] // blk
    last = lambda i, nact: jnp.maximum(jnp.minimum(i, nact[0] - 1), 0)
    grid_spec = pltpu.PrefetchScalarGridSpec(
        num_scalar_prefetch=2,
        grid=(n_tiles,),
        in_specs=[
            pl.BlockSpec((blk, LANES), lambda i, te, nact: (last(i, nact), 0)),
            pl.BlockSpec((1, 1, d, f), lambda i, te, nact: (layer, te[i], 0, 0)),
            pl.BlockSpec((1, 1, d, f), lambda i, te, nact: (layer, te[i], 0, 0)),
            pl.BlockSpec((1, 1, f, d), lambda i, te, nact: (layer, te[i], 0, 0)),
        ],
        out_specs=pl.BlockSpec((blk, LANES), lambda i, te, nact: (i, 0)),
        scratch_shapes=[pltpu.VMEM((d, f), BF16), pltpu.VMEM((d, f), BF16), pltpu.VMEM((f, d), BF16)],
    )
    return pl.pallas_call(
        _moe_kernel,
        grid_spec=grid_spec,
        out_shape=jax.ShapeDtypeStruct(xs.shape, F32),
        compiler_params=_params(("arbitrary",)),
        name="moe_experts",
    )(te, nact, xs, w1, w3, w2)


def _moe_rows(n):
    return ((TOP_K * n + N_EXPERTS * (MOE_TILE - 1)) // MOE_TILE) * MOE_TILE


def _route_tables(info, counts_row, n):
    tm = MOE_TILE
    n_tiles = _moe_rows(n) // tm
    ids = jnp.arange(N_EXPERTS, dtype=I32)
    counts = counts_row[0, N_GROUPS:N_GROUPS + N_EXPERTS].astype(I32)
    padded = ((counts + tm - 1) // tm) * tm
    cum = jnp.cumsum(padded)
    off = cum - padded
    e = info[:, INFO_E:INFO_E + TOP_K].astype(I32)
    rank = info[:, INFO_R:INFO_R + TOP_K].astype(I32)
    pos = (jnp.sum(jnp.where(e[:, :, None] == ids, off, 0), axis=-1) + rank).reshape(TOP_K * n)
    tile_start = jnp.arange(n_tiles, dtype=I32) * tm
    nact = cum[-1] // tm
    te_raw = jnp.sum((cum[None, :] <= tile_start[:, None]).astype(I32), axis=1)
    te_last = jnp.sum((cum <= (nact - 1) * tm).astype(I32))
    te = jnp.where(tile_start < cum[-1], jnp.minimum(te_raw, N_EXPERTS - 1), te_last)
    return pos, off + counts, cum, te, nact.reshape(1)


def _pad_lanes(a, width=LANES):
    return jnp.pad(a, [(0, 0)] * (a.ndim - 1) + [(0, width - a.shape[-1])])


def _seq_group(nb, want):
    g = want
    while nb % g:
        g //= 2
    return g


PROMPT_SEQS_PER_STEP = 8
SAMPLE_SEQS_PER_STEP = 16


def kernel(x_prompt, x_sample, state_conv_ab, state_delta_ab, state_gla_ab, state_hgrn_c, mix_norm, ab_w_in, ab_conv_w, ab_a_log, ab_dt_bias, ab_dn_norm, ab_gk_w2, ab_gk_b, ab_gla_norm, ab_w_out, c_w_in, c_lower_bounds, c_norm, c_w_out, ffn_norm, moe_w_group, moe_b_group, moe_w_expert, moe_b_expert, moe_w1, moe_w3, moe_w2, final_norm):
    bp, tp, d = x_prompt.shape
    bs, ts, _ = x_sample.shape
    n_p, n_s = bp * tp, bs * ts
    n = n_p + n_s
    depth = mix_norm.shape[0]
    xs_res = (x_prompt.reshape(n_p, d), x_sample.reshape(n_s, d))
    pg, sg = _seq_group(bp, PROMPT_SEQS_PER_STEP), _seq_group(bs, SAMPLE_SEQS_PER_STEP)

    info = pos = ys = None
    conv_p, delta_p, gla_p, hgrn_p = [], [], [], []
    conv_s, delta_s, gla_s, hgrn_s = [], [], [], []
    for layer in range(depth):
        j = layer // 2
        gain = mix_norm[layer].reshape(1, d)
        if layer % 2 == 0:
            u_w, a_w, b_w, z_w, gq_w, gk_w, gv_w, lr_w, gg_w = jnp.split(ab_w_in[j], _split_points(AB_SPLITS), axis=1)
            w_in = jnp.concatenate([u_w, _pad_lanes(jnp.concatenate([a_w, b_w, lr_w], axis=1)), z_w, gq_w, gk_w, gv_w, gg_w],
                                   axis=1).astype(BF16)
        else:
            w_in = c_w_in[j].astype(BF16)
        if ys is not None:
            xs_res = (_combine(xs_res[0], info, pos, ys),)
        (x_p, off_p), (x_s, off_s) = ((xs_res[0], 0), (xs_res[-1], 0 if len(xs_res) == 2 else n_p))

        if layer % 2 == 0:
            prm = jnp.concatenate([_pad_lanes(ab_a_log[j].reshape(1, -1)), _pad_lanes(ab_dt_bias[j].reshape(1, -1)),
                                   ab_dn_norm[j].reshape(1, -1), ab_gla_norm[j].reshape(1, -1),
                                   jnp.zeros((4, LANES), F32)], axis=0)
            gkw = jnp.zeros((LANES, GLA_K), F32).at[MISC_LR:MISC_LR + GLA_RANK].set(ab_gk_w2[j])
            gkb = ab_gk_b[j].reshape(1, GLA_K)
            cw = ab_conv_w[j]
            o_p, c8, sd, sgl = _ab_mixer(x_p, off_p, bp, tp, pg, gain, w_in, cw, prm, gkw, gkb, None)
            conv_p.append(c8[:, SUBLANES - (CONV_W - 1):])
            delta_p.append(sd)
            gla_p.append(sgl)
            conv8 = jnp.pad(state_conv_ab[j], ((0, 0), (SUBLANES - (CONV_W - 1), 0), (0, 0)))
            o_s, c8, sd, sgl = _ab_mixer(x_s, off_s, bs, ts, sg, gain, w_in, cw, prm, gkw, gkb,
                                         (conv8, state_delta_ab[j], state_gla_ab[j]))
            conv_s.append(c8[:, SUBLANES - (CONV_W - 1):])
            delta_s.append(sd)
            gla_s.append(sgl)
            w_out = ab_w_out[j]
        else:
            hg_gain = c_norm[j].reshape(1, -1)
            o_p, sh = _hgrn_mixer(x_p, off_p, bp, tp, pg, gain, w_in, c_lower_bounds, hg_gain, None, layer)
            hgrn_p.append(sh)
            o_s, sh = _hgrn_mixer(x_s, off_s, bs, ts, sg, gain, w_in, c_lower_bounds, hg_gain, state_hgrn_c[j], layer)
            hgrn_s.append(sh)
            w_out = c_w_out[j]

        wr = _pad_lanes(jnp.concatenate([moe_w_group[layer], moe_w_expert[layer]], axis=1))
        br = _pad_lanes(jnp.concatenate([moe_b_group[layer], moe_b_expert[layer]]).reshape(1, -1))
        x, hn, info, counts = _outproj_router(o_p, o_s, xs_res, w_out.astype(BF16), ffn_norm[layer].reshape(1, d), wr, br)
        xs_res = (x,)
        pos, pad_lo, pad_hi, te, nact = _route_tables(info, counts, n)
        xsort = _dispatch(hn, pos, pad_lo, pad_hi, _moe_rows(n))
        ys = _moe(xsort, te, nact, moe_w1, moe_w3, moe_w2, layer)

    y_p, y_s = _combine(xs_res[0], info, pos, ys, final_gain=final_norm.reshape(1, d), n_a=n_p)
    y_prompt = y_p.reshape(bp, tp, d)
    y_sample = y_s.reshape(bs, ts, d)
    return (y_prompt, y_sample, jnp.stack(conv_p), jnp.stack(delta_p), jnp.stack(gla_p), jnp.stack(hgrn_p),
            jnp.stack(conv_s), jnp.stack(delta_s), jnp.stack(gla_s), jnp.stack(hgrn_s))


def _split_points(sizes):
    pts, acc = [], 0
    for s in sizes[:-1]:
        acc += s
        pts.append(acc)
    return pts
```

```python
import functools

import jax
import jax.numpy as jnp
from jax import lax
from jax.experimental import pallas as pl
from jax.experimental.pallas import tpu as pltpu

F32 = jnp.float32
BF16 = jnp.bfloat16
I32 = jnp.int32

D_MODEL = 1024
EPS = 1e-6
CHUNK = 64
CONV_W = 4
DN_HEADS, DN_DK, DN_DV = 4, 128, 128
DN_QK = DN_HEADS * DN_DK
DN_V = DN_HEADS * DN_DV
CONV_CH = 2 * DN_QK + DN_V
GLA_HEADS, GLA_DK, GLA_DV = 4, 64, 128
GLA_K = GLA_HEADS * GLA_DK
GLA_V = GLA_HEADS * GLA_DV
GLA_RANK = 16
GLA_NORMALIZER = 16.0
GLA_PACK = 128 // GLA_DK
AB_SPLITS = (CONV_CH, DN_HEADS, DN_HEADS, DN_V, GLA_K, GLA_K, GLA_V, GLA_RANK, GLA_V)
HG_HEADS, HG_DK, HG_DV = 8, 128, 128
HG_F = HG_HEADS * HG_DK
N_GROUPS, EXPERTS_PER_GROUP = 4, 8
N_EXPERTS = N_GROUPS * EXPERTS_PER_GROUP
TOP_K = 2
L2_EPS = 1e-6

LANES = 128
SUBLANES = 8
VMEM_LIMIT_BYTES = 56 * 1024 * 1024

AB_U, AB_MISC, AB_Z, AB_GQ, AB_GK, AB_GV, AB_GG = 0, 1536, 1664, 2176, 2432, 2688, 3200
AB_COLS = AB_GG + GLA_V
MISC_A, MISC_B, MISC_LR = 0, DN_HEADS, 2 * DN_HEADS

INFO_E, INFO_W, INFO_R = 0, TOP_K, 2 * TOP_K

TOKEN_TILE = 512
MOE_TILE = 256
NEG = -1e30
LOG2E = 1.4426950408889634
GLA_SUB_BLOCK = SUBLANES


def _params(sem):
    return pltpu.CompilerParams(dimension_semantics=sem, vmem_limit_bytes=VMEM_LIMIT_BYTES)


def _mm(a, b):
    return jnp.dot(a.astype(BF16), b.astype(BF16), preferred_element_type=F32)


def _mm_nt(a, b):
    return lax.dot_general(a.astype(BF16), b.astype(BF16), (((1,), (1,)), ((), ())), preferred_element_type=F32)


def _mm_tn(a, b):
    return lax.dot_general(a.astype(BF16), b.astype(BF16), (((0,), (0,)), ((), ())), preferred_element_type=F32)


def _sigmoid(x):
    return 1.0 / (1.0 + jnp.exp(-x))


def _softplus(x):
    return jnp.maximum(x, 0.0) + jnp.log(1.0 + jnp.exp(-jnp.abs(x)))


def _rms(x, gain):
    return x * lax.rsqrt(jnp.mean(x * x, axis=-1, keepdims=True) + EPS) * gain


def _gated_rms(o, gate, gain):
    return _rms(o, gain) * (gate * _sigmoid(gate))


ROW_TILE = D_MODEL // LANES
assert ROW_TILE == SUBLANES


def _store_token_tiles(ref, x):
    tm = x.shape[0]
    for j in range(ROW_TILE):
        ref[pl.ds(j, tm, stride=ROW_TILE), :] = x[:, j * LANES:(j + 1) * LANES]


def _load_token_tiles(ref):
    tm = ref.shape[0] // ROW_TILE
    return jnp.concatenate([ref[pl.ds(j, tm, stride=ROW_TILE), :] for j in range(ROW_TILE)], axis=1)


def _row_to_col(row):
    n = row.shape[1]
    r = lax.broadcasted_iota(I32, (n, n), 0)
    c = lax.broadcasted_iota(I32, (n, n), 1)
    return jnp.sum(jnp.where(r == c, jnp.broadcast_to(row, (n, n)), 0.0), axis=1, keepdims=True)


def _split2(x):
    hi = x.astype(BF16)
    return hi, (x - hi.astype(F32)).astype(BF16)


def _split3(x):
    hi = x.astype(BF16)
    r = x - hi.astype(F32)
    mid = r.astype(BF16)
    return hi, mid, (r - mid.astype(F32)).astype(BF16)


def _mm_cum(tri, x):
    hi, mid, lo = _split3(x)
    dot = lambda p: jnp.dot(tri, p, preferred_element_type=F32)
    return dot(hi) + (dot(mid) + dot(lo))


def _mm_split(a, b):
    ah, al = _split2(a)
    bh, bl = _split2(b)
    dot = lambda x, y: jnp.dot(x, y, preferred_element_type=F32)
    return dot(ah, bh) + (dot(ah, bl) + dot(al, bh))


def _ticker(fillers):
    pending = list(fillers)

    def tick(flush=False):
        while pending:
            pending.pop(0)()
            if not flush:
                break
    return tick


def _delta_chunks(qs_, ks_, vs_, gcs_, grs_, betas_, ss_, fillers=()):
    tick = _ticker(fillers)
    heads = range(len(qs_))
    c = qs_[0].shape[0]
    ti = lax.broadcasted_iota(I32, (c, c), 0)
    tj = lax.broadcasted_iota(I32, (c, c), 1)
    eye = jnp.where(ti == tj, 1.0, 0.0)
    decay = [jnp.exp(jnp.where(ti >= tj, gcs_[h] - grs_[h], NEG)) for h in heads]
    qs = [qs_[h] * (DN_DK ** -0.5) for h in heads]
    kk = [_mm_nt(ks_[h], ks_[h]) for h in heads]
    nm = [jnp.where(ti > tj, betas_[h] * kk[h] * decay[h], 0.0) for h in heads]
    t = [eye - nm[h] for h in heads]
    p = nm
    step = 2
    while step < c:
        tick()
        p = [_mm(p[h], p[h]) for h in heads]
        t = [t[h] + _mm(t[h], p[h]) for h in heads]
        step *= 2
    eg = [jnp.exp(gcs_[h]) for h in heads]
    rhs = [jnp.concatenate([betas_[h] * vs_[h], (betas_[h] * eg[h]) * ks_[h]], axis=1) for h in heads]
    tick()
    sol = [_mm(t[h], rhs[h]) for h in heads]
    tick()
    resid = [rhs[h] - (sol[h] + _mm_split(nm[h], sol[h])) for h in heads]
    tick()
    sol = [sol[h] + _mm(t[h], resid[h]) for h in heads]
    qk = [_mm_nt(qs[h], ks_[h]) * decay[h] for h in heads]
    tick()
    u = [sol[h][:, :DN_DV] - _mm(sol[h][:, DN_DV:], ss_[h]) for h in heads]
    tick()
    o = [_mm(qs[h] * eg[h], ss_[h]) + _mm(qk[h], u[h]) for h in heads]
    gl = [gcs_[h][c - 1:c] for h in heads]
    s_new = [ss_[h] * jnp.exp(gl[h]) + _mm_tn(ks_[h] * jnp.exp(gl[h] - gcs_[h]), u[h]) for h in heads]
    tick(flush=True)
    return o, s_new


def _gla_chunks(qs, ks, gs, vss, ss, m, fillers=()):
    tick = _ticker(fillers)
    probs = range(len(qs))
    c = qs[0].shape[0]
    nh = len(vss[0])
    heads = range(nh)
    dk = LANES // nh
    dlane = lax.broadcasted_iota(I32, (1, LANES), 1)
    if nh == 1:
        mask = lambda x, h: x
    else:
        hm = [jnp.where(jnp.logical_and(dlane >= h * dk, dlane < (h + 1) * dk), 1.0, 0.0) for h in heads]
        mask = lambda x, h: x * hm[h]
    gs = [gs[i] * LOG2E for i in probs]
    qeg = [qs[i] * jnp.exp2(gs[i]) for i in probs]
    o = [[_mm(mask(qeg[i], h), ss[i]) for h in heads] for i in probs]
    lane = lax.broadcasted_iota(I32, (m, c), 1)
    sub = lax.broadcasted_iota(I32, (m, c), 0)
    blocks = [[[] for _ in heads] for _ in probs]
    for blk in range(c // m):
        tick()
        r0 = blk * m
        qb = [qs[i][r0:r0 + m] for i in probs]
        gb = [gs[i][r0:r0 + m] for i in probs]
        kb = [ks[i][r0:r0 + m] for i in probs]
        if blk > 0:
            base = [gs[i][r0 - 1:r0] for i in probs]
            kt = [jnp.concatenate([ks[i][:r0] * jnp.exp2(base[i] - gs[i][:r0]), jnp.zeros((c - r0, LANES), F32)], axis=0)
                  for i in probs]
            qt = [qb[i] * jnp.exp2(gb[i] - base[i]) for i in probs]
            att = [[_mm_nt(mask(qt[i], h), kt[i]) for h in heads] for i in probs]
        else:
            att = [[jnp.zeros((m, c), F32) for _ in heads] for _ in probs]
        for j in range(m):
            keep = jnp.logical_and(lane == r0 + j, sub >= j)
            for i in probs:
                prod = qb[i] * (kb[i][j:j + 1] * jnp.exp2(gb[i] - gb[i][j:j + 1]))
                for h in heads:
                    att[i][h] = jnp.where(keep, jnp.sum(mask(prod, h), axis=1, keepdims=True), att[i][h])
        for i in probs:
            for h in heads:
                blocks[i][h].append(att[i][h])
    tick(flush=True)
    s_new = []
    for i in probs:
        gl = gs[i][c - 1:c]
        kg = ks[i] * jnp.exp2(gl - gs[i])
        s_i = ss[i] * _row_to_col(jnp.exp2(gl))
        for h in heads:
            v = vss[i][h]() if callable(vss[i][h]) else vss[i][h]
            att = blocks[i][h][0] if len(blocks[i][h]) == 1 else jnp.concatenate(blocks[i][h], axis=0)
            o[i][h] = o[i][h] + _mm(att, v)
            s_i = s_i + _mm_tn(mask(kg, h), v)
        s_new.append(s_i)
    return o, s_new


def _tri_incl(c):
    r = lax.broadcasted_iota(I32, (c, c), 0)
    cc = lax.broadcasted_iota(I32, (c, c), 1)
    return jnp.where(r >= cc, 1.0, 0.0).astype(BF16)


def _combine_kernel(pos_ref, *refs, final, tiles_a):
    if final:
        x_ref, info_ref, gain_ref, ys_hbm, outa_ref, outb_ref, ybuf, sem = refs
    else:
        x_ref, info_ref, ys_hbm, out_ref, ybuf, sem = refs
    i = pl.program_id(0)
    n = pl.num_programs(0)
    tm = x_ref.shape[0]
    slot = i % 2

    def gather_start(tile, s):
        def body(r, carry):
            for k in range(TOP_K):
                src = pl.multiple_of(pos_ref[TOP_K * (tile * tm + r) + k] * ROW_TILE, ROW_TILE)
                dst = pl.multiple_of(r * ROW_TILE, ROW_TILE)
                pltpu.make_async_copy(ys_hbm.at[pl.ds(src, ROW_TILE)], ybuf.at[s, k, pl.ds(dst, ROW_TILE)],
                                      sem.at[s]).start(priority=k % 2)
            return carry
        lax.fori_loop(0, tm, body, 0, unroll=8)

    @pl.when(i == 0)
    def _():
        gather_start(0, 0)

    @pl.when(i + 1 < n)
    def _():
        gather_start(i + 1, 1 - slot)

    for k in range(TOP_K):
        pltpu.make_async_copy(ys_hbm.at[pl.ds(0, tm * ROW_TILE)], ybuf.at[slot, k], sem.at[slot]).wait()
    info = info_ref[...]
    y0, y1 = (_load_token_tiles(ybuf.at[slot, k]) for k in range(TOP_K))
    x = x_ref[...] + (info[:, INFO_W:INFO_W + 1] * y0 + info[:, INFO_W + 1:INFO_W + 2] * y1)
    if final:
        h = _rms(x, gain_ref[...])

        @pl.when(i < tiles_a)
        def _():
            outa_ref[...] = h

        @pl.when(i >= tiles_a)
        def _():
            outb_ref[...] = h
    else:
        out_ref[...] = x


def _combine(x, info, pos, ys, final_gain=None, n_a=None):
    n, d = x.shape
    tm = TOKEN_TILE
    final = final_gain is not None
    tiles_a = n_a // tm if final else None
    row = lambda width: pl.BlockSpec((tm, width), lambda i, pos: (i, 0))
    in_specs = [row(d), row(LANES)]
    args = [x, info]
    if final:
        in_specs.append(pl.BlockSpec((1, d), lambda i, pos: (0, 0)))
        args.append(final_gain)
        out_specs = [pl.BlockSpec((tm, d), lambda i, pos: (jnp.minimum(i, tiles_a - 1), 0)),
                     pl.BlockSpec((tm, d), lambda i, pos: (jnp.maximum(i - tiles_a, 0), 0))]
        out_shape = [jax.ShapeDtypeStruct((n_a, d), F32), jax.ShapeDtypeStruct((n - n_a, d), F32)]
    else:
        out_specs = row(d)
        out_shape = jax.ShapeDtypeStruct((n, d), F32)
    in_specs.append(pl.BlockSpec(memory_space=pl.ANY))
    args.append(ys)
    grid_spec = pltpu.PrefetchScalarGridSpec(
        num_scalar_prefetch=1, grid=(n // tm,), in_specs=in_specs, out_specs=out_specs,
        scratch_shapes=[pltpu.VMEM((2, TOP_K, tm * ROW_TILE, LANES), F32), pltpu.SemaphoreType.DMA((2,))])
    return pl.pallas_call(
        functools.partial(_combine_kernel, final=final, tiles_a=tiles_a),
        grid_spec=grid_spec,
        out_shape=out_shape,
        compiler_params=_params(("arbitrary",)),
        name="moe_combine",
    )(pos, *args)


PROJ_CHUNK = 256


def _project_rows(x_refs, gain_ref, w_ref, p_scr, h_scr, c, first_cols):
    x = jnp.concatenate([r[...] for r in x_refs], axis=0) if len(x_refs) > 1 else x_refs[0][...]
    h_scr[...] = _rms(x, gain_ref[...]).astype(BF16)

    def piece(c0, c1):
        def run():
            p_scr[:, c0:c1] = jnp.dot(h_scr[...], w_ref[:, c0:c1], preferred_element_type=F32)
        return run

    piece(0, first_cols)()
    cols = p_scr.shape[1]
    fillers = [piece(c0, min(c0 + PROJ_CHUNK, cols)) for c0 in range(first_cols, cols, PROJ_CHUNK)]
    return [p_scr.at[pl.ds(b * c, c)] for b in range(len(x_refs))], fillers


def _ab_mixer_kernel(*refs, c, nseq, zero_init):
    x_refs, refs = refs[:nseq], refs[nseq:]
    if zero_init:
        gain_ref, win_ref, cw_ref, prm_ref, gkw_ref, gkb_ref, o_ref, conv_ref, sdn_ref, sgla_ref, p_scr, h_scr = refs
    else:
        (gain_ref, win_ref, cw_ref, prm_ref, gkw_ref, gkb_ref, conv_in, sdn_in, sgla_in,
         o_ref, conv_ref, sdn_ref, sgla_ref, p_scr, h_scr) = refs

    @pl.when(pl.program_id(1) == 0)
    def _():
        if zero_init:
            conv_ref[...] = jnp.zeros_like(conv_ref)
            sdn_ref[...] = jnp.zeros_like(sdn_ref)
            sgla_ref[...] = jnp.zeros_like(sgla_ref)
        else:
            conv_ref[...] = conv_in[...]
            sdn_ref[...] = sdn_in[...]
            sgla_ref[...] = sgla_in[...]

    p_refs, fillers = _project_rows(x_refs, gain_ref, win_ref, p_scr, h_scr, c, AB_Z)

    m = min(GLA_SUB_BLOCK, c)
    tri = _tri_incl(c)
    cw = cw_ref[...]
    a_log, dt_bias = prm_ref[0:1], prm_ref[1:2]
    dn_gain, gla_gain = prm_ref[2:3], prm_ref[3:4]
    n_pairs = GLA_HEADS // GLA_PACK
    conv_old = [conv_ref[b] for b in range(nseq)]
    sdn_old = [[sdn_ref[b, h] for h in range(DN_HEADS)] for b in range(nseq)]
    sgla_old = [[sgla_ref[b, pr] for pr in range(n_pairs)] for b in range(nseq)]
    dn = {key: [] for key in ("q", "k", "v", "gc", "gr", "beta", "s")}
    miscs = []
    for b in range(nseq):
        p_ref = p_refs[b]
        u = p_ref[:, AB_U:AB_U + CONV_CH]
        ucat = jnp.concatenate([conv_old[b], u], axis=0)
        acc = u * cw[CONV_W - 1:CONV_W]
        for j in range(1, CONV_W):
            acc = acc + pltpu.roll(ucat, j, 0)[SUBLANES:SUBLANES + c] * cw[CONV_W - 1 - j:CONV_W - j]
        conv_ref[b] = ucat[c:c + SUBLANES]
        qkv = acc * _sigmoid(acc)

        misc = p_ref[:, AB_MISC:AB_MISC + LANES]
        miscs.append(misc)
        g_all = -jnp.exp(a_log) * _softplus(misc + dt_bias)
        beta_all = _sigmoid(misc)
        gcum = _mm_cum(tri, g_all)
        gcum_t = jnp.concatenate([gcum, jnp.zeros((LANES - c, LANES), F32)], axis=0).T
        for h in range(DN_HEADS):
            lo = h * DN_DK
            q = qkv[:, lo:lo + DN_DK]
            k = qkv[:, DN_QK + lo:DN_QK + lo + DN_DK]
            dn["q"].append(q * lax.rsqrt(jnp.sum(q * q, axis=-1, keepdims=True) + L2_EPS))
            dn["k"].append(k * lax.rsqrt(jnp.sum(k * k, axis=-1, keepdims=True) + L2_EPS))
            dn["v"].append(qkv[:, 2 * DN_QK + h * DN_DV:2 * DN_QK + (h + 1) * DN_DV])
            dn["gc"].append(gcum[:, MISC_A + h:MISC_A + h + 1])
            dn["gr"].append(gcum_t[MISC_A + h:MISC_A + h + 1, :c])
            dn["beta"].append(beta_all[:, MISC_B + h:MISC_B + h + 1])
            dn["s"].append(sdn_old[b][h])
    os_, ss_ = _delta_chunks(dn["q"], dn["k"], dn["v"], dn["gc"], dn["gr"], dn["beta"], dn["s"], fillers)
    for b in range(nseq):
        for h in range(DN_HEADS):
            i = b * DN_HEADS + h
            sdn_ref[b, h] = ss_[i]
            z = p_refs[b][:, AB_Z + h * DN_DV:AB_Z + (h + 1) * DN_DV]
            o_ref[b, :, h * DN_DV:(h + 1) * DN_DV] = _gated_rms(os_[i], z, dn_gain).astype(o_ref.dtype)

    gl = {key: [] for key in ("q", "k", "g", "v", "s")}
    for b in range(nseq):
        p_ref = p_refs[b]
        logits = _mm_split(miscs[b], gkw_ref[...]) + gkb_ref[...]
        logf = -_softplus(-logits) * (1.0 / GLA_NORMALIZER)
        gall = _mm_cum(tri, logf)
        for pair in range(n_pairs):
            lo = pair * LANES
            gl["q"].append(p_ref[:, AB_GQ + lo:AB_GQ + lo + LANES] * (GLA_DK ** -0.5))
            gl["k"].append(p_ref[:, AB_GK + lo:AB_GK + lo + LANES])
            gl["g"].append(gall[:, lo:lo + LANES])
            gl["v"].append([p_ref[:, AB_GV + h * GLA_DV:AB_GV + (h + 1) * GLA_DV]
                            for h in range(pair * GLA_PACK, (pair + 1) * GLA_PACK)])
            gl["s"].append(sgla_old[b][pair])
    os_, ss_ = _gla_chunks(gl["q"], gl["k"], gl["g"], gl["v"], gl["s"], m)
    for b in range(nseq):
        for pair in range(n_pairs):
            i = b * n_pairs + pair
            sgla_ref[b, pair] = ss_[i]
            for hh in range(GLA_PACK):
                h = pair * GLA_PACK + hh
                gate = p_refs[b][:, AB_GG + h * GLA_DV:AB_GG + (h + 1) * GLA_DV]
                o_ref[b, :, DN_V + h * GLA_DV:DN_V + (h + 1) * GLA_DV] = \
                    _gated_rms(os_[i][hh], gate, gla_gain).astype(o_ref.dtype)


def _seq_specs(cols, row0, c, nchunk, nseq):
    blk0 = row0 // c
    return [pl.BlockSpec((c, cols), lambda g, i, s=s: (blk0 + (g * nseq + s) * nchunk + i, 0)) for s in range(nseq)]


def _ab_mixer(x, row0, nb, t, nseq, gain, w_in, cw, prm, gkw, gkb, states):
    c = min(CHUNK, t)
    nchunk = t // c
    n_groups = nb // nseq
    zero_init = states is None
    st_map3 = lambda g, i: (g, 0, 0)
    st_map4 = lambda g, i: (g, 0, 0, 0)
    const = lambda shape: pl.BlockSpec(shape, lambda g, i: (0,) * len(shape))
    consts = [gain, w_in, cw, prm, gkw, gkb]
    in_specs = _seq_specs(D_MODEL, row0, c, nchunk, nseq) + [const(a.shape) for a in consts]
    args = [x] * nseq + consts
    gla_packed = (GLA_HEADS // GLA_PACK, GLA_PACK * GLA_DK, GLA_DV)
    st_specs = [pl.BlockSpec((nseq, SUBLANES, CONV_CH), st_map3),
                pl.BlockSpec((nseq, DN_HEADS, DN_DK, DN_DV), st_map4),
                pl.BlockSpec((nseq,) + gla_packed, st_map4)]
    if not zero_init:
        conv8, sdn, sgla = states
        in_specs += st_specs
        args += [conv8, sdn, sgla.reshape((nb,) + gla_packed)]
    out_shape = [jax.ShapeDtypeStruct((nb, t, D_MODEL), BF16),
                 jax.ShapeDtypeStruct((nb, SUBLANES, CONV_CH), F32),
                 jax.ShapeDtypeStruct((nb, DN_HEADS, DN_DK, DN_DV), F32),
                 jax.ShapeDtypeStruct((nb,) + gla_packed, F32)]
    out_specs = [pl.BlockSpec((nseq, c, D_MODEL), lambda g, i: (g, i, 0))] + st_specs
    o, conv_out, sdn_out, sgla_out = pl.pallas_call(
        functools.partial(_ab_mixer_kernel, c=c, nseq=nseq, zero_init=zero_init),
        grid=(n_groups, nchunk),
        in_specs=in_specs,
        out_specs=out_specs,
        out_shape=out_shape,
        scratch_shapes=[pltpu.VMEM((nseq * c, AB_COLS), F32), pltpu.VMEM((nseq * c, D_MODEL), BF16)],
        compiler_params=_params(("parallel", "arbitrary")),
        name="ab_mixer",
    )(*args)
    return o.reshape(nb * t, D_MODEL), conv_out, sdn_out, sgla_out.reshape(nb, GLA_HEADS, GLA_DK, GLA_DV)


def _hgrn_mixer_kernel(*refs, c, nseq, zero_init, layer):
    x_refs, refs = refs[:nseq], refs[nseq:]
    if zero_init:
        mixgain_ref, win_ref, lb_ref, gain_ref, o_ref, s_ref, p_scr, h_scr = refs
    else:
        mixgain_ref, win_ref, lb_ref, gain_ref, s_in, o_ref, s_ref, p_scr, h_scr = refs
    p_refs, fillers = _project_rows(x_refs, mixgain_ref, win_ref, p_scr, h_scr, c, 2 * HG_F)

    @pl.when(pl.program_id(1) == 0)
    def _():
        if zero_init:
            s_ref[...] = jnp.zeros_like(s_ref)
        else:
            s_ref[...] = s_in[...]

    raw = lb_ref[...]
    e = jnp.exp(raw - jnp.max(raw, axis=0, keepdims=True))
    sm = e / jnp.sum(e, axis=0, keepdims=True)
    cum = sm[0:1]
    for l in range(1, layer + 1):
        cum = cum + sm[l:l + 1]
    lb = cum - sm[0:1]

    m = min(GLA_SUB_BLOCK, c)
    tri = _tri_incl(c)
    gain = gain_ref[...]
    pr = {key: [] for key in ("q", "k", "g", "v", "s")}
    for b in range(nseq):
        p_ref = p_refs[b]
        fr = p_ref[:, HG_F:2 * HG_F]
        logf = jnp.log(lb + (1.0 - lb) * _sigmoid(fr))
        kall = (1.0 - lb) * _sigmoid(-fr)
        gall = _mm_cum(tri, logf)
        for h in range(HG_HEADS):
            lo = h * HG_DK
            pr["q"].append(p_ref[:, lo:lo + HG_DK] * (HG_DK ** -0.5))
            pr["k"].append(kall[:, lo:lo + HG_DK])
            pr["g"].append(gall[:, lo:lo + HG_DK])
            pr["v"].append([functools.partial(lambda r, h: r[:, 2 * HG_F + h * HG_DV:2 * HG_F + (h + 1) * HG_DV], p_ref, h)])
            pr["s"].append(s_ref[b, h])
    os_, ss_ = _gla_chunks(pr["q"], pr["k"], pr["g"], pr["v"], pr["s"], m, fillers)
    for b in range(nseq):
        for h in range(HG_HEADS):
            i = b * HG_HEADS + h
            s_ref[b, h] = ss_[i]
            gate = p_refs[b][:, 2 * HG_F + D_MODEL + h * HG_DV:2 * HG_F + D_MODEL + (h + 1) * HG_DV]
            o_ref[b, :, h * HG_DV:(h + 1) * HG_DV] = _gated_rms(os_[i][0], gate, gain).astype(o_ref.dtype)


def _hgrn_mixer(x, row0, nb, t, nseq, mix_gain, w_in, lb, gain, state, layer):
    c = min(CHUNK, t)
    nchunk = t // c
    zero_init = state is None
    const = lambda shape: pl.BlockSpec(shape, lambda g, i: (0,) * len(shape))
    st_spec = pl.BlockSpec((nseq, HG_HEADS, HG_DK, HG_DV), lambda g, i: (g, 0, 0, 0))
    consts = [mix_gain, w_in, lb, gain]
    in_specs = _seq_specs(D_MODEL, row0, c, nchunk, nseq) + [const(a.shape) for a in consts]
    args = [x] * nseq + consts
    if not zero_init:
        in_specs.append(st_spec)
        args.append(state)
    o, s_out = pl.pallas_call(
        functools.partial(_hgrn_mixer_kernel, c=c, nseq=nseq, zero_init=zero_init, layer=layer),
        grid=(nb // nseq, nchunk),
        in_specs=in_specs,
        out_specs=[pl.BlockSpec((nseq, c, D_MODEL), lambda g, i: (g, i, 0)), st_spec],
        out_shape=[jax.ShapeDtypeStruct((nb, t, D_MODEL), BF16),
                   jax.ShapeDtypeStruct((nb, HG_HEADS, HG_DK, HG_DV), F32)],
        scratch_shapes=[pltpu.VMEM((nseq * c, w_in.shape[1]), F32), pltpu.VMEM((nseq * c, D_MODEL), BF16)],
        compiler_params=_params(("parallel", "arbitrary")),
        name="hgrn_mixer",
    )(*args)
    return o.reshape(nb * t, D_MODEL), s_out


def _outproj_router_kernel(*refs, tiles_a, split_x):
    if split_x:
        oa_ref, ob_ref, xa_ref, xb_ref = refs[:4]
        refs = refs[4:]
    else:
        oa_ref, ob_ref, x_ref = refs[:3]
        refs = refs[3:]
    wout_ref, gain_ref, wr_ref, br_ref, x1_ref, h_ref, info_ref, cnt_ref, run_ref = refs
    first = pl.program_id(0) < tiles_a

    @pl.when(pl.program_id(0) == 0)
    def _():
        run_ref[...] = jnp.zeros_like(run_ref)

    o = jnp.where(first, oa_ref[...], ob_ref[...])
    x = jnp.where(first, xa_ref[...], xb_ref[...]) if split_x else x_ref[...]
    x1 = x + jnp.dot(o, wout_ref[...], preferred_element_type=F32)
    x1_ref[...] = x1
    h = _rms(x1, gain_ref[...])
    _store_token_tiles(h_ref, h)
    logit = _mm_split(h, wr_ref[...]) + br_ref[...]
    lane = lax.broadcasted_iota(I32, logit.shape, 1)
    far = jnp.int32(LANES)
    lg = jnp.where(lane < N_GROUPS, logit, NEG)
    mg = jnp.max(lg, axis=-1, keepdims=True)
    p_g = 1.0 / jnp.sum(jnp.exp(lg - mg), axis=-1, keepdims=True)
    g_top = jnp.min(jnp.where(lg == mg, lane, far), axis=-1, keepdims=True)
    lo = N_GROUPS + g_top * EXPERTS_PER_GROUP
    sel = jnp.logical_and(lane >= lo, lane < lo + EXPERTS_PER_GROUP)
    le = jnp.where(sel, logit, NEG)
    ee = jnp.exp(le - jnp.max(le, axis=-1, keepdims=True))
    pe = jnp.where(sel, ee / jnp.sum(ee, axis=-1, keepdims=True), -1.0)
    p1 = jnp.max(pe, axis=-1, keepdims=True)
    i1 = jnp.min(jnp.where(pe == p1, lane, far), axis=-1, keepdims=True)
    pe2 = jnp.where(lane == i1, -1.0, pe)
    p2 = jnp.max(pe2, axis=-1, keepdims=True)
    i2 = jnp.min(jnp.where(pe2 == p2, lane, far), axis=-1, keepdims=True)
    denom = p1 + p2
    w1 = p_g * (p1 / denom)
    w2 = p_g * (p2 / denom)
    tm = logit.shape[0]
    oh1 = jnp.where(lane == i1, 1.0, 0.0)
    oh2 = jnp.where(lane == i2, 1.0, 0.0)
    both = oh1 + oh2
    tr = lax.broadcasted_iota(I32, (tm, tm), 0)
    tc = lax.broadcasted_iota(I32, (tm, tm), 1)
    before = run_ref[...] + _mm(jnp.where(tr > tc, 1.0, 0.0), both)
    r1 = jnp.sum(oh1 * before, axis=-1, keepdims=True)
    r2 = jnp.sum(oh2 * before, axis=-1, keepdims=True)
    run = run_ref[...] + jnp.sum(both, axis=0, keepdims=True)
    run_ref[...] = run
    cnt_ref[...] = jnp.broadcast_to(run, cnt_ref.shape)
    vals = ((i1 - N_GROUPS).astype(F32), (i2 - N_GROUPS).astype(F32), w1, w2, r1, r2)
    info = jnp.zeros_like(logit)
    for idx, val in enumerate(vals):
        info = jnp.where(lane == idx, val, info)
    info_ref[...] = info


def _outproj_router(o_a, o_b, xs, wout, gain, wr, br):
    n_a, d = o_a.shape
    n = n_a + o_b.shape[0]
    tm = TOKEN_TILE
    tiles_a = n_a // tm
    row = lambda w: pl.BlockSpec((tm, w), lambda i: (i, 0))
    seg_a = pl.BlockSpec((tm, d), lambda i: (jnp.minimum(i, tiles_a - 1), 0))
    seg_b = pl.BlockSpec((tm, d), lambda i: (jnp.maximum(i - tiles_a, 0), 0))
    const = lambda shape: pl.BlockSpec(shape, lambda i: (0,) * len(shape))
    split_x = len(xs) == 2
    return pl.pallas_call(
        functools.partial(_outproj_router_kernel, tiles_a=tiles_a, split_x=split_x),
        grid=(n // tm,),
        in_specs=[seg_a, seg_b] + ([seg_a, seg_b] if split_x else [row(d)])
        + [const((d, d)), const((1, d)), const((d, LANES)), const((1, LANES))],
        out_specs=[row(d), pl.BlockSpec((tm * ROW_TILE, LANES), lambda i: (i, 0)), row(LANES), const((SUBLANES, LANES))],
        out_shape=[jax.ShapeDtypeStruct((n, d), F32), jax.ShapeDtypeStruct((n * ROW_TILE, LANES), F32),
                   jax.ShapeDtypeStruct((n, LANES), F32), jax.ShapeDtypeStruct((SUBLANES, LANES), F32)],
        scratch_shapes=[pltpu.VMEM((1, LANES), F32)],
        compiler_params=_params(("arbitrary",)),
        name="outproj_router",
    )(o_a, o_b, *xs, wout, gain, wr, br)


def _dispatch_kernel(pos_ref, lo_ref, hi_ref, h_ref, xs_hbm, buf, zrow, sem, zsem):
    i = pl.program_id(0)
    n = pl.num_programs(0)
    tm = h_ref.shape[0] // ROW_TILE
    blk = tm * ROW_TILE
    slot = i % 2
    token = lambda r: pl.ds(pl.multiple_of(r * ROW_TILE, ROW_TILE), ROW_TILE)

    def tile_wait(s):
        for _ in range(TOP_K):
            pltpu.make_async_copy(buf.at[s], xs_hbm.at[pl.ds(0, blk)], sem.at[s]).wait()

    zblk = zrow.shape[0]

    def pad_rows(fn, tile_fn):
        for e in range(N_EXPERTS):
            lax.fori_loop(lo_ref[e], hi_ref[e], fn, 0)
        lax.fori_loop(hi_ref[N_EXPERTS - 1] // (zblk // ROW_TILE), xs_hbm.shape[0] // zblk, tile_fn, 0)

    def pad_start(r, carry):
        pltpu.make_async_copy(zrow.at[token(0)], xs_hbm.at[token(r)], zsem.at[0]).start()
        return carry

    def pad_wait(r, carry):
        pltpu.make_async_copy(zrow.at[token(0)], xs_hbm.at[token(0)], zsem.at[0]).wait()
        return carry

    def pad_tile_start(t, carry):
        pltpu.make_async_copy(zrow, xs_hbm.at[pl.ds(pl.multiple_of(t * zblk, zblk), zblk)], zsem.at[0]).start()
        return carry

    def pad_tile_wait(t, carry):
        pltpu.make_async_copy(zrow, xs_hbm.at[pl.ds(0, zblk)], zsem.at[0]).wait()
        return carry

    @pl.when(i == 0)
    def _():
        zrow[...] = jnp.zeros_like(zrow)
        pad_rows(pad_start, pad_tile_start)

    @pl.when(i >= 2)
    def _():
        tile_wait(slot)
    buf[slot] = h_ref[...]

    def body(r, carry):
        for k in range(TOP_K):
            row = pos_ref[TOP_K * (i * tm + r) + k]
            pltpu.make_async_copy(buf.at[slot, token(r)], xs_hbm.at[token(row)], sem.at[slot]).start(priority=k % 2)
        return carry
    lax.fori_loop(0, tm, body, 0, unroll=8)

    @pl.when(i == n - 1)
    def _():
        tile_wait(slot)

        @pl.when(n > 1)
        def _():
            tile_wait(1 - slot)
        pad_rows(pad_wait, pad_tile_wait)


def _dispatch(h, pos, pad_lo, pad_hi, n_rows):
    blk = TOKEN_TILE * ROW_TILE
    n = h.shape[0] // ROW_TILE
    grid_spec = pltpu.PrefetchScalarGridSpec(
        num_scalar_prefetch=3,
        grid=(n // TOKEN_TILE,),
        in_specs=[pl.BlockSpec((blk, LANES), lambda i, pos, lo, hi: (i, 0))],
        out_specs=pl.BlockSpec(memory_space=pl.ANY),
        scratch_shapes=[pltpu.VMEM((2, blk, LANES), F32), pltpu.VMEM((MOE_TILE * ROW_TILE, LANES), F32),
                        pltpu.SemaphoreType.DMA((2,)), pltpu.SemaphoreType.DMA((1,))],
    )
    return pl.pallas_call(
        _dispatch_kernel,
        grid_spec=grid_spec,
        out_shape=jax.ShapeDtypeStruct((n_rows * ROW_TILE, LANES), F32),
        compiler_params=_params(("arbitrary",)),
        name="moe_dispatch",
    )(pos, pad_lo, pad_hi, h)


def _moe_kernel(te_ref, nact_ref, nxt_ref, par_ref, x_ref, w1_hbm, w3_hbm, w2_hbm, y_ref,
                w1b, w3b, w2b, w1f, w3f, w2f, wsem, *, layer):
    i = pl.program_id(0)

    def weight_copies(e, s):
        return [pltpu.make_async_copy(src.at[layer, e], dst.at[s], wsem.at[s])
                for src, dst in ((w1_hbm, w1f), (w3_hbm, w3f), (w2_hbm, w2f))]

    @pl.when(i < nact_ref[0])
    def _():
        @pl.when(jnp.logical_or(i == 0, te_ref[i] != te_ref[jnp.maximum(i - 1, 0)]))
        def _():
            s = par_ref[i]

            @pl.when(i == 0)
            def _():
                for cp in weight_copies(te_ref[0], s):
                    cp.start()
            for cp in weight_copies(te_ref[i], s):
                cp.wait()
            w1b[...] = w1f[s].astype(BF16)
            w3b[...] = w3f[s].astype(BF16)
            w2b[...] = w2f[s].astype(BF16)

            @pl.when(nxt_ref[i] >= 0)
            def _():
                for cp in weight_copies(nxt_ref[i], 1 - s):
                    cp.start()

        xb = _load_token_tiles(x_ref).astype(BF16)
        h1 = jnp.dot(xb, w1b[...], preferred_element_type=F32)
        h3 = jnp.dot(xb, w3b[...], preferred_element_type=F32)
        act = (h1 * _sigmoid(h1)) * h3
        _store_token_tiles(y_ref, jnp.dot(act.astype(BF16), w2b[...], preferred_element_type=F32))

    @pl.when(i >= nact_ref[0])
    def _():
        y_ref[...] = jnp.zeros_like(y_ref)


def _moe(xs, te, nact, nxt, par, w1, w3, w2, layer):
    d, f = w1.shape[2], w1.shape[3]
    tm = MOE_TILE
    blk = tm * ROW_TILE
    n_tiles = xs.shape[0] // blk
    last = lambda i, nact: jnp.maximum(jnp.minimum(i, nact[0] - 1), 0)
    grid_spec = pltpu.PrefetchScalarGridSpec(
        num_scalar_prefetch=4,
        grid=(n_tiles,),
        in_specs=[
            pl.BlockSpec((blk, LANES), lambda i, te, nact, nxt, par: (last(i, nact), 0)),
            pl.BlockSpec(memory_space=pl.ANY),
            pl.BlockSpec(memory_space=pl.ANY),
            pl.BlockSpec(memory_space=pl.ANY),
        ],
        out_specs=pl.BlockSpec((blk, LANES), lambda i, te, nact, nxt, par: (i, 0)),
        scratch_shapes=[pltpu.VMEM((d, f), BF16), pltpu.VMEM((d, f), BF16), pltpu.VMEM((f, d), BF16),
                        pltpu.VMEM((2, d, f), F32), pltpu.VMEM((2, d, f), F32), pltpu.VMEM((2, f, d), F32),
                        pltpu.SemaphoreType.DMA((2,))],
    )
    return pl.pallas_call(
        functools.partial(_moe_kernel, layer=layer),
        grid_spec=grid_spec,
        out_shape=jax.ShapeDtypeStruct(xs.shape, F32),
        compiler_params=_params(("arbitrary",)),
        name="moe_experts",
    )(te, nact, nxt, par, xs, w1, w3, w2)


def _moe_rows(n):
    return ((TOP_K * n + N_EXPERTS * (MOE_TILE - 1)) // MOE_TILE) * MOE_TILE


def _route_tables(info, counts_row, n):
    tm = MOE_TILE
    n_tiles = _moe_rows(n) // tm
    ids = jnp.arange(N_EXPERTS, dtype=I32)
    counts = counts_row[0, N_GROUPS:N_GROUPS + N_EXPERTS].astype(I32)
    padded = ((counts + tm - 1) // tm) * tm
    cum = jnp.cumsum(padded)
    off = cum - padded
    e = info[:, INFO_E:INFO_E + TOP_K].astype(I32)
    rank = info[:, INFO_R:INFO_R + TOP_K].astype(I32)
    pos = (jnp.sum(jnp.where(e[:, :, None] == ids, off, 0), axis=-1) + rank).reshape(TOP_K * n)
    tile_start = jnp.arange(n_tiles, dtype=I32) * tm
    nact = cum[-1] // tm
    te_raw = jnp.sum((cum[None, :] <= tile_start[:, None]).astype(I32), axis=1)
    te_last = jnp.sum((cum <= (nact - 1) * tm).astype(I32))
    te = jnp.where(tile_start < cum[-1], jnp.minimum(te_raw, N_EXPERTS - 1), te_last)
    cum_te = jnp.sum(jnp.where(te[:, None] == ids, cum, 0), axis=-1)
    nxt = jnp.where(cum_te < cum[-1], jnp.minimum(jnp.sum((cum[None, :] <= cum_te[:, None]).astype(I32), axis=1),
                                                  N_EXPERTS - 1), -1)
    par = jnp.sum(jnp.logical_and(ids[None, :] < te[:, None], padded[None, :] > 0).astype(I32), axis=1) % 2
    return pos, off + counts, cum, te, nact.reshape(1), nxt.astype(I32), par.astype(I32)


def _pad_lanes(a, width=LANES):
    return jnp.pad(a, [(0, 0)] * (a.ndim - 1) + [(0, width - a.shape[-1])])


def _seq_group(nb, want):
    g = want
    while nb % g:
        g //= 2
    return g


PROMPT_SEQS_PER_STEP = 8
SAMPLE_SEQS_PER_STEP = 16


def kernel(x_prompt, x_sample, state_conv_ab, state_delta_ab, state_gla_ab, state_hgrn_c, mix_norm, ab_w_in, ab_conv_w, ab_a_log, ab_dt_bias, ab_dn_norm, ab_gk_w2, ab_gk_b, ab_gla_norm, ab_w_out, c_w_in, c_lower_bounds, c_norm, c_w_out, ffn_norm, moe_w_group, moe_b_group, moe_w_expert, moe_b_expert, moe_w1, moe_w3, moe_w2, final_norm):
    bp, tp, d = x_prompt.shape
    bs, ts, _ = x_sample.shape
    n_p, n_s = bp * tp, bs * ts
    n = n_p + n_s
    depth = mix_norm.shape[0]
    xs_res = (x_prompt.reshape(n_p, d), x_sample.reshape(n_s, d))
    pg, sg = _seq_group(bp, PROMPT_SEQS_PER_STEP), _seq_group(bs, SAMPLE_SEQS_PER_STEP)

    info = pos = ys = None
    conv_p, delta_p, gla_p, hgrn_p = [], [], [], []
    conv_s, delta_s, gla_s, hgrn_s = [], [], [], []
    for layer in range(depth):
        j = layer // 2
        gain = mix_norm[layer].reshape(1, d)
        if layer % 2 == 0:
            u_w, a_w, b_w, z_w, gq_w, gk_w, gv_w, lr_w, gg_w = jnp.split(ab_w_in[j], _split_points(AB_SPLITS), axis=1)
            w_in = jnp.concatenate([u_w, _pad_lanes(jnp.concatenate([a_w, b_w, lr_w], axis=1)), z_w, gq_w, gk_w, gv_w, gg_w],
                                   axis=1).astype(BF16)
        else:
            w_in = c_w_in[j].astype(BF16)
        if ys is not None:
            xs_res = (_combine(xs_res[0], info, pos, ys),)
        (x_p, off_p), (x_s, off_s) = ((xs_res[0], 0), (xs_res[-1], 0 if len(xs_res) == 2 else n_p))

        if layer % 2 == 0:
            prm = jnp.concatenate([_pad_lanes(ab_a_log[j].reshape(1, -1)), _pad_lanes(ab_dt_bias[j].reshape(1, -1)),
                                   ab_dn_norm[j].reshape(1, -1), ab_gla_norm[j].reshape(1, -1),
                                   jnp.zeros((4, LANES), F32)], axis=0)
            gkw = jnp.zeros((LANES, GLA_K), F32).at[MISC_LR:MISC_LR + GLA_RANK].set(ab_gk_w2[j])
            gkb = ab_gk_b[j].reshape(1, GLA_K)
            cw = ab_conv_w[j]
            o_p, c8, sd, sgl = _ab_mixer(x_p, off_p, bp, tp, pg, gain, w_in, cw, prm, gkw, gkb, None)
            conv_p.append(c8[:, SUBLANES - (CONV_W - 1):])
            delta_p.append(sd)
            gla_p.append(sgl)
            conv8 = jnp.pad(state_conv_ab[j], ((0, 0), (SUBLANES - (CONV_W - 1), 0), (0, 0)))
            o_s, c8, sd, sgl = _ab_mixer(x_s, off_s, bs, ts, sg, gain, w_in, cw, prm, gkw, gkb,
                                         (conv8, state_delta_ab[j], state_gla_ab[j]))
            conv_s.append(c8[:, SUBLANES - (CONV_W - 1):])
            delta_s.append(sd)
            gla_s.append(sgl)
            w_out = ab_w_out[j]
        else:
            hg_gain = c_norm[j].reshape(1, -1)
            o_p, sh = _hgrn_mixer(x_p, off_p, bp, tp, pg, gain, w_in, c_lower_bounds, hg_gain, None, layer)
            hgrn_p.append(sh)
            o_s, sh = _hgrn_mixer(x_s, off_s, bs, ts, sg, gain, w_in, c_lower_bounds, hg_gain, state_hgrn_c[j], layer)
            hgrn_s.append(sh)
            w_out = c_w_out[j]

        wr = _pad_lanes(jnp.concatenate([moe_w_group[layer], moe_w_expert[layer]], axis=1))
        br = _pad_lanes(jnp.concatenate([moe_b_group[layer], moe_b_expert[layer]]).reshape(1, -1))
        x, hn, info, counts = _outproj_router(o_p, o_s, xs_res, w_out.astype(BF16), ffn_norm[layer].reshape(1, d), wr, br)
        xs_res = (x,)
        pos, pad_lo, pad_hi, te, nact, nxt, par = _route_tables(info, counts, n)
        xsort = _dispatch(hn, pos, pad_lo, pad_hi, _moe_rows(n))
        ys = _moe(xsort, te, nact, nxt, par, moe_w1, moe_w3, moe_w2, layer)

    y_p, y_s = _combine(xs_res[0], info, pos, ys, final_gain=final_norm.reshape(1, d), n_a=n_p)
    y_prompt = y_p.reshape(bp, tp, d)
    y_sample = y_s.reshape(bs, ts, d)
    return (y_prompt, y_sample, jnp.stack(conv_p), jnp.stack(delta_p), jnp.stack(gla_p), jnp.stack(hgrn_p),
            jnp.stack(conv_s), jnp.stack(delta_s), jnp.stack(gla_s), jnp.stack(hgrn_s))


def _split_points(sizes):
    pts, acc = [], 0
    for s in sizes[:-1]:
        acc += s
        pts.append(acc)
    return pts
```
